```python
import jax, jax.numpy as jnp
from jax import lax
import numpy as np

D_MODEL = 4096
BATCH = 4
SEQ = 2048
DEPTH = 4
DEC_BATCH = 32
DEC_SEQ = 8
PAST_LEN = 8192
PAGE_SIZE = 128

MIX_WIDTH = D_MODEL
NORM_EPS = 1e-6
L2_EPS = 1e-6
RWKV_WIDTH = D_MODEL // 4
RWKV_HEAD_DIM = 64
RWKV_HEADS = RWKV_WIDTH // RWKV_HEAD_DIM
DECAY_LORA = 64
AAA_LORA = 64
GATE_LORA = 128
RWKV_PROJ = 3 * RWKV_WIDTH + DECAY_LORA + AAA_LORA + GATE_LORA
RWKV_SPLITS = [RWKV_WIDTH, 2 * RWKV_WIDTH, 3 * RWKV_WIDTH, 3 * RWKV_WIDTH + DECAY_LORA,
               3 * RWKV_WIDTH + DECAY_LORA + AAA_LORA]
RWKV_LN_EPS = 64e-5
SWA_WIDTH = D_MODEL // 4
SWA_HEAD_DIM = 64
SWA_HEADS = SWA_WIDTH // SWA_HEAD_DIM
SWA_KV_HEADS = SWA_HEADS // 8
SWA_GROUP = SWA_HEADS // SWA_KV_HEADS
SWA_KV_WIDTH = SWA_KV_HEADS * SWA_HEAD_DIM
SWA_PROJ = SWA_WIDTH + 2 * SWA_KV_WIDTH
WINDOW = 128
SWA_BLOCK = WINDOW
SWA_SCALE = SWA_HEAD_DIM ** -0.5
GDN_WIDTH = MIX_WIDTH - RWKV_WIDTH - SWA_WIDTH
GDN_HEAD_DIM = 128
GDN_HEADS = GDN_WIDTH // GDN_HEAD_DIM
GDN_CONV_DIM = 3 * GDN_WIDTH
CONV_WIDTH = 4
GDN_CHUNK = 64
GDN_PROJ = GDN_CONV_DIM + GDN_WIDTH + 2 * GDN_HEADS
GDN_SPLITS = [GDN_CONV_DIM, GDN_CONV_DIM + GDN_WIDTH, GDN_CONV_DIM + GDN_WIDTH + GDN_HEADS]
IN_PROJ = RWKV_PROJ + SWA_PROJ + GDN_PROJ
D_FF = -(-8 * D_MODEL // (3 * 256)) * 256

kernel_name = "hymba_rwkv7_swa_sink_gdn_step"


def rms_norm(x, w, eps=NORM_EPS):
    xf = x.astype(jnp.float32)
    y = xf * lax.rsqrt(jnp.mean(xf * xf, axis=-1, keepdims=True) + eps)
    return (y * w.astype(jnp.float32)).astype(x.dtype)


def l2_normalize(x, eps=L2_EPS):
    x = x.astype(jnp.float32)
    return x * lax.rsqrt(jnp.sum(x * x, axis=-1, keepdims=True) + eps)


def rwkv7_time_mix(p, prev_row, S0, shift_mix, w0, w2, a0, a2, g2, k_k, k_a, r_k, ln_w, ln_b):
    B, T, _ = p.shape
    H, N = RWKV_HEADS, RWKV_HEAD_DIM
    pf = p.astype(jnp.float32)
    p_prev = jnp.concatenate([prev_row[:, None, :].astype(jnp.float32), pf[:, :-1]], axis=1)
    xs = pf + (p_prev - pf) * shift_mix
    r, k, v, w_lo, a_lo, g_lo = jnp.split(xs, RWKV_SPLITS, axis=-1)
    w = -jax.nn.softplus(-(w0 + jnp.tanh(w_lo) @ w2)) - 0.5
    decay = jnp.exp(-jnp.exp(w))
    a = jax.nn.sigmoid(a0 + a_lo @ a2)
    g = jax.nn.sigmoid(g_lo) @ g2
    kk = l2_normalize((k * k_k).reshape(B, T, H, N))
    k = k * (1.0 + (a - 1.0) * k_a)
    r_h, k_h, v_h, d_h, a_h = [t.reshape(B, T, H, N) for t in (r, k, v, decay, a)]
    b_h = kk * a_h

    def step(S, inp):
        r_t, k_t, v_t, d_t, kk_t, b_t = inp
        sa = jnp.einsum('bhij,bhj->bhi', S, -kk_t)
        S = S * d_t[:, :, None, :] + sa[..., None] * b_t[:, :, None, :] + v_t[..., None] * k_t[:, :, None, :]
        return S, jnp.einsum('bhij,bhj->bhi', S, r_t)

    seq = tuple(jnp.moveaxis(t, 1, 0) for t in (r_h, k_h, v_h, d_h, kk, b_h))
    S, y = lax.scan(step, S0.astype(jnp.float32), seq)
    y = jnp.moveaxis(y, 0, 1)
    mu = jnp.mean(y, axis=-1, keepdims=True)
    var = jnp.mean(jnp.square(y - mu), axis=-1, keepdims=True)
    y = (y - mu) * lax.rsqrt(var + RWKV_LN_EPS) * ln_w.reshape(H, N) + ln_b.reshape(H, N)
    y = y + jnp.sum(r_h * k_h * r_k, axis=-1, keepdims=True) * v_h
    out = (y.reshape(B, T, RWKV_WIDTH) * g).astype(p.dtype)
    return out, p[:, -1], S


def sink_softmax(s, mask, sink):
    s = jnp.where(mask, s, -jnp.inf)
    m = jnp.maximum(jnp.max(s, axis=-1, keepdims=True), sink)
    e = jnp.exp(s - m)
    return e / (jnp.sum(e, axis=-1, keepdims=True) + jnp.exp(sink - m))


def swa_prompt(q, k, v, sinks):
    B, T = q.shape[:2]
    nb = T // SWA_BLOCK
    qb = q.reshape(B, nb, SWA_BLOCK, SWA_KV_HEADS, SWA_GROUP, SWA_HEAD_DIM)
    kb = k.reshape(B, nb, SWA_BLOCK, SWA_KV_HEADS, SWA_HEAD_DIM)
    vb = v.reshape(B, nb, SWA_BLOCK, SWA_KV_HEADS, SWA_HEAD_DIM)
    prev = lambda t: jnp.concatenate([jnp.zeros_like(t[:, :1]), t[:, :-1]], axis=1)
    k2 = jnp.concatenate([prev(kb), kb], axis=2)
    v2 = jnp.concatenate([prev(vb), vb], axis=2)
    blk = jnp.arange(nb)[:, None] * SWA_BLOCK
    qpos = blk + jnp.arange(SWA_BLOCK)[None]
    kpos = blk - SWA_BLOCK + jnp.arange(2 * SWA_BLOCK)[None]
    delta = qpos[:, :, None] - kpos[:, None, :]
    mask = (delta >= 0) & (delta <= WINDOW) & (kpos[:, None, :] >= 0)
    s = jnp.einsum('bnqkgd,bnskd->bnkgqs', qb, k2, preferred_element_type=jnp.float32) * SWA_SCALE
    sink = sinks.astype(jnp.float32).reshape(SWA_KV_HEADS, SWA_GROUP)[None, None, :, :, None, None]
    pr = sink_softmax(s, mask[None, :, None, None], sink)
    o = jnp.einsum('bnkgqs,bnskd->bnqkgd', pr, v2.astype(jnp.float32))
    return o.reshape(B, T, SWA_WIDTH).astype(q.dtype)


def swa_decode(q, k, v, k_buf, v_buf, sinks):
    B, T = q.shape[:2]
    wb = k_buf.shape[1]
    k_all = jnp.concatenate([k_buf.astype(k.dtype), k], axis=1)
    v_all = jnp.concatenate([v_buf.astype(v.dtype), v], axis=1)
    delta = (wb + jnp.arange(T))[:, None] - jnp.arange(wb + T)[None, :]
    mask = (delta >= 0) & (delta <= WINDOW)
    qg = q.reshape(B, T, SWA_KV_HEADS, SWA_GROUP, SWA_HEAD_DIM)
    s = jnp.einsum('btkgd,bskd->bkgts', qg, k_all, preferred_element_type=jnp.float32) * SWA_SCALE
    sink = sinks.astype(jnp.float32).reshape(SWA_KV_HEADS, SWA_GROUP)[None, :, :, None, None]
    pr = sink_softmax(s, mask, sink)
    o = jnp.einsum('bkgts,bskd->btkgd', pr, v_all.astype(jnp.float32))
    return o.reshape(B, T, SWA_WIDTH).astype(q.dtype), k_all[:, -wb:], v_all[:, -wb:]


def causal_dwconv_silu(x, buf, w):
    xp = jnp.concatenate([buf.astype(x.dtype), x], axis=1)
    y = lax.conv_general_dilated(xp, w[:, None, :].astype(x.dtype), window_strides=(1,), padding='VALID',
                                 dimension_numbers=('NWC', 'WIO', 'NWC'), feature_group_count=x.shape[-1])
    return jax.nn.silu(y), xp[:, -(CONV_WIDTH - 1):]


def gated_delta_rule(q, k, v, g, beta, S0):
    B, T, H, DK = q.shape
    DV = v.shape[-1]
    C = min(GDN_CHUNK, T)
    pad = (-T) % C
    padT = lambda t: jnp.pad(t, [(0, 0), (0, pad)] + [(0, 0)] * (t.ndim - 2))
    q, k, v, g, beta = map(padT, (q, k, v, g, beta))
    N = (T + pad) // C
    qc, kc, vc = [t.reshape(B, N, C, H, -1).transpose(1, 0, 3, 2, 4) for t in (q, k, v)]
    gc, bc = [t.reshape(B, N, C, H).transpose(1, 0, 3, 2) for t in (g, beta)]
    gcum = jnp.cumsum(gc, axis=-1)
    idx = jnp.arange(C)
    incl = idx[:, None] >= idx[None, :]
    strict = idx[:, None] > idx[None, :]
    decay = jnp.exp(jnp.where(incl, gcum[..., :, None] - gcum[..., None, :], -jnp.inf))
    kb = kc * bc[..., None]
    A = jnp.where(strict, jnp.einsum('nbhcd,nbhsd->nbhcs', kb, kc) * decay, 0.0)
    eye = jnp.eye(C, dtype=jnp.float32)
    Tinv = lax.linalg.triangular_solve(eye + A, jnp.broadcast_to(eye, A.shape), left_side=True, lower=True)
    u = jnp.einsum('nbhcs,nbhse->nbhce', Tinv, vc * bc[..., None])
    w = jnp.einsum('nbhcs,nbhsd->nbhcd', Tinv, kb * jnp.exp(gcum)[..., None])
    attn = jnp.where(incl, jnp.einsum('nbhcd,nbhsd->nbhcs', qc, kc) * decay, 0.0)

    def step(S, inp):
        q_c, k_c, u_c, w_c, g_c, a_c = inp
        v_new = u_c - jnp.einsum('bhcd,bhde->bhce', w_c, S)
        o = (jnp.einsum('bhcd,bhde->bhce', q_c * jnp.exp(g_c)[..., None], S)
             + jnp.einsum('bhcs,bhse->bhce', a_c, v_new))
        g_last = g_c[..., -1:]
        S = (S * jnp.exp(g_last)[..., None]
             + jnp.einsum('bhcd,bhce->bhde', k_c * jnp.exp(g_last - g_c)[..., None], v_new))
        return S, o

    S, o = lax.scan(step, S0.astype(jnp.float32), (qc, kc, u, w, gcum, attn))
    o = o.transpose(1, 0, 3, 2, 4).reshape(B, N * C, H, DV)[:, :T]
    return o, S


def gated_deltanet_mix(p, conv_buf, S0, conv_w, A_log, dt_bias, norm_w):
    B, T, _ = p.shape
    H, Dh = GDN_HEADS, GDN_HEAD_DIM
    qkv, z, b_raw, a_raw = jnp.split(p, GDN_SPLITS, axis=-1)
    qkv, new_buf = causal_dwconv_silu(qkv, conv_buf, conv_w)
    q, k, v = [t.reshape(B, T, H, Dh) for t in jnp.split(qkv, 3, axis=-1)]
    q = l2_normalize(q) * (Dh ** -0.5)
    k = l2_normalize(k)
    beta = jax.nn.sigmoid(b_raw.astype(jnp.float32))
    g = -jnp.exp(A_log.astype(jnp.float32)) * jax.nn.softplus(a_raw.astype(jnp.float32) + dt_bias)
    o, S = gated_delta_rule(q, k, v.astype(jnp.float32), g, beta, S0)
    o = rms_norm(o, norm_w) * jax.nn.silu(z.astype(jnp.float32).reshape(B, T, H, Dh))
    return o.reshape(B, T, GDN_WIDTH).astype(p.dtype), new_buf, S


def run_trunk(x, W, win_buf, past):
    B, T, _ = x.shape
    h = x
    new = [[] for _ in range(6)]
    for l in range(DEPTH):
        u = rms_norm(h, W['norm_mix'][l])
        p = u @ W['w_in'][l]
        p_rwkv, p_swa, p_gdn = jnp.split(p, [RWKV_PROJ, RWKV_PROJ + SWA_PROJ], axis=-1)
        if past is None:
            shift0 = jnp.zeros((B, RWKV_PROJ), x.dtype)
            wkv0 = jnp.zeros((B, RWKV_HEADS, RWKV_HEAD_DIM, RWKV_HEAD_DIM), jnp.float32)
            conv0 = jnp.zeros((B, CONV_WIDTH - 1, GDN_CONV_DIM), x.dtype)
            ssm0 = jnp.zeros((B, GDN_HEADS, GDN_HEAD_DIM, GDN_HEAD_DIM), jnp.float32)
        else:
            shift0, wkv0, kbuf, vbuf, conv0, ssm0 = (t[l] for t in past)
        o_rwkv, shift1, wkv1 = rwkv7_time_mix(
            p_rwkv, shift0, wkv0, W['rwkv_shift_mix'][l], W['rwkv_w0'][l], W['rwkv_w2'][l],
            W['rwkv_a0'][l], W['rwkv_a2'][l], W['rwkv_g2'][l], W['rwkv_k_k'][l], W['rwkv_k_a'][l],
            W['rwkv_r_k'][l], W['rwkv_ln_w'][l], W['rwkv_ln_b'][l])
        q, k, v = jnp.split(p_swa, [SWA_WIDTH, SWA_WIDTH + SWA_KV_WIDTH], axis=-1)
        q = q.reshape(B, T, SWA_HEADS, SWA_HEAD_DIM)
        k = k.reshape(B, T, SWA_KV_HEADS, SWA_HEAD_DIM)
        v = v.reshape(B, T, SWA_KV_HEADS, SWA_HEAD_DIM)
        if past is None:
            o_swa = swa_prompt(q, k, v, W['swa_sinks'][l])
            k1, v1 = k[:, T - win_buf:], v[:, T - win_buf:]
        else:
            o_swa, k1, v1 = swa_decode(q, k, v, kbuf, vbuf, W['swa_sinks'][l])
        o_gdn, conv1, ssm1 = gated_deltanet_mix(p_gdn, conv0, ssm0, W['gdn_conv_w'][l], W['gdn_A_log'][l],
                                                W['gdn_dt_bias'][l], W['gdn_norm_w'][l])
        h = h + jnp.concatenate([o_rwkv, o_swa, o_gdn], axis=-1) @ W['w_out'][l]
        u = rms_norm(h, W['norm_ffn'][l])
        h = h + (jax.nn.silu(u @ W['w_gate'][l]) * (u @ W['w_up'][l])) @ W['w_down'][l]
        for lst, t in zip(new, (shift1, wkv1, k1, v1, conv1, ssm1)):
            lst.append(t.astype(x.dtype))
    y = rms_norm(h, W['final_norm'])
    return y, tuple(jnp.stack(lst) for lst in new)


def setup_inputs(seed: int = 0) -> dict:
    key = jax.random.key(seed)
    ks = iter(jax.random.split(key, 48))
    nrm = lambda shape, scale: scale * jax.random.normal(next(ks), shape, jnp.float32)
    unif = lambda shape, lo, hi: jax.random.uniform(next(ks), shape, jnp.float32, lo, hi)
    wb = min(WINDOW, PAST_LEN)
    L = DEPTH
    dt = jnp.exp(unif((L, GDN_HEADS), float(np.log(1e-3)), float(np.log(1e-1))))
    return {
        'x_prompt': nrm((BATCH, SEQ, D_MODEL), 1.0),
        'x_sample': nrm((DEC_BATCH, DEC_SEQ, D_MODEL), 1.0),
        'state_rwkv_shift': nrm((L, DEC_BATCH, RWKV_PROJ), 1.0),
        'state_rwkv_wkv': nrm((L, DEC_BATCH, RWKV_HEADS, RWKV_HEAD_DIM, RWKV_HEAD_DIM), 0.5),
        'cache_swa_k': nrm((L, DEC_BATCH, wb, SWA_KV_HEADS, SWA_HEAD_DIM), 1.0),
        'cache_swa_v': nrm((L, DEC_BATCH, wb, SWA_KV_HEADS, SWA_HEAD_DIM), 1.0),
        'state_gdn_conv': nrm((L, DEC_BATCH, CONV_WIDTH - 1, GDN_CONV_DIM), 1.0),
        'state_gdn_ssm': nrm((L, DEC_BATCH, GDN_HEADS, GDN_HEAD_DIM, GDN_HEAD_DIM), 0.1),
        'norm_mix': 1.0 + nrm((L, D_MODEL), 0.05),
        'w_in': nrm((L, D_MODEL, IN_PROJ), D_MODEL ** -0.5),
        'rwkv_shift_mix': unif((L, RWKV_PROJ), 0.0, 1.0),
        'rwkv_w0': unif((L, RWKV_WIDTH), -6.0, -1.0),
        'rwkv_w2': nrm((L, DECAY_LORA, RWKV_WIDTH), 0.5 * DECAY_LORA ** -0.5),
        'rwkv_a0': nrm((L, RWKV_WIDTH), 0.1),
        'rwkv_a2': nrm((L, AAA_LORA, RWKV_WIDTH), 0.5 * AAA_LORA ** -0.5),
        'rwkv_g2': nrm((L, GATE_LORA, RWKV_WIDTH), GATE_LORA ** -0.5),
        'rwkv_k_k': 0.85 + nrm((L, RWKV_WIDTH), 0.05),
        'rwkv_k_a': 1.0 + nrm((L, RWKV_WIDTH), 0.05),
        'rwkv_r_k': nrm((L, RWKV_HEADS, RWKV_HEAD_DIM), 0.1),
        'rwkv_ln_w': 1.0 + nrm((L, RWKV_WIDTH), 0.05),
        'rwkv_ln_b': nrm((L, RWKV_WIDTH), 0.01),
        'swa_sinks': nrm((L, SWA_HEADS), 0.5),
        'gdn_conv_w': nrm((L, CONV_WIDTH, GDN_CONV_DIM), CONV_WIDTH ** -0.5),
        'gdn_A_log': jnp.log(unif((L, GDN_HEADS), 1.0, 16.0)),
        'gdn_dt_bias': jnp.log(jnp.expm1(dt)),
        'gdn_norm_w': 1.0 + nrm((L, GDN_HEAD_DIM), 0.05),
        'w_out': nrm((L, MIX_WIDTH, D_MODEL), MIX_WIDTH ** -0.5),
        'norm_ffn': 1.0 + nrm((L, D_MODEL), 0.05),
        'w_gate': nrm((L, D_MODEL, D_FF), D_MODEL ** -0.5),
        'w_up': nrm((L, D_MODEL, D_FF), D_MODEL ** -0.5),
        'w_down': nrm((L, D_FF, D_MODEL), D_FF ** -0.5),
        'final_norm': 1.0 + nrm((D_MODEL,), 0.05),
    }


def reference(x_prompt, x_sample, state_rwkv_shift, state_rwkv_wkv, cache_swa_k, cache_swa_v,
              state_gdn_conv, state_gdn_ssm, norm_mix, w_in, rwkv_shift_mix, rwkv_w0, rwkv_w2,
              rwkv_a0, rwkv_a2, rwkv_g2, rwkv_k_k, rwkv_k_a, rwkv_r_k, rwkv_ln_w, rwkv_ln_b,
              swa_sinks, gdn_conv_w, gdn_A_log, gdn_dt_bias, gdn_norm_w, w_out, norm_ffn,
              w_gate, w_up, w_down, final_norm):
    W = dict(norm_mix=norm_mix, w_in=w_in, rwkv_shift_mix=rwkv_shift_mix, rwkv_w0=rwkv_w0,
             rwkv_w2=rwkv_w2, rwkv_a0=rwkv_a0, rwkv_a2=rwkv_a2, rwkv_g2=rwkv_g2, rwkv_k_k=rwkv_k_k,
             rwkv_k_a=rwkv_k_a, rwkv_r_k=rwkv_r_k, rwkv_ln_w=rwkv_ln_w, rwkv_ln_b=rwkv_ln_b,
             swa_sinks=swa_sinks, gdn_conv_w=gdn_conv_w, gdn_A_log=gdn_A_log, gdn_dt_bias=gdn_dt_bias,
             gdn_norm_w=gdn_norm_w, w_out=w_out, norm_ffn=norm_ffn, w_gate=w_gate, w_up=w_up,
             w_down=w_down, final_norm=final_norm)
    win_buf = cache_swa_k.shape[2]
    y_prompt, (p_sh, p_wkv, p_k, p_v, p_conv, p_ssm) = run_trunk(x_prompt, W, win_buf, None)
    y_sample, (s_sh, s_wkv, s_k, s_v, s_conv, s_ssm) = run_trunk(
        x_sample, W, win_buf,
        (state_rwkv_shift, state_rwkv_wkv, cache_swa_k, cache_swa_v, state_gdn_conv, state_gdn_ssm))
    return (y_prompt, y_sample, p_sh, p_wkv, p_k, p_v, p_conv, p_ssm,
            s_sh, s_wkv, s_k, s_v, s_conv, s_ssm)
```

```python
import functools

import jax
import jax.numpy as jnp
from jax import lax
from jax.experimental import pallas as pl
from jax.experimental.pallas import tpu as pltpu

D_MODEL = 4096
BATCH = 4
SEQ = 2048
DEPTH = 4
DEC_BATCH = 32
DEC_SEQ = 8
NORM_EPS = 1e-6
L2_EPS = 1e-6
RWKV_WIDTH = 1024
RWKV_HEAD_DIM = 64
RWKV_HEADS = 16
DECAY_LORA = 64
AAA_LORA = 64
GATE_LORA = 128
RWKV_PROJ = 3 * RWKV_WIDTH + DECAY_LORA + AAA_LORA + GATE_LORA
RWKV_LN_EPS = 64e-5
SWA_WIDTH = 1024
SWA_HEAD_DIM = 64
SWA_HEADS = 16
SWA_KV_HEADS = 2
SWA_GROUP = 8
SWA_KV_WIDTH = 128
SWA_PROJ = SWA_WIDTH + 2 * SWA_KV_WIDTH
WINDOW = 128
SWA_SCALE = SWA_HEAD_DIM ** -0.5
GDN_WIDTH = 2048
GDN_HEAD_DIM = 128
GDN_HEADS = 16
GDN_CONV_DIM = 3 * GDN_WIDTH
CONV_WIDTH = 4
GDN_MAIN = GDN_CONV_DIM + GDN_WIDTH
D_FF = 11008

ROWS_P = BATCH * SEQ
ROWS_D = DEC_BATCH * DEC_SEQ
ROWS = ROWS_P + ROWS_D

LANES = 128
SUBLANES = 8
RWKV_PAD = 3584
SWA_PAD = 1536
BA_COL = SWA_PROJ
VMEM_LIMIT = 56 * 1024 * 1024

F32 = jnp.float32
BF16 = jnp.bfloat16


def _cparams(n_axes):
    return pltpu.CompilerParams(dimension_semantics=("arbitrary",) * n_axes, vmem_limit_bytes=VMEM_LIMIT)


def _split(x):
    hi = x.astype(BF16)
    lo = (x - hi.astype(F32)).astype(BF16)
    return hi, lo


def _dot(a, b):
    return jnp.dot(a, b, preferred_element_type=F32)


def _dot_lhs2(x, m_bf16):
    hi, lo = _split(x)
    return _dot(hi, m_bf16) + _dot(lo, m_bf16)


def _dot3(x, w):
    xh, xl = _split(x)
    wh, wl = _split(w)
    return _dot(xh, wh) + (_dot(xl, wh) + _dot(xh, wl))


def _segsum(x, j_bf16):
    n = x.shape[-1] // LANES
    return jnp.concatenate([_dot_lhs2(x[:, c * LANES:(c + 1) * LANES], j_bf16) for c in range(n)], axis=-1)


def _sigmoid(x):
    return 1.0 / (1.0 + jnp.exp(-x))


def _softplus(x):
    return jnp.maximum(x, 0.0) + jnp.log(1.0 + jnp.exp(-jnp.abs(x)))


def _rmsnorm_kernel(x_ref, w_ref, o_ref):
    x = x_ref[...]
    ms = jnp.mean(x * x, axis=-1, keepdims=True)
    o_ref[...] = (x * lax.rsqrt(ms + NORM_EPS) * w_ref[...]).astype(o_ref.dtype)


def rmsnorm(x, w, out_dtype, br=256):
    r, d = x.shape
    return pl.pallas_call(
        _rmsnorm_kernel,
        grid=(r // br,),
        in_specs=[pl.BlockSpec((br, d), lambda i: (i, 0)), pl.BlockSpec((1, d), lambda i: (0, 0))],
        out_specs=pl.BlockSpec((br, d), lambda i: (i, 0)),
        out_shape=jax.ShapeDtypeStruct((r, d), out_dtype),
        compiler_params=_cparams(1),
        name="rmsnorm",
    )(x, w.reshape(1, d))


def _mm_kernel(a_ref, b_ref, o_ref):
    o_ref[...] = _dot(a_ref[...], b_ref[...]).astype(o_ref.dtype)


def matmul(a, w, layer, bm, bn, out_dtype=F32):
    r, k = a.shape
    n = w.shape[2]
    return pl.pallas_call(
        _mm_kernel,
        grid=(r // bm, n // bn),
        in_specs=[pl.BlockSpec((bm, k), lambda i, j: (i, 0)),
                  pl.BlockSpec((None, k, bn), lambda i, j: (layer, 0, j))],
        out_specs=pl.BlockSpec((bm, bn), lambda i, j: (i, j)),
        out_shape=jax.ShapeDtypeStruct((r, n), out_dtype),
        compiler_params=_cparams(2),
        name="proj_in",
    )(a, w)


def _swiglu_kernel(a_ref, g_ref, u_ref, o_ref):
    a = a_ref[...]
    g = _dot(a, g_ref[...])
    u = _dot(a, u_ref[...])
    o_ref[...] = (g * _sigmoid(g) * u).astype(o_ref.dtype)


def swiglu_matmul(a, wg, wu, layer, bm, bn):
    r, k = a.shape
    n = wg.shape[2]
    wspec = pl.BlockSpec((None, k, bn), lambda i, j: (layer, 0, j))
    return pl.pallas_call(
        _swiglu_kernel,
        grid=(r // bm, pl.cdiv(n, bn)),
        in_specs=[pl.BlockSpec((bm, k), lambda i, j: (i, 0)), wspec, wspec],
        out_specs=pl.BlockSpec((bm, bn), lambda i, j: (i, j)),
        out_shape=jax.ShapeDtypeStruct((r, n), BF16),
        compiler_params=_cparams(2),
        name="ffn_swiglu",
    )(a, wg, wu)


def _mm_res_kernel(*refs, n_a):
    a_refs, b_refs, res_ref, o_ref = refs[:n_a], refs[n_a:2 * n_a], refs[2 * n_a], refs[2 * n_a + 1]
    kk = pl.program_id(2)

    @pl.when(kk == 0)
    def _():
        o_ref[...] = res_ref[...]

    acc = _dot(a_refs[0][...], b_refs[0][...])
    for a_ref, b_ref in zip(a_refs[1:], b_refs[1:]):
        acc = acc + _dot(a_ref[...], b_ref[...])
    o_ref[...] += acc


def matmul_residual(a_list, w, layer, res, bm, bn, bk_list):
    r = res.shape[0]
    n = w.shape[2]
    nk = a_list[0].shape[1] // bk_list[0]
    in_specs, row0 = [], 0
    for a, bk in zip(a_list, bk_list):
        assert a.shape[1] == nk * bk
        in_specs.append(pl.BlockSpec((bm, bk), lambda i, j, k: (i, k)))
    for a, bk in zip(a_list, bk_list):
        assert row0 % bk == 0
        in_specs.append(pl.BlockSpec((None, bk, bn), functools.partial(
            lambda i, j, k, off: (layer, off + k, j), off=row0 // bk)))
        row0 += a.shape[1]
    in_specs.append(pl.BlockSpec((bm, bn), lambda i, j, k: (i, j)))
    return pl.pallas_call(
        functools.partial(_mm_res_kernel, n_a=len(a_list)),
        grid=(r // bm, n // bn, nk),
        in_specs=in_specs,
        out_specs=pl.BlockSpec((bm, bn), lambda i, j, k: (i, j)),
        out_shape=jax.ShapeDtypeStruct((r, n), F32),
        compiler_params=_cparams(3),
        name="proj_residual",
    )(*a_list, *([w] * len(a_list)), res)


def _shifted(prev8, x, k):
    ext = jnp.concatenate([prev8, x], axis=0)
    return pltpu.roll(ext, k, axis=0)[SUBLANES:]


def _seq_specs(width, col_block, tb, n_tb, row0):
    assert row0 % tb == 0 and tb % SUBLANES == 0
    cur = pl.BlockSpec((tb, width), lambda b, t: (row0 // tb + b * n_tb + t, col_block))
    prev = pl.BlockSpec((SUBLANES, width), lambda b, t: (
        jnp.maximum((row0 + (b * n_tb + t) * tb) // SUBLANES - 1, 0), col_block))
    state = pl.BlockSpec((SUBLANES, width), lambda b, t: (b, 0))
    return cur, prev, state


def _row_spec(width, tb, n_tb, row0=0, col_block=0):
    return pl.BlockSpec((tb, width), lambda b, t: (row0 // tb + b * n_tb + t, col_block))


def _const_spec(shape):
    return pl.BlockSpec(shape, lambda b, t: (0,) * len(shape))


def _pad_state_rows(st):
    b, k, c = st.shape
    return jnp.pad(st, ((0, 0), (SUBLANES - k, 0), (0, 0))).reshape(b * SUBLANES, c)


def _rwkv_prep_kernel(cur_ref, prev_ref, st_ref, mix_ref, w0_ref, a0_ref, wwa_ref, g2_ref, kkw_ref, kaw_ref, j_ref,
                      r_o, kk_o, d_o, b_o, k_o, v_o, g_o):
    x = cur_ref[...]
    prev8 = jnp.where(pl.program_id(1) == 0, st_ref[...], prev_ref[...])
    xs = x + (_shifted(prev8, x, 1) - x) * mix_ref[...]
    w3 = RWKV_WIDTH
    r, k, v = xs[:, 0:w3], xs[:, w3:2 * w3], xs[:, 2 * w3:3 * w3]
    wa = xs[:, 3 * w3:3 * w3 + LANES]
    glo = xs[:, 3 * w3 + LANES:3 * w3 + 2 * LANES]
    lane = lax.broadcasted_iota(jnp.int32, wa.shape, 1)
    lora = _dot3(jnp.where(lane < DECAY_LORA, jnp.tanh(wa), wa), wwa_ref[...])
    w = -_softplus(-(w0_ref[...] + lora[:, :w3])) - 0.5
    d = jnp.exp(-jnp.exp(w))
    a = _sigmoid(a0_ref[...] + lora[:, w3:])
    g = _dot3(_sigmoid(glo), g2_ref[...])
    kn = k * kkw_ref[...]
    kk = kn * lax.rsqrt(_segsum(kn * kn, j_ref[...]) + L2_EPS)
    r_o[...] = r
    kk_o[...] = kk
    d_o[...] = d
    b_o[...] = kk * a
    k_o[...] = k * (1.0 + (a - 1.0) * kaw_ref[...])
    v_o[...] = v
    g_o[...] = g


def rwkv_prep(p_rwkv, st8, prm, n_seq, t_len, tb, row0):
    n_tb = t_len // tb
    cur, prev, state = _seq_specs(RWKV_PROJ, 0, tb, n_tb, row0)
    out = jax.ShapeDtypeStruct((n_seq * t_len, RWKV_WIDTH), F32)
    ospec = _row_spec(RWKV_WIDTH, tb, n_tb)
    return pl.pallas_call(
        _rwkv_prep_kernel,
        grid=(n_seq, n_tb),
        in_specs=[cur, prev, state, _const_spec((1, RWKV_PROJ)), _const_spec((1, RWKV_WIDTH)),
                  _const_spec((1, RWKV_WIDTH)), _const_spec((LANES, 2 * RWKV_WIDTH)),
                  _const_spec((GATE_LORA, RWKV_WIDTH)), _const_spec((1, RWKV_WIDTH)), _const_spec((1, RWKV_WIDTH)),
                  _const_spec((LANES, LANES))],
        out_specs=[ospec] * 7,
        out_shape=[out] * 7,
        compiler_params=_cparams(2),
        name="rwkv_prep",
    )(p_rwkv, p_rwkv, st8, prm["mix"], prm["w0"], prm["a0"], prm["wwa"], prm["g2"], prm["k_k"], prm["k_a"],
      prm["j64"])


def _rwkv_post_kernel(y_ref, r_ref, k_ref, v_ref, g_ref, lnw_ref, lnb_ref, rk_ref, j_ref, o_ref):
    j = j_ref[...]
    y = y_ref[...]
    inv_n = 1.0 / RWKV_HEAD_DIM
    yc = y - _segsum(y, j) * inv_n
    var = _segsum(yc * yc, j) * inv_n
    out = yc * lax.rsqrt(var + RWKV_LN_EPS) * lnw_ref[...] + lnb_ref[...]
    out = out + _segsum(r_ref[...] * k_ref[...] * rk_ref[...], j) * v_ref[...]
    o_ref[...] = (out * g_ref[...]).astype(o_ref.dtype)


def rwkv_post(y, r, k, v, g, prm, tb):
    rows = y.shape[0]
    spec = pl.BlockSpec((tb, RWKV_WIDTH), lambda i: (i, 0))
    cspec = pl.BlockSpec((1, RWKV_WIDTH), lambda i: (0, 0))
    return pl.pallas_call(
        _rwkv_post_kernel,
        grid=(rows // tb,),
        in_specs=[spec] * 5 + [cspec] * 3 + [pl.BlockSpec((LANES, LANES), lambda i: (0, 0))],
        out_specs=spec,
        out_shape=jax.ShapeDtypeStruct((rows, RWKV_WIDTH), BF16),
        compiler_params=_cparams(1),
        name="rwkv_post",
    )(y, r, k, v, g, prm["ln_w"], prm["ln_b"], prm["r_k"], prm["j64"])


def _scan_kernel(kk_ref, d_ref, b_ref, k_ref, r_ref, v_ref, s0_ref, y_ref, s_ref, *, tb, dj, di):
    ng = di // SUBLANES

    @pl.when(pl.program_id(1) == 0)
    def _():
        s_ref[...] = s0_ref[...]

    def row(ref, t, j):
        return ref[0, t, pl.ds(j, 1), :]

    def sl(g):
        return pl.ds(g * SUBLANES, SUBLANES)

    def step(t, carry):
        def p1(j, acc):
            kkj = row(kk_ref, t, j)
            return tuple(acc[g] + s_ref[0, j, sl(g), :] * kkj for g in range(ng))

        zero = tuple(jnp.zeros((SUBLANES, LANES), F32) for _ in range(ng))
        acc = lax.fori_loop(0, dj, p1, zero, unroll=8)
        sa = [-a for a in acc]
        v = [v_ref[0, t, sl(g), :] for g in range(ng)]

        def p2(j, yacc):
            dj_, bj, kj, rj = row(d_ref, t, j), row(b_ref, t, j), row(k_ref, t, j), row(r_ref, t, j)
            out = []
            for g in range(ng):
                s = s_ref[0, j, sl(g), :] * dj_ + sa[g] * bj + v[g] * kj
                s_ref[0, j, sl(g), :] = s
                out.append(yacc[g] + s * rj)
            return tuple(out)

        yacc = lax.fori_loop(0, dj, p2, zero, unroll=8)
        for g in range(ng):
            y_ref[0, t, sl(g), :] = yacc[g]
        return carry

    lax.fori_loop(0, tb, step, 0)


def dplr_scan(kk, d, b, k, r, v, s0, tb):
    g_n, t_len, dj, _ = kk.shape
    di = v.shape[2]
    jspec = pl.BlockSpec((1, tb, dj, LANES), lambda g, t: (g, t, 0, 0))
    ispec = pl.BlockSpec((1, tb, di, LANES), lambda g, t: (g, t, 0, 0))
    sspec = pl.BlockSpec((1, dj, di, LANES), lambda g, t: (g, 0, 0, 0))
    return pl.pallas_call(
        functools.partial(_scan_kernel, tb=tb, dj=dj, di=di),
        grid=(g_n, t_len // tb),
        in_specs=[jspec] * 5 + [ispec, sspec],
        out_specs=[ispec, sspec],
        out_shape=[jax.ShapeDtypeStruct(v.shape, F32), jax.ShapeDtypeStruct(s0.shape, F32)],
        compiler_params=_cparams(2),
        name="dplr_scan",
    )(kk, d, b, k, r, v, s0)


def _to_key_layout(x, n_seq, t_len, heads, dj, dup):
    x = x.reshape(n_seq, t_len, heads, dj).transpose(1, 3, 0, 2).reshape(t_len, dj, n_seq * heads)
    x = jnp.tile(x, (1, 1, dup))
    g_n = dup * n_seq * heads // LANES
    return x.reshape(t_len, dj, g_n, LANES).transpose(2, 0, 1, 3)


def _to_value_layout(x, n_seq, t_len, heads, dv, dup):
    di = dv // dup
    x = x.reshape(n_seq, t_len, heads, dup, di).transpose(1, 4, 3, 0, 2)
    g_n = dup * n_seq * heads // LANES
    return x.reshape(t_len, di, g_n, LANES).transpose(2, 0, 1, 3)


def _from_value_layout(y, n_seq, t_len, heads, dv, dup):
    g_n, _, di, _ = y.shape
    y = y.transpose(1, 2, 0, 3).reshape(t_len, di, dup, n_seq, heads)
    return y.transpose(3, 0, 4, 2, 1).reshape(n_seq * t_len, heads * dv)


def _state_to_layout(s, dup, value_major):
    bsz, heads = s.shape[:2]
    if value_major:
        dv, dj = s.shape[2:]
        s = s.reshape(bsz, heads, dup, dv // dup, dj).transpose(4, 3, 2, 0, 1)
    else:
        dj, dv = s.shape[2:]
        s = s.reshape(bsz, heads, dj, dup, dv // dup).transpose(2, 4, 3, 0, 1)
    g_n = dup * bsz * heads // LANES
    return s.reshape(dj, dv // dup, g_n, LANES).transpose(2, 0, 1, 3)


def _state_from_layout(s, bsz, heads, dup, value_major):
    g_n, dj, di, _ = s.shape
    s = s.transpose(1, 2, 0, 3).reshape(dj, di, dup, bsz, heads)
    if value_major:
        return s.transpose(3, 4, 2, 1, 0).reshape(bsz, heads, dup * di, dj)
    return s.transpose(3, 4, 0, 2, 1).reshape(bsz, heads, dj, dup * di)


def _swa_kernel(sink_ref, q_ref, kc_ref, vc_ref, kp_ref, vp_ref, o_ref, *, tq, has_cache):
    s_len = WINDOW + tq
    kfull = jnp.concatenate([kp_ref[...], kc_ref[...]], axis=0)
    vfull = jnp.concatenate([vp_ref[...], vc_ref[...]], axis=0)
    lane = lax.broadcasted_iota(jnp.int32, kfull.shape, 1)
    low = lane < SWA_HEAD_DIM
    kswap = pltpu.roll(kfull, SWA_HEAD_DIM, axis=1)
    vswap = pltpu.roll(vfull, SWA_HEAD_DIM, axis=1)
    k_lo = [jnp.where(low, kfull, 0.0), jnp.where(low, kswap, 0.0)]
    k_hi = [jnp.where(low, 0.0, kswap), jnp.where(low, 0.0, kfull)]
    v_lo = [jnp.where(low, vfull, 0.0), jnp.where(low, vswap, 0.0)]
    v_hi = [jnp.where(low, 0.0, vswap), jnp.where(low, 0.0, vfull)]
    t_idx = lax.broadcasted_iota(jnp.int32, (tq, s_len), 0)
    s_idx = lax.broadcasted_iota(jnp.int32, (tq, s_len), 1)
    delta = WINDOW + t_idx - s_idx
    valid = (delta >= 0) & (delta <= WINDOW)
    if not has_cache:
        valid = valid & ((s_idx >= WINDOW) | (pl.program_id(1) > 0))
    contract_last = (((1,), (1,)), ((), ()))
    for hp in range(SWA_HEADS // 2):
        qp = q_ref[:, hp * LANES:(hp + 1) * LANES]
        kv = (2 * hp) // SWA_GROUP
        acc = None
        for half, (kmat, vmat) in enumerate(((k_lo[kv], v_lo[kv]), (k_hi[kv], v_hi[kv]))):
            sink = sink_ref[2 * hp + half]
            s = lax.dot_general(qp, kmat, contract_last, preferred_element_type=F32) * SWA_SCALE
            s = jnp.where(valid, s, -jnp.inf)
            m = jnp.maximum(jnp.max(s, axis=-1, keepdims=True), sink)
            e = jnp.exp(s - m)
            den = jnp.sum(e, axis=-1, keepdims=True) + jnp.exp(sink - m)
            o = _dot(e, vmat) / den
            acc = o if acc is None else acc + o
        o_ref[:, hp * LANES:(hp + 1) * LANES] = acc.astype(o_ref.dtype)


def swa_attention(sinks, p_swa, n_seq, t_len, tq, row0, cache_k=None, cache_v=None):
    n_blk = t_len // tq
    has_cache = cache_k is not None
    kcol, vcol = SWA_WIDTH // LANES, SWA_WIDTH // LANES + 1
    q_spec = _row_spec(SWA_WIDTH, tq, n_blk, row0)
    kc_spec = _row_spec(LANES, tq, n_blk, row0, kcol)
    vc_spec = _row_spec(LANES, tq, n_blk, row0, vcol)
    if has_cache:
        kp_spec = vp_spec = pl.BlockSpec((WINDOW, LANES), lambda b, t: (b, 0))
        kp_arr, vp_arr = cache_k, cache_v
    else:
        assert tq == WINDOW and row0 == 0

        def prev_rows(b, t):
            return jnp.maximum(b * n_blk + t - 1, 0)

        kp_spec = pl.BlockSpec((WINDOW, LANES), lambda b, t: (prev_rows(b, t), kcol))
        vp_spec = pl.BlockSpec((WINDOW, LANES), lambda b, t: (prev_rows(b, t), vcol))
        kp_arr = vp_arr = p_swa
    return pl.pallas_call(
        functools.partial(_swa_kernel, tq=tq, has_cache=has_cache),
        grid=(n_seq, n_blk),
        in_specs=[pl.BlockSpec(memory_space=pltpu.SMEM), q_spec, kc_spec, vc_spec, kp_spec, vp_spec],
        out_specs=_row_spec(SWA_WIDTH, tq, n_blk),
        out_shape=jax.ShapeDtypeStruct((n_seq * t_len, SWA_WIDTH), BF16),
        compiler_params=_cparams(2),
        name="swa",
    )(sinks, p_swa, p_swa, p_swa, kp_arr, vp_arr)


def _gdn_prep_kernel(cur_ref, prev_ref, st_ref, ba_ref, cw_ref, alog_ref, dt_ref, eb_ref, eg_ref, j_ref,
                     kn_o, d_o, b_o, v_o, q_o):
    x = cur_ref[...]
    prev8 = jnp.where(pl.program_id(1) == 0, st_ref[...], prev_ref[...])
    y = x * cw_ref[3:4, :]
    for k in range(1, CONV_WIDTH):
        y = y + _shifted(prev8, x, k) * cw_ref[3 - k:4 - k, :]
    act = y * _sigmoid(y)
    w2 = GDN_WIDTH
    q, k, v = act[:, :w2], act[:, w2:2 * w2], act[:, 2 * w2:]
    j = j_ref[...]
    qn = q * lax.rsqrt(_segsum(q * q, j) + L2_EPS) * (GDN_HEAD_DIM ** -0.5)
    kn = k * lax.rsqrt(_segsum(k * k, j) + L2_EPS)
    ba = ba_ref[...]
    beta = _dot_lhs2(_sigmoid(ba), eb_ref[...])
    gate = _dot_lhs2(-jnp.exp(alog_ref[...]) * _softplus(ba + dt_ref[...]), eg_ref[...])
    eg = jnp.exp(gate)
    kn_o[...] = kn
    d_o[...] = eg
    b_o[...] = eg * beta * kn
    v_o[...] = beta * v
    q_o[...] = qn


def gdn_prep(p_gdn, p_swa, st8, prm, n_seq, t_len, tb, row0):
    n_tb = t_len // tb
    cur, prev, state = _seq_specs(GDN_CONV_DIM, 0, tb, n_tb, row0)
    ba_spec = _row_spec(2 * LANES, tb, n_tb, row0, BA_COL // (2 * LANES))
    out = jax.ShapeDtypeStruct((n_seq * t_len, GDN_WIDTH), F32)
    ospec = _row_spec(GDN_WIDTH, tb, n_tb)
    return pl.pallas_call(
        _gdn_prep_kernel,
        grid=(n_seq, n_tb),
        in_specs=[cur, prev, state, ba_spec, _const_spec((SUBLANES, GDN_CONV_DIM)), _const_spec((1, 2 * LANES)),
                  _const_spec((1, 2 * LANES)), _const_spec((2 * LANES, GDN_WIDTH)),
                  _const_spec((2 * LANES, GDN_WIDTH)), _const_spec((LANES, LANES))],
        out_specs=[ospec] * 5,
        out_shape=[out] * 5,
        compiler_params=_cparams(2),
        name="gdn_prep",
    )(p_gdn, p_gdn, st8, p_swa, prm["conv_w"], prm["a_log"], prm["dt"], prm["e_beta"], prm["e_gate"], prm["j128"])


def _gdn_post_kernel(o_ref_in, z_ref, w_ref, j_ref, o_ref):
    o = o_ref_in[...]
    z = z_ref[...]
    ms = _segsum(o * o, j_ref[...]) * (1.0 / GDN_HEAD_DIM)
    y = o * lax.rsqrt(ms + NORM_EPS) * w_ref[...]
    o_ref[...] = (y * (z * _sigmoid(z))).astype(o_ref.dtype)


def gdn_post(o, p_gdn, prm, tb, row0):
    rows = o.shape[0]
    spec = pl.BlockSpec((tb, GDN_WIDTH), lambda i: (i, 0))
    zspec = pl.BlockSpec((tb, GDN_WIDTH), lambda i: (row0 // tb + i, GDN_CONV_DIM // GDN_WIDTH))
    return pl.pallas_call(
        _gdn_post_kernel,
        grid=(rows // tb,),
        in_specs=[spec, zspec, pl.BlockSpec((1, GDN_WIDTH), lambda i: (0, 0)),
                  pl.BlockSpec((LANES, LANES), lambda i: (0, 0))],
        out_specs=spec,
        out_shape=jax.ShapeDtypeStruct((rows, GDN_WIDTH), BF16),
        compiler_params=_cparams(1),
        name="gdn_post",
    )(o, p_gdn, prm["norm_w"], prm["j128"])


def rwkv_mix(p_rwkv, shift_state, wkv_state, prm, n_seq, t_len, row0, tb_prep, tb_scan):
    st8 = _pad_state_rows(shift_state[:, None, :])
    r, kk, d, b, k, v, g = rwkv_prep(p_rwkv, st8, prm, n_seq, t_len, tb_prep, row0)
    dup = max(1, LANES // (n_seq * RWKV_HEADS))
    kl = functools.partial(_to_key_layout, n_seq=n_seq, t_len=t_len, heads=RWKV_HEADS, dj=RWKV_HEAD_DIM, dup=dup)
    vt = _to_value_layout(v, n_seq, t_len, RWKV_HEADS, RWKV_HEAD_DIM, dup)
    s0 = _state_to_layout(wkv_state, dup, value_major=True)
    y, s1 = dplr_scan(kl(kk), kl(d), kl(b), kl(k), kl(r), vt, s0, tb_scan)
    y = _from_value_layout(y, n_seq, t_len, RWKV_HEADS, RWKV_HEAD_DIM, dup)
    out = rwkv_post(y, r, k, v, g, prm, tb_prep)
    return out, _state_from_layout(s1, n_seq, RWKV_HEADS, dup, value_major=True)


def gdn_mix(p_gdn, p_swa, conv_state, ssm_state, prm, n_seq, t_len, row0, tb_prep, tb_scan):
    st8 = _pad_state_rows(conv_state)
    kn, d, b, v, q = gdn_prep(p_gdn, p_swa, st8, prm, n_seq, t_len, tb_prep, row0)
    dup = 2 * max(1, LANES // (2 * n_seq * GDN_HEADS))
    kl = functools.partial(_to_key_layout, n_seq=n_seq, t_len=t_len, heads=GDN_HEADS, dj=GDN_HEAD_DIM, dup=dup)
    vt = _to_value_layout(v, n_seq, t_len, GDN_HEADS, GDN_HEAD_DIM, dup)
    s0 = _state_to_layout(ssm_state, dup, value_major=False)
    knl = kl(kn)
    o, s1 = dplr_scan(knl, kl(d), kl(b), knl, kl(q), vt, s0, tb_scan)
    o = _from_value_layout(o, n_seq, t_len, GDN_HEADS, GDN_HEAD_DIM, dup)
    out = gdn_post(o, p_gdn, prm, tb_prep, row0)
    return out, _state_from_layout(s1, n_seq, GDN_HEADS, dup, value_major=False)


def _block_ones(seg):
    i = jnp.arange(LANES)
    return (i[:, None] // seg == i[None, :] // seg).astype(BF16)


def _head_expander(first_row):
    rows = jnp.arange(2 * LANES)[:, None]
    cols = jnp.arange(GDN_WIDTH)[None, :] // GDN_HEAD_DIM
    return (rows == cols + first_row).astype(BF16)


def _layer_params(l, W):
    row = lambda x: x.reshape(1, -1)
    zeros = jnp.zeros((DECAY_LORA, RWKV_WIDTH), F32)
    wwa = jnp.concatenate([jnp.concatenate([W["rwkv_w2"][l], zeros], axis=1),
                           jnp.concatenate([zeros, W["rwkv_a2"][l]], axis=1)], axis=0)
    lane_row = lambda x, off: jnp.pad(x, (off, 2 * LANES - off - x.shape[0])).reshape(1, 2 * LANES)
    rwkv = dict(mix=row(W["rwkv_shift_mix"][l]), w0=row(W["rwkv_w0"][l]), a0=row(W["rwkv_a0"][l]), wwa=wwa,
                g2=W["rwkv_g2"][l], k_k=row(W["rwkv_k_k"][l]), k_a=row(W["rwkv_k_a"][l]),
                r_k=row(W["rwkv_r_k"][l]), ln_w=row(W["rwkv_ln_w"][l]), ln_b=row(W["rwkv_ln_b"][l]),
                j64=_block_ones(RWKV_HEAD_DIM))
    gdn = dict(conv_w=jnp.pad(W["gdn_conv_w"][l], ((0, SUBLANES - CONV_WIDTH), (0, 0))),
               a_log=lane_row(W["gdn_A_log"][l], GDN_HEADS), dt=lane_row(W["gdn_dt_bias"][l], GDN_HEADS),
               e_beta=_head_expander(0), e_gate=_head_expander(GDN_HEADS),
               norm_w=jnp.tile(W["gdn_norm_w"][l], GDN_HEADS).reshape(1, GDN_WIDTH), j128=_block_ones(LANES))
    return rwkv, gdn


def kernel(x_prompt, x_sample, state_rwkv_shift, state_rwkv_wkv, cache_swa_k, cache_swa_v, state_gdn_conv,
           state_gdn_ssm, norm_mix, w_in, rwkv_shift_mix, rwkv_w0, rwkv_w2, rwkv_a0, rwkv_a2, rwkv_g2, rwkv_k_k,
           rwkv_k_a, rwkv_r_k, rwkv_ln_w, rwkv_ln_b, swa_sinks, gdn_conv_w, gdn_A_log, gdn_dt_bias, gdn_norm_w,
           w_out, norm_ffn, w_gate, w_up, w_down, final_norm):
    W = dict(rwkv_shift_mix=rwkv_shift_mix, rwkv_w0=rwkv_w0, rwkv_w2=rwkv_w2, rwkv_a0=rwkv_a0, rwkv_a2=rwkv_a2,
             rwkv_g2=rwkv_g2, rwkv_k_k=rwkv_k_k, rwkv_k_a=rwkv_k_a, rwkv_r_k=rwkv_r_k, rwkv_ln_w=rwkv_ln_w,
             rwkv_ln_b=rwkv_ln_b, gdn_conv_w=gdn_conv_w, gdn_A_log=gdn_A_log, gdn_dt_bias=gdn_dt_bias,
             gdn_norm_w=gdn_norm_w)
    r0, s0 = RWKV_PROJ, RWKV_PROJ + SWA_PROJ
    w_rwkv = jnp.pad(w_in[:, :, :r0], ((0, 0), (0, 0), (0, RWKV_PAD - RWKV_PROJ))).astype(BF16)
    w_swa = jnp.pad(jnp.concatenate([w_in[:, :, r0:s0], w_in[:, :, s0 + GDN_MAIN:]], axis=2),
                    ((0, 0), (0, 0), (0, SWA_PAD - SWA_PROJ - 2 * GDN_HEADS))).astype(BF16)
    w_gdn = w_in[:, :, s0:s0 + GDN_MAIN].astype(BF16)
    w_out_b, w_gate_b, w_up_b, w_down_b = (w.astype(BF16) for w in (w_out, w_gate, w_up, w_down))

    h = jnp.concatenate([x_prompt.reshape(ROWS_P, D_MODEL), x_sample.reshape(ROWS_D, D_MODEL)], axis=0)
    zero_shift = jnp.zeros((BATCH, RWKV_PROJ), F32)
    zero_wkv = jnp.zeros((BATCH, RWKV_HEADS, RWKV_HEAD_DIM, RWKV_HEAD_DIM), F32)
    zero_conv = jnp.zeros((BATCH, CONV_WIDTH - 1, GDN_CONV_DIM), F32)
    zero_ssm = jnp.zeros((BATCH, GDN_HEADS, GDN_HEAD_DIM, GDN_HEAD_DIM), F32)
    wb = cache_swa_k.shape[2]
    assert wb == WINDOW
    outs = [[] for _ in range(12)]
    bm = 1056
    for l in range(DEPTH):
        rw, gd = _layer_params(l, W)
        u = rmsnorm(h, norm_mix[l], BF16)
        p_rwkv = matmul(u, w_rwkv, l, bm, 512)
        p_swa = matmul(u, w_swa, l, bm, 512)
        p_gdn = matmul(u, w_gdn, l, bm, 512)

        o_rp, wkv_p = rwkv_mix(p_rwkv, zero_shift, zero_wkv, rw, BATCH, SEQ, 0, 128, 64)
        o_rd, wkv_d = rwkv_mix(p_rwkv, state_rwkv_shift[l], state_rwkv_wkv[l], rw, DEC_BATCH, DEC_SEQ, ROWS_P,
                               DEC_SEQ, DEC_SEQ)
        o_sp = swa_attention(swa_sinks[l], p_swa, BATCH, SEQ, WINDOW, 0)
        o_sd = swa_attention(swa_sinks[l], p_swa, DEC_BATCH, DEC_SEQ, DEC_SEQ, ROWS_P,
                             cache_swa_k[l].reshape(DEC_BATCH * wb, SWA_KV_WIDTH),
                             cache_swa_v[l].reshape(DEC_BATCH * wb, SWA_KV_WIDTH))
        o_gp, ssm_p = gdn_mix(p_gdn, p_swa, zero_conv, zero_ssm, gd, BATCH, SEQ, 0, 64, 32)
        o_gd, ssm_d = gdn_mix(p_gdn, p_swa, state_gdn_conv[l], state_gdn_ssm[l], gd, DEC_BATCH, DEC_SEQ, ROWS_P,
                              DEC_SEQ, DEC_SEQ)
        o_r = jnp.concatenate([o_rp, o_rd], axis=0)
        o_s = jnp.concatenate([o_sp, o_sd], axis=0)
        o_g = jnp.concatenate([o_gp, o_gd], axis=0)
        h = matmul_residual([o_r, o_s, o_g], w_out_b, l, h, bm, 512, [RWKV_WIDTH, SWA_WIDTH, GDN_WIDTH])
        u = rmsnorm(h, norm_ffn[l], BF16)
        h1 = swiglu_matmul(u, w_gate_b, w_up_b, l, bm, 512)
        h = matmul_residual([h1], w_down_b, l, h, bm, 512, [D_FF // 2])

        pp = lambda p, w: p[:ROWS_P, :w].reshape(BATCH, SEQ, w)
        pd = lambda p, w: p[ROWS_P:, :w].reshape(DEC_BATCH, DEC_SEQ, w)
        kv = lambda x: x.reshape(x.shape[0], x.shape[1], SWA_KV_HEADS, SWA_HEAD_DIM)
        swa_p, swa_d = pp(p_swa, SWA_PROJ), pd(p_swa, SWA_PROJ)
        k_p, v_p = swa_p[:, SEQ - wb:, SWA_WIDTH:SWA_WIDTH + SWA_KV_WIDTH], swa_p[:, SEQ - wb:, SWA_WIDTH + SWA_KV_WIDTH:]
        k_d, v_d = swa_d[:, :, SWA_WIDTH:SWA_WIDTH + SWA_KV_WIDTH], swa_d[:, :, SWA_WIDTH + SWA_KV_WIDTH:]
        layer_out = (
            pp(p_rwkv, RWKV_PROJ)[:, -1], wkv_p, kv(k_p), kv(v_p),
            pp(p_gdn, GDN_CONV_DIM)[:, SEQ - (CONV_WIDTH - 1):], ssm_p,
            pd(p_rwkv, RWKV_PROJ)[:, -1], wkv_d,
            jnp.concatenate([cache_swa_k[l], kv(k_d)], axis=1)[:, -wb:],
            jnp.concatenate([cache_swa_v[l], kv(v_d)], axis=1)[:, -wb:],
            pd(p_gdn, GDN_CONV_DIM)[:, DEC_SEQ - (CONV_WIDTH - 1):], ssm_d)
        for lst, t in zip(outs, layer_out):
            lst.append(t)
    y = rmsnorm(h, final_norm, F32)
    y_prompt = y[:ROWS_P].reshape(BATCH, SEQ, D_MODEL)
    y_sample = y[ROWS_P:].reshape(DEC_BATCH, DEC_SEQ, D_MODEL)
    return (y_prompt, y_sample) + tuple(jnp.stack(lst) for lst in outs)
```

```python
import functools

import jax
import jax.numpy as jnp
from jax import lax
from jax.experimental import pallas as pl
from jax.experimental.pallas import tpu as pltpu

D_MODEL = 4096
BATCH = 4
SEQ = 2048
DEPTH = 4
DEC_BATCH = 32
DEC_SEQ = 8
NORM_EPS = 1e-6
L2_EPS = 1e-6
RWKV_WIDTH = 1024
RWKV_HEAD_DIM = 64
RWKV_HEADS = 16
DECAY_LORA = 64
AAA_LORA = 64
GATE_LORA = 128
RWKV_PROJ = 3 * RWKV_WIDTH + DECAY_LORA + AAA_LORA + GATE_LORA
RWKV_LN_EPS = 64e-5
SWA_WIDTH = 1024
SWA_HEAD_DIM = 64
SWA_HEADS = 16
SWA_KV_HEADS = 2
SWA_GROUP = 8
SWA_KV_WIDTH = 128
SWA_PROJ = SWA_WIDTH + 2 * SWA_KV_WIDTH
WINDOW = 128
SWA_SCALE = SWA_HEAD_DIM ** -0.5
GDN_WIDTH = 2048
GDN_HEAD_DIM = 128
GDN_HEADS = 16
GDN_CONV_DIM = 3 * GDN_WIDTH
CONV_WIDTH = 4
GDN_CHUNK = 64
GDN_MAIN = GDN_CONV_DIM + GDN_WIDTH
D_FF = 11008

ROWS_P = BATCH * SEQ
ROWS_D = DEC_BATCH * DEC_SEQ
ROWS = ROWS_P + ROWS_D

LANES = 128
SUBLANES = 8
RWKV_PAD = 3584
SWA_PAD = 1536
BA_COL = SWA_PROJ
VMEM_LIMIT = 56 * 1024 * 1024

F32 = jnp.float32
BF16 = jnp.bfloat16


def _cparams(n_axes):
    return pltpu.CompilerParams(dimension_semantics=("arbitrary",) * n_axes, vmem_limit_bytes=VMEM_LIMIT)


def _split(x):
    hi = x.astype(BF16)
    lo = (x - hi.astype(F32)).astype(BF16)
    return hi, lo


def _dot(a, b):
    return jnp.dot(a, b, preferred_element_type=F32)


def _dot_lhs2(x, m_bf16):
    hi, lo = _split(x)
    return _dot(hi, m_bf16) + _dot(lo, m_bf16)


def _dot3(x, w):
    xh, xl = _split(x)
    wh, wl = _split(w)
    return _dot(xh, wh) + (_dot(xl, wh) + _dot(xh, wl))


def _segsum(x, j_bf16):
    n = x.shape[-1] // LANES
    return jnp.concatenate([_dot_lhs2(x[:, c * LANES:(c + 1) * LANES], j_bf16) for c in range(n)], axis=-1)


def _sigmoid(x):
    return 1.0 / (1.0 + jnp.exp(-x))


def _softplus(x):
    return jnp.maximum(x, 0.0) + jnp.log(1.0 + jnp.exp(-jnp.abs(x)))


def _rmsnorm_kernel(x_ref, w_ref, o_ref):
    x = x_ref[...]
    ms = jnp.mean(x * x, axis=-1, keepdims=True)
    o_ref[...] = (x * lax.rsqrt(ms + NORM_EPS) * w_ref[...]).astype(o_ref.dtype)


def rmsnorm(x, w, out_dtype, br=256):
    r, d = x.shape
    return pl.pallas_call(
        _rmsnorm_kernel,
        grid=(r // br,),
        in_specs=[pl.BlockSpec((br, d), lambda i: (i, 0)), pl.BlockSpec((1, d), lambda i: (0, 0))],
        out_specs=pl.BlockSpec((br, d), lambda i: (i, 0)),
        out_shape=jax.ShapeDtypeStruct((r, d), out_dtype),
        compiler_params=_cparams(1),
        name="rmsnorm",
    )(x, w.reshape(1, d))


def _mm_kernel(a_ref, b_ref, o_ref):
    o_ref[...] = _dot(a_ref[...], b_ref[...]).astype(o_ref.dtype)


def matmul(a, w, layer, bm, bn, out_dtype=F32):
    r, k = a.shape
    n = w.shape[2]
    return pl.pallas_call(
        _mm_kernel,
        grid=(r // bm, n // bn),
        in_specs=[pl.BlockSpec((bm, k), lambda i, j: (i, 0)),
                  pl.BlockSpec((None, k, bn), lambda i, j: (layer, 0, j))],
        out_specs=pl.BlockSpec((bm, bn), lambda i, j: (i, j)),
        out_shape=jax.ShapeDtypeStruct((r, n), out_dtype),
        compiler_params=_cparams(2),
        name="proj_in",
    )(a, w)


def _swiglu_kernel(a_ref, g_ref, u_ref, o_ref):
    a = a_ref[...]
    g = _dot(a, g_ref[...])
    u = _dot(a, u_ref[...])
    o_ref[...] = (g * _sigmoid(g) * u).astype(o_ref.dtype)


def swiglu_matmul(a, wg, wu, layer, bm, bn):
    r, k = a.shape
    n = wg.shape[2]
    wspec = pl.BlockSpec((None, k, bn), lambda i, j: (layer, 0, j))
    return pl.pallas_call(
        _swiglu_kernel,
        grid=(r // bm, pl.cdiv(n, bn)),
        in_specs=[pl.BlockSpec((bm, k), lambda i, j: (i, 0)), wspec, wspec],
        out_specs=pl.BlockSpec((bm, bn), lambda i, j: (i, j)),
        out_shape=jax.ShapeDtypeStruct((r, n), BF16),
        compiler_params=_cparams(2),
        name="ffn_swiglu",
    )(a, wg, wu)


def _mm_res_kernel(*refs, n_a):
    a_refs, b_refs, res_ref, o_ref = refs[:n_a], refs[n_a:2 * n_a], refs[2 * n_a], refs[2 * n_a + 1]
    kk = pl.program_id(2)

    @pl.when(kk == 0)
    def _():
        o_ref[...] = res_ref[...]

    acc = _dot(a_refs[0][...], b_refs[0][...])
    for a_ref, b_ref in zip(a_refs[1:], b_refs[1:]):
        acc = acc + _dot(a_ref[...], b_ref[...])
    o_ref[...] += acc


def matmul_residual(a_list, w, layer, res, bm, bn, bk_list):
    r = res.shape[0]
    n = w.shape[2]
    nk = a_list[0].shape[1] // bk_list[0]
    in_specs, row0 = [], 0
    for a, bk in zip(a_list, bk_list):
        assert a.shape[1] == nk * bk
        in_specs.append(pl.BlockSpec((bm, bk), lambda i, j, k: (i, k)))
    for a, bk in zip(a_list, bk_list):
        assert row0 % bk == 0
        in_specs.append(pl.BlockSpec((None, bk, bn), functools.partial(
            lambda i, j, k, off: (layer, off + k, j), off=row0 // bk)))
        row0 += a.shape[1]
    in_specs.append(pl.BlockSpec((bm, bn), lambda i, j, k: (i, j)))
    return pl.pallas_call(
        functools.partial(_mm_res_kernel, n_a=len(a_list)),
        grid=(r // bm, n // bn, nk),
        in_specs=in_specs,
        out_specs=pl.BlockSpec((bm, bn), lambda i, j, k: (i, j)),
        out_shape=jax.ShapeDtypeStruct((r, n), F32),
        compiler_params=_cparams(3),
        name="proj_residual",
    )(*a_list, *([w] * len(a_list)), res)


def _shifted(prev8, x, k):
    ext = jnp.concatenate([prev8, x], axis=0)
    return pltpu.roll(ext, k, axis=0)[SUBLANES:]


def _seq_specs(width, col_block, tb, n_tb, row0):
    assert row0 % tb == 0 and tb % SUBLANES == 0
    cur = pl.BlockSpec((tb, width), lambda b, t: (row0 // tb + b * n_tb + t, col_block))
    prev = pl.BlockSpec((SUBLANES, width), lambda b, t: (
        jnp.maximum((row0 + (b * n_tb + t) * tb) // SUBLANES - 1, 0), col_block))
    state = pl.BlockSpec((SUBLANES, width), lambda b, t: (b, 0))
    return cur, prev, state


def _row_spec(width, tb, n_tb, row0=0, col_block=0):
    return pl.BlockSpec((tb, width), lambda b, t: (row0 // tb + b * n_tb + t, col_block))


def _const_spec(shape):
    return pl.BlockSpec(shape, lambda b, t: (0,) * len(shape))


def _pad_state_rows(st):
    b, k, c = st.shape
    return jnp.pad(st, ((0, 0), (SUBLANES - k, 0), (0, 0))).reshape(b * SUBLANES, c)


def _rwkv_prep_kernel(cur_ref, prev_ref, st_ref, mix_ref, w0_ref, a0_ref, wwa_ref, g2_ref, kkw_ref, kaw_ref, j_ref,
                      r_o, kk_o, d_o, b_o, k_o, v_o, g_o):
    x = cur_ref[...]
    prev8 = jnp.where(pl.program_id(1) == 0, st_ref[...], prev_ref[...])
    xs = x + (_shifted(prev8, x, 1) - x) * mix_ref[...]
    w3 = RWKV_WIDTH
    r, k, v = xs[:, 0:w3], xs[:, w3:2 * w3], xs[:, 2 * w3:3 * w3]
    wa = xs[:, 3 * w3:3 * w3 + LANES]
    glo = xs[:, 3 * w3 + LANES:3 * w3 + 2 * LANES]
    lane = lax.broadcasted_iota(jnp.int32, wa.shape, 1)
    lora = _dot3(jnp.where(lane < DECAY_LORA, jnp.tanh(wa), wa), wwa_ref[...])
    w = -_softplus(-(w0_ref[...] + lora[:, :w3])) - 0.5
    d = jnp.exp(-jnp.exp(w))
    a = _sigmoid(a0_ref[...] + lora[:, w3:])
    g = _dot3(_sigmoid(glo), g2_ref[...])
    kn = k * kkw_ref[...]
    kk = kn * lax.rsqrt(_segsum(kn * kn, j_ref[...]) + L2_EPS)
    r_o[...] = r
    kk_o[...] = kk
    d_o[...] = d
    b_o[...] = kk * a
    k_o[...] = k * (1.0 + (a - 1.0) * kaw_ref[...])
    v_o[...] = v
    g_o[...] = g


def rwkv_prep(p_rwkv, st8, prm, n_seq, t_len, tb, row0):
    n_tb = t_len // tb
    cur, prev, state = _seq_specs(RWKV_PROJ, 0, tb, n_tb, row0)
    out = jax.ShapeDtypeStruct((n_seq * t_len, RWKV_WIDTH), F32)
    ospec = _row_spec(RWKV_WIDTH, tb, n_tb)
    return pl.pallas_call(
        _rwkv_prep_kernel,
        grid=(n_seq, n_tb),
        in_specs=[cur, prev, state, _const_spec((1, RWKV_PROJ)), _const_spec((1, RWKV_WIDTH)),
                  _const_spec((1, RWKV_WIDTH)), _const_spec((LANES, 2 * RWKV_WIDTH)),
                  _const_spec((GATE_LORA, RWKV_WIDTH)), _const_spec((1, RWKV_WIDTH)), _const_spec((1, RWKV_WIDTH)),
                  _const_spec((LANES, LANES))],
        out_specs=[ospec] * 7,
        out_shape=[out] * 7,
        compiler_params=_cparams(2),
        name="rwkv_prep",
    )(p_rwkv, p_rwkv, st8, prm["mix"], prm["w0"], prm["a0"], prm["wwa"], prm["g2"], prm["k_k"], prm["k_a"],
      prm["j64"])


def _rwkv_post_kernel(y_ref, r_ref, k_ref, v_ref, g_ref, lnw_ref, lnb_ref, rk_ref, j_ref, o_ref):
    j = j_ref[...]
    y = y_ref[...]
    inv_n = 1.0 / RWKV_HEAD_DIM
    yc = y - _segsum(y, j) * inv_n
    var = _segsum(yc * yc, j) * inv_n
    out = yc * lax.rsqrt(var + RWKV_LN_EPS) * lnw_ref[...] + lnb_ref[...]
    out = out + _segsum(r_ref[...] * k_ref[...] * rk_ref[...], j) * v_ref[...]
    o_ref[...] = (out * g_ref[...]).astype(o_ref.dtype)


def rwkv_post(y, r, k, v, g, prm, tb):
    rows = y.shape[0]
    spec = pl.BlockSpec((tb, RWKV_WIDTH), lambda i: (i, 0))
    cspec = pl.BlockSpec((1, RWKV_WIDTH), lambda i: (0, 0))
    return pl.pallas_call(
        _rwkv_post_kernel,
        grid=(rows // tb,),
        in_specs=[spec] * 5 + [cspec] * 3 + [pl.BlockSpec((LANES, LANES), lambda i: (0, 0))],
        out_specs=spec,
        out_shape=jax.ShapeDtypeStruct((rows, RWKV_WIDTH), BF16),
        compiler_params=_cparams(1),
        name="rwkv_post",
    )(y, r, k, v, g, prm["ln_w"], prm["ln_b"], prm["r_k"], prm["j64"])


def _scan_kernel(kk_ref, d_ref, b_ref, k_ref, r_ref, v_ref, s0_ref, y_ref, s_ref, *, tb, dj, di):
    ng = di // SUBLANES

    @pl.when(pl.program_id(1) == 0)
    def _():
        s_ref[...] = s0_ref[...]

    def row(ref, t, j):
        return ref[0, t, pl.ds(j, 1), :]

    def sl(g):
        return pl.ds(g * SUBLANES, SUBLANES)

    def step(t, carry):
        def p1(j, acc):
            kkj = row(kk_ref, t, j)
            return tuple(acc[g] + s_ref[0, j, sl(g), :] * kkj for g in range(ng))

        zero = tuple(jnp.zeros((SUBLANES, LANES), F32) for _ in range(ng))
        acc = lax.fori_loop(0, dj, p1, zero, unroll=8)
        sa = [-a for a in acc]
        v = [v_ref[0, t, sl(g), :] for g in range(ng)]

        def p2(j, yacc):
            dj_, bj, kj, rj = row(d_ref, t, j), row(b_ref, t, j), row(k_ref, t, j), row(r_ref, t, j)
            out = []
            for g in range(ng):
                s = s_ref[0, j, sl(g), :] * dj_ + sa[g] * bj + v[g] * kj
                s_ref[0, j, sl(g), :] = s
                out.append(yacc[g] + s * rj)
            return tuple(out)

        yacc = lax.fori_loop(0, dj, p2, zero, unroll=8)
        for g in range(ng):
            y_ref[0, t, sl(g), :] = yacc[g]
        return carry

    lax.fori_loop(0, tb, step, 0)


def dplr_scan(kk, d, b, k, r, v, s0, tb):
    g_n, t_len, dj, _ = kk.shape
    di = v.shape[2]
    jspec = pl.BlockSpec((1, tb, dj, LANES), lambda g, t: (g, t, 0, 0))
    ispec = pl.BlockSpec((1, tb, di, LANES), lambda g, t: (g, t, 0, 0))
    sspec = pl.BlockSpec((1, dj, di, LANES), lambda g, t: (g, 0, 0, 0))
    return pl.pallas_call(
        functools.partial(_scan_kernel, tb=tb, dj=dj, di=di),
        grid=(g_n, t_len // tb),
        in_specs=[jspec] * 5 + [ispec, sspec],
        out_specs=[ispec, sspec],
        out_shape=[jax.ShapeDtypeStruct(v.shape, F32), jax.ShapeDtypeStruct(s0.shape, F32)],
        compiler_params=_cparams(2),
        name="dplr_scan",
    )(kk, d, b, k, r, v, s0)


def _to_key_layout(x, n_seq, t_len, heads, dj, dup):
    x = x.reshape(n_seq, t_len, heads, dj).transpose(1, 3, 0, 2).reshape(t_len, dj, n_seq * heads)
    x = jnp.tile(x, (1, 1, dup))
    g_n = dup * n_seq * heads // LANES
    return x.reshape(t_len, dj, g_n, LANES).transpose(2, 0, 1, 3)


def _to_value_layout(x, n_seq, t_len, heads, dv, dup):
    di = dv // dup
    x = x.reshape(n_seq, t_len, heads, dup, di).transpose(1, 4, 3, 0, 2)
    g_n = dup * n_seq * heads // LANES
    return x.reshape(t_len, di, g_n, LANES).transpose(2, 0, 1, 3)


def _from_value_layout(y, n_seq, t_len, heads, dv, dup):
    g_n, _, di, _ = y.shape
    y = y.transpose(1, 2, 0, 3).reshape(t_len, di, dup, n_seq, heads)
    return y.transpose(3, 0, 4, 2, 1).reshape(n_seq * t_len, heads * dv)


def _state_to_layout(s, dup, value_major):
    bsz, heads = s.shape[:2]
    if value_major:
        dv, dj = s.shape[2:]
        s = s.reshape(bsz, heads, dup, dv // dup, dj).transpose(4, 3, 2, 0, 1)
    else:
        dj, dv = s.shape[2:]
        s = s.reshape(bsz, heads, dj, dup, dv // dup).transpose(2, 4, 3, 0, 1)
    g_n = dup * bsz * heads // LANES
    return s.reshape(dj, dv // dup, g_n, LANES).transpose(2, 0, 1, 3)


def _state_from_layout(s, bsz, heads, dup, value_major):
    g_n, dj, di, _ = s.shape
    s = s.transpose(1, 2, 0, 3).reshape(dj, di, dup, bsz, heads)
    if value_major:
        return s.transpose(3, 4, 2, 1, 0).reshape(bsz, heads, dup * di, dj)
    return s.transpose(3, 4, 0, 2, 1).reshape(bsz, heads, dj, dup * di)


def _swa_kernel(sink_ref, q_ref, kc_ref, vc_ref, kp_ref, vp_ref, o_ref, *, tq, has_cache):
    s_len = WINDOW + tq
    kfull = jnp.concatenate([kp_ref[...], kc_ref[...]], axis=0)
    vfull = jnp.concatenate([vp_ref[...], vc_ref[...]], axis=0)
    lane = lax.broadcasted_iota(jnp.int32, kfull.shape, 1)
    low = lane < SWA_HEAD_DIM
    kswap = pltpu.roll(kfull, SWA_HEAD_DIM, axis=1)
    vswap = pltpu.roll(vfull, SWA_HEAD_DIM, axis=1)
    k_lo = [jnp.where(low, kfull, 0.0), jnp.where(low, kswap, 0.0)]
    k_hi = [jnp.where(low, 0.0, kswap), jnp.where(low, 0.0, kfull)]
    v_lo = [jnp.where(low, vfull, 0.0), jnp.where(low, vswap, 0.0)]
    v_hi = [jnp.where(low, 0.0, vswap), jnp.where(low, 0.0, vfull)]
    t_idx = lax.broadcasted_iota(jnp.int32, (tq, s_len), 0)
    s_idx = lax.broadcasted_iota(jnp.int32, (tq, s_len), 1)
    delta = WINDOW + t_idx - s_idx
    valid = (delta >= 0) & (delta <= WINDOW)
    if not has_cache:
        valid = valid & ((s_idx >= WINDOW) | (pl.program_id(1) > 0))
    contract_last = (((1,), (1,)), ((), ()))
    for hp in range(SWA_HEADS // 2):
        qp = q_ref[:, hp * LANES:(hp + 1) * LANES]
        kv = (2 * hp) // SWA_GROUP
        acc = None
        for half, (kmat, vmat) in enumerate(((k_lo[kv], v_lo[kv]), (k_hi[kv], v_hi[kv]))):
            sink = sink_ref[2 * hp + half]
            s = lax.dot_general(qp, kmat, contract_last, preferred_element_type=F32) * SWA_SCALE
            s = jnp.where(valid, s, -jnp.inf)
            m = jnp.maximum(jnp.max(s, axis=-1, keepdims=True), sink)
            e = jnp.exp(s - m)
            den = jnp.sum(e, axis=-1, keepdims=True) + jnp.exp(sink - m)
            o = _dot(e, vmat) / den
            acc = o if acc is None else acc + o
        o_ref[:, hp * LANES:(hp + 1) * LANES] = acc.astype(o_ref.dtype)


def swa_attention(sinks, p_swa, n_seq, t_len, tq, row0, cache_k=None, cache_v=None):
    n_blk = t_len // tq
    has_cache = cache_k is not None
    kcol, vcol = SWA_WIDTH // LANES, SWA_WIDTH // LANES + 1
    q_spec = _row_spec(SWA_WIDTH, tq, n_blk, row0)
    kc_spec = _row_spec(LANES, tq, n_blk, row0, kcol)
    vc_spec = _row_spec(LANES, tq, n_blk, row0, vcol)
    if has_cache:
        kp_spec = vp_spec = pl.BlockSpec((WINDOW, LANES), lambda b, t: (b, 0))
        kp_arr, vp_arr = cache_k, cache_v
    else:
        assert tq == WINDOW and row0 == 0

        def prev_rows(b, t):
            return jnp.maximum(b * n_blk + t - 1, 0)

        kp_spec = pl.BlockSpec((WINDOW, LANES), lambda b, t: (prev_rows(b, t), kcol))
        vp_spec = pl.BlockSpec((WINDOW, LANES), lambda b, t: (prev_rows(b, t), vcol))
        kp_arr = vp_arr = p_swa
    return pl.pallas_call(
        functools.partial(_swa_kernel, tq=tq, has_cache=has_cache),
        grid=(n_seq, n_blk),
        in_specs=[pl.BlockSpec(memory_space=pltpu.SMEM), q_spec, kc_spec, vc_spec, kp_spec, vp_spec],
        out_specs=_row_spec(SWA_WIDTH, tq, n_blk),
        out_shape=jax.ShapeDtypeStruct((n_seq * t_len, SWA_WIDTH), BF16),
        compiler_params=_cparams(2),
        name="swa",
    )(sinks, p_swa, p_swa, p_swa, kp_arr, vp_arr)


def _gdn_prep_kernel(cur_ref, prev_ref, st_ref, ba_ref, cw_ref, alog_ref, dt_ref, eb_ref, eg_ref, j_ref,
                     kn_o, d_o, b_o, v_o, q_o, *, chunked):
    x = cur_ref[...]
    prev8 = jnp.where(pl.program_id(1) == 0, st_ref[...], prev_ref[...])
    y = x * cw_ref[3:4, :]
    for k in range(1, CONV_WIDTH):
        y = y + _shifted(prev8, x, k) * cw_ref[3 - k:4 - k, :]
    act = y * _sigmoid(y)
    w2 = GDN_WIDTH
    q, k, v = act[:, :w2], act[:, w2:2 * w2], act[:, 2 * w2:]
    j = j_ref[...]
    qn = q * lax.rsqrt(_segsum(q * q, j) + L2_EPS) * (GDN_HEAD_DIM ** -0.5)
    kn = k * lax.rsqrt(_segsum(k * k, j) + L2_EPS)
    ba = ba_ref[...]
    beta = _dot_lhs2(_sigmoid(ba), eb_ref[...])
    gate = _dot_lhs2(-jnp.exp(alog_ref[...]) * _softplus(ba + dt_ref[...]), eg_ref[...])
    kn_o[...] = kn
    q_o[...] = qn
    if chunked:
        d_o[...] = gate
        b_o[...] = beta
        v_o[...] = v
    else:
        eg = jnp.exp(gate)
        d_o[...] = eg
        b_o[...] = eg * beta * kn
        v_o[...] = beta * v


def gdn_prep(p_gdn, p_swa, st8, prm, n_seq, t_len, tb, row0, chunked):
    n_tb = t_len // tb
    cur, prev, state = _seq_specs(GDN_CONV_DIM, 0, tb, n_tb, row0)
    ba_spec = _row_spec(2 * LANES, tb, n_tb, row0, BA_COL // (2 * LANES))
    out = jax.ShapeDtypeStruct((n_seq * t_len, GDN_WIDTH), F32)
    ospec = _row_spec(GDN_WIDTH, tb, n_tb)
    return pl.pallas_call(
        functools.partial(_gdn_prep_kernel, chunked=chunked),
        grid=(n_seq, n_tb),
        in_specs=[cur, prev, state, ba_spec, _const_spec((SUBLANES, GDN_CONV_DIM)), _const_spec((1, 2 * LANES)),
                  _const_spec((1, 2 * LANES)), _const_spec((2 * LANES, GDN_WIDTH)),
                  _const_spec((2 * LANES, GDN_WIDTH)), _const_spec((LANES, LANES))],
        out_specs=[ospec] * 5,
        out_shape=[out] * 5,
        compiler_params=_cparams(2),
        name="gdn_prep",
    )(p_gdn, p_gdn, st8, p_swa, prm["conv_w"], prm["a_log"], prm["dt"], prm["e_beta"], prm["e_gate"], prm["j128"])


def _gdn_post_kernel(o_ref_in, z_ref, w_ref, j_ref, o_ref):
    o = o_ref_in[...]
    z = z_ref[...]
    ms = _segsum(o * o, j_ref[...]) * (1.0 / GDN_HEAD_DIM)
    y = o * lax.rsqrt(ms + NORM_EPS) * w_ref[...]
    o_ref[...] = (y * (z * _sigmoid(z))).astype(o_ref.dtype)


def gdn_post(o, p_gdn, prm, tb, row0):
    rows = o.shape[0]
    spec = pl.BlockSpec((tb, GDN_WIDTH), lambda i: (i, 0))
    zspec = pl.BlockSpec((tb, GDN_WIDTH), lambda i: (row0 // tb + i, GDN_CONV_DIM // GDN_WIDTH))
    return pl.pallas_call(
        _gdn_post_kernel,
        grid=(rows // tb,),
        in_specs=[spec, zspec, pl.BlockSpec((1, GDN_WIDTH), lambda i: (0, 0)),
                  pl.BlockSpec((LANES, LANES), lambda i: (0, 0))],
        out_specs=spec,
        out_shape=jax.ShapeDtypeStruct((rows, GDN_WIDTH), BF16),
        compiler_params=_cparams(1),
        name="gdn_post",
    )(o, p_gdn, prm["norm_w"], prm["j128"])


def _split3(x):
    hi = x.astype(BF16)
    r = x - hi.astype(F32)
    mid = r.astype(BF16)
    return hi, mid, (r - mid.astype(F32)).astype(BF16)


def _dotb(a, b):
    return _dot(a.astype(BF16), b.astype(BF16))


def _dot3w(x, w):
    xh, xl = _split(x)
    wh, wl = _split(w)
    return _dot(jnp.concatenate([xh, xl, xh], axis=1), jnp.concatenate([wh, wh, wl], axis=0))


def _pad_rows(x, rows):
    return jnp.concatenate([x, jnp.zeros((rows - x.shape[0], x.shape[1]), x.dtype)], axis=0)


def _gdn_chunk_kernel(q_ref, k_ref, v_ref, beta_ref, g_ref, s0_ref, o_ref, s_ref, *, hb, nc):
    c = GDN_CHUNK

    @pl.when(pl.program_id(2) == 0)
    def _():
        s_ref[...] = s0_ref[...]

    row = lax.broadcasted_iota(jnp.int32, (c, LANES), 0)
    col = lax.broadcasted_iota(jnp.int32, (c, LANES), 1)
    incl, strict = row >= col, row > col
    eye = (row == col).astype(F32)
    row3 = lax.broadcasted_iota(jnp.int32, (c, 2 * LANES), 0)
    col3 = lax.broadcasted_iota(jnp.int32, (c, 2 * LANES), 1)
    tri3 = ((row3 >= col3 % c) & (col3 < 3 * c)).astype(BF16)
    lane0_3 = (lax.broadcasted_iota(jnp.int32, (c, 3 * LANES), 1) % LANES == 0).astype(BF16)
    zero_c = jnp.zeros((c, LANES), BF16)
    contract_last = (((1,), (1,)), ((), ()))
    contract_first = (((0,), (0,)), ((), ()))

    units = [(h, ci) for ci in range(nc) for h in range(hb)]
    n_u = len(units)

    def load(ref):
        return [ref[ci * c:(ci + 1) * c, h * LANES:(h + 1) * LANES] for h, ci in units]

    q, k, v, beta, g = load(q_ref), load(k_ref), load(v_ref), load(beta_ref), load(g_ref)
    lanes = lambda x, u: x[:, u * LANES:(u + 1) * LANES]
    g_parts = jnp.concatenate([jnp.concatenate(list(_split3(x)) + [zero_c], axis=0) for x in g], axis=1)
    gc_all = _dot(tri3, g_parts)
    gc = [lanes(gc_all, u) for u in range(n_u)]
    gc_parts = jnp.concatenate([_pad_rows(jnp.concatenate(_split3(x), axis=1), LANES) for x in gc], axis=0)
    gc_row_all = lax.dot_general(lane0_3, gc_parts, contract_last, preferred_element_type=F32)
    dec = [jnp.where(incl, jnp.exp(jnp.where(incl, gc[u] - lanes(gc_row_all, u), 0.0)), 0.0) for u in range(n_u)]
    kb = [k[u] * beta[u] for u in range(n_u)]
    gram = [lax.dot_general(jnp.concatenate([kb[u], q[u]], axis=0).astype(BF16), _pad_rows(k[u].astype(BF16), LANES),
                            contract_last, preferred_element_type=F32) for u in range(n_u)]
    a = [jnp.where(strict, gram[u][:c] * dec[u], 0.0) for u in range(n_u)]
    attn = [jnp.where(incl, gram[u][c:] * dec[u], 0.0)[:, :c].astype(BF16) for u in range(n_u)]
    tinv = [eye - x for x in a]
    p = [_dot3w(x, _pad_rows(x, LANES)) for x in a]
    n_factors = c.bit_length() - 2
    for lvl in range(n_factors):
        pw = [_pad_rows(x, LANES) for x in p]
        tinv = [tinv[u] + _dot3w(tinv[u], pw[u]) for u in range(n_u)]
        if lvl + 1 < n_factors:
            p = [_dot3w(p[u], pw[u]) for u in range(n_u)]
    eg = [jnp.exp(x) for x in gc]
    uw = [_dotb(tinv[u][:, :c], jnp.concatenate([v[u] * beta[u], kb[u] * eg[u]], axis=1)) for u in range(n_u)]
    w_qe = [jnp.concatenate([uw[u][:, LANES:], q[u] * eg[u]], axis=0).astype(BF16) for u in range(n_u)]
    g_last = [x[c - 1:c, :] for x in gc]
    kd = [(k[u] * jnp.exp(g_last[u] - gc[u])).astype(BF16) for u in range(n_u)]
    eg_last = [jnp.exp(x) for x in g_last]

    s = [s_ref[0, h] for h in range(hb)]
    o_rows = []
    for ci in range(nc):
        us = [ci * hb + h for h in range(hb)]
        wq = [_dot(w_qe[u], s[h].astype(BF16)) for h, u in enumerate(us)]
        v_new = [(uw[u][:, :LANES] - wq[h][:c]).astype(BF16) for h, u in enumerate(us)]
        o_rows.append(jnp.concatenate([wq[h][c:] + _dot(attn[u], v_new[h]) for h, u in enumerate(us)], axis=1))
        s = [s[h] * eg_last[u] + lax.dot_general(kd[u], v_new[h], contract_first, preferred_element_type=F32)
             for h, u in enumerate(us)]
    o_ref[...] = jnp.concatenate(o_rows, axis=0)
    s_ref[0] = jnp.stack(s)


def gdn_chunked(q, k, v, beta, g, s0, n_seq, t_len, hb, nc):
    n_ch = t_len // (GDN_CHUNK * nc)
    spec = pl.BlockSpec((GDN_CHUNK * nc, hb * LANES), lambda b, h, c: (b * n_ch + c, h))
    sspec = pl.BlockSpec((1, hb, GDN_HEAD_DIM, GDN_HEAD_DIM), lambda b, h, c: (b, h, 0, 0))
    return pl.pallas_call(
        functools.partial(_gdn_chunk_kernel, hb=hb, nc=nc),
        grid=(n_seq, GDN_HEADS // hb, n_ch),
        in_specs=[spec] * 5 + [sspec],
        out_specs=[spec, sspec],
        out_shape=[jax.ShapeDtypeStruct(q.shape, F32), jax.ShapeDtypeStruct(s0.shape, F32)],
        compiler_params=_cparams(3),
        name="gdn_chunk",
    )(q, k, v, beta, g, s0)


def rwkv_mix(p_rwkv, shift_state, wkv_state, prm, n_seq, t_len, row0, tb_prep, tb_scan):
    st8 = _pad_state_rows(shift_state[:, None, :])
    r, kk, d, b, k, v, g = rwkv_prep(p_rwkv, st8, prm, n_seq, t_len, tb_prep, row0)
    dup = max(1, LANES // (n_seq * RWKV_HEADS))
    kl = functools.partial(_to_key_layout, n_seq=n_seq, t_len=t_len, heads=RWKV_HEADS, dj=RWKV_HEAD_DIM, dup=dup)
    vt = _to_value_layout(v, n_seq, t_len, RWKV_HEADS, RWKV_HEAD_DIM, dup)
    s0 = _state_to_layout(wkv_state, dup, value_major=True)
    y, s1 = dplr_scan(kl(kk), kl(d), kl(b), kl(k), kl(r), vt, s0, tb_scan)
    y = _from_value_layout(y, n_seq, t_len, RWKV_HEADS, RWKV_HEAD_DIM, dup)
    out = rwkv_post(y, r, k, v, g, prm, tb_prep)
    return out, _state_from_layout(s1, n_seq, RWKV_HEADS, dup, value_major=True)


def gdn_mix(p_gdn, p_swa, conv_state, ssm_state, prm, n_seq, t_len, row0, tb_prep, tb_scan):
    st8 = _pad_state_rows(conv_state)
    if t_len % GDN_CHUNK == 0:
        kn, g, beta, v, q = gdn_prep(p_gdn, p_swa, st8, prm, n_seq, t_len, tb_prep, row0, chunked=True)
        nc = 4 if t_len % (4 * GDN_CHUNK) == 0 else 1
        o, s1 = gdn_chunked(q, kn, v, beta, g, ssm_state, n_seq, t_len, hb=4, nc=nc)
        return gdn_post(o, p_gdn, prm, tb_prep, row0), s1
    kn, d, b, v, q = gdn_prep(p_gdn, p_swa, st8, prm, n_seq, t_len, tb_prep, row0, chunked=False)
    dup = 2 * max(1, LANES // (2 * n_seq * GDN_HEADS))
    kl = functools.partial(_to_key_layout, n_seq=n_seq, t_len=t_len, heads=GDN_HEADS, dj=GDN_HEAD_DIM, dup=dup)
    vt = _to_value_layout(v, n_seq, t_len, GDN_HEADS, GDN_HEAD_DIM, dup)
    s0 = _state_to_layout(ssm_state, dup, value_major=False)
    knl = kl(kn)
    o, s1 = dplr_scan(knl, kl(d), kl(b), knl, kl(q), vt, s0, tb_scan)
    o = _from_value_layout(o, n_seq, t_len, GDN_HEADS, GDN_HEAD_DIM, dup)
    out = gdn_post(o, p_gdn, prm, tb_prep, row0)
    return out, _state_from_layout(s1, n_seq, GDN_HEADS, dup, value_major=False)


def _block_ones(seg):
    i = jnp.arange(LANES)
    return (i[:, None] // seg == i[None, :] // seg).astype(BF16)


def _head_expander(first_row):
    rows = jnp.arange(2 * LANES)[:, None]
    cols = jnp.arange(GDN_WIDTH)[None, :] // GDN_HEAD_DIM
    return (rows == cols + first_row).astype(BF16)


def _layer_params(l, W):
    row = lambda x: x.reshape(1, -1)
    zeros = jnp.zeros((DECAY_LORA, RWKV_WIDTH), F32)
    wwa = jnp.concatenate([jnp.concatenate([W["rwkv_w2"][l], zeros], axis=1),
                           jnp.concatenate([zeros, W["rwkv_a2"][l]], axis=1)], axis=0)
    lane_row = lambda x, off: jnp.pad(x, (off, 2 * LANES - off - x.shape[0])).reshape(1, 2 * LANES)
    rwkv = dict(mix=row(W["rwkv_shift_mix"][l]), w0=row(W["rwkv_w0"][l]), a0=row(W["rwkv_a0"][l]), wwa=wwa,
                g2=W["rwkv_g2"][l], k_k=row(W["rwkv_k_k"][l]), k_a=row(W["rwkv_k_a"][l]),
                r_k=row(W["rwkv_r_k"][l]), ln_w=row(W["rwkv_ln_w"][l]), ln_b=row(W["rwkv_ln_b"][l]),
                j64=_block_ones(RWKV_HEAD_DIM))
    gdn = dict(conv_w=jnp.pad(W["gdn_conv_w"][l], ((0, SUBLANES - CONV_WIDTH), (0, 0))),
               a_log=lane_row(W["gdn_A_log"][l], GDN_HEADS), dt=lane_row(W["gdn_dt_bias"][l], GDN_HEADS),
               e_beta=_head_expander(0), e_gate=_head_expander(GDN_HEADS),
               norm_w=jnp.tile(W["gdn_norm_w"][l], GDN_HEADS).reshape(1, GDN_WIDTH), j128=_block_ones(LANES))
    return rwkv, gdn


def kernel(x_prompt, x_sample, state_rwkv_shift, state_rwkv_wkv, cache_swa_k, cache_swa_v, state_gdn_conv,
           state_gdn_ssm, norm_mix, w_in, rwkv_shift_mix, rwkv_w0, rwkv_w2, rwkv_a0, rwkv_a2, rwkv_g2, rwkv_k_k,
           rwkv_k_a, rwkv_r_k, rwkv_ln_w, rwkv_ln_b, swa_sinks, gdn_conv_w, gdn_A_log, gdn_dt_bias, gdn_norm_w,
           w_out, norm_ffn, w_gate, w_up, w_down, final_norm):
    W = dict(rwkv_shift_mix=rwkv_shift_mix, rwkv_w0=rwkv_w0, rwkv_w2=rwkv_w2, rwkv_a0=rwkv_a0, rwkv_a2=rwkv_a2,
             rwkv_g2=rwkv_g2, rwkv_k_k=rwkv_k_k, rwkv_k_a=rwkv_k_a, rwkv_r_k=rwkv_r_k, rwkv_ln_w=rwkv_ln_w,
             rwkv_ln_b=rwkv_ln_b, gdn_conv_w=gdn_conv_w, gdn_A_log=gdn_A_log, gdn_dt_bias=gdn_dt_bias,
             gdn_norm_w=gdn_norm_w)
    r0, s0 = RWKV_PROJ, RWKV_PROJ + SWA_PROJ
    w_rwkv = jnp.pad(w_in[:, :, :r0], ((0, 0), (0, 0), (0, RWKV_PAD - RWKV_PROJ))).astype(BF16)
    w_swa = jnp.pad(jnp.concatenate([w_in[:, :, r0:s0], w_in[:, :, s0 + GDN_MAIN:]], axis=2),
                    ((0, 0), (0, 0), (0, SWA_PAD - SWA_PROJ - 2 * GDN_HEADS))).astype(BF16)
    w_gdn = w_in[:, :, s0:s0 + GDN_MAIN].astype(BF16)
    w_out_b, w_gate_b, w_up_b, w_down_b = (w.astype(BF16) for w in (w_out, w_gate, w_up, w_down))

    h = jnp.concatenate([x_prompt.reshape(ROWS_P, D_MODEL), x_sample.reshape(ROWS_D, D_MODEL)], axis=0)
    zero_shift = jnp.zeros((BATCH, RWKV_PROJ), F32)
    zero_wkv = jnp.zeros((BATCH, RWKV_HEADS, RWKV_HEAD_DIM, RWKV_HEAD_DIM), F32)
    zero_conv = jnp.zeros((BATCH, CONV_WIDTH - 1, GDN_CONV_DIM), F32)
    zero_ssm = jnp.zeros((BATCH, GDN_HEADS, GDN_HEAD_DIM, GDN_HEAD_DIM), F32)
    wb = cache_swa_k.shape[2]
    assert wb == WINDOW
    outs = [[] for _ in range(12)]
    bm = 1056
    for l in range(DEPTH):
        rw, gd = _layer_params(l, W)
        u = rmsnorm(h, norm_mix[l], BF16)
        p_rwkv = matmul(u, w_rwkv, l, bm, 512)
        p_swa = matmul(u, w_swa, l, bm, 512)
        p_gdn = matmul(u, w_gdn, l, bm, 512)

        o_rp, wkv_p = rwkv_mix(p_rwkv, zero_shift, zero_wkv, rw, BATCH, SEQ, 0, 128, 64)
        o_rd, wkv_d = rwkv_mix(p_rwkv, state_rwkv_shift[l], state_rwkv_wkv[l], rw, DEC_BATCH, DEC_SEQ, ROWS_P,
                               DEC_SEQ, DEC_SEQ)
        o_sp = swa_attention(swa_sinks[l], p_swa, BATCH, SEQ, WINDOW, 0)
        o_sd = swa_attention(swa_sinks[l], p_swa, DEC_BATCH, DEC_SEQ, DEC_SEQ, ROWS_P,
                             cache_swa_k[l].reshape(DEC_BATCH * wb, SWA_KV_WIDTH),
                             cache_swa_v[l].reshape(DEC_BATCH * wb, SWA_KV_WIDTH))
        o_gp, ssm_p = gdn_mix(p_gdn, p_swa, zero_conv, zero_ssm, gd, BATCH, SEQ, 0, 64, 32)
        o_gd, ssm_d = gdn_mix(p_gdn, p_swa, state_gdn_conv[l], state_gdn_ssm[l], gd, DEC_BATCH, DEC_SEQ, ROWS_P,
                              DEC_SEQ, DEC_SEQ)
        o_r = jnp.concatenate([o_rp, o_rd], axis=0)
        o_s = jnp.concatenate([o_sp, o_sd], axis=0)
        o_g = jnp.concatenate([o_gp, o_gd], axis=0)
        h = matmul_residual([o_r, o_s, o_g], w_out_b, l, h, bm, 512, [RWKV_WIDTH, SWA_WIDTH, GDN_WIDTH])
        u = rmsnorm(h, norm_ffn[l], BF16)
        h1 = swiglu_matmul(u, w_gate_b, w_up_b, l, bm, 512)
        h = matmul_residual([h1], w_down_b, l, h, bm, 512, [D_FF // 2])

        pp = lambda p, w: p[:ROWS_P, :w].reshape(BATCH, SEQ, w)
        pd = lambda p, w: p[ROWS_P:, :w].reshape(DEC_BATCH, DEC_SEQ, w)
        kv = lambda x: x.reshape(x.shape[0], x.shape[1], SWA_KV_HEADS, SWA_HEAD_DIM)
        swa_p, swa_d = pp(p_swa, SWA_PROJ), pd(p_swa, SWA_PROJ)
        k_p, v_p = swa_p[:, SEQ - wb:, SWA_WIDTH:SWA_WIDTH + SWA_KV_WIDTH], swa_p[:, SEQ - wb:, SWA_WIDTH + SWA_KV_WIDTH:]
        k_d, v_d = swa_d[:, :, SWA_WIDTH:SWA_WIDTH + SWA_KV_WIDTH], swa_d[:, :, SWA_WIDTH + SWA_KV_WIDTH:]
        layer_out = (
            pp(p_rwkv, RWKV_PROJ)[:, -1], wkv_p, kv(k_p), kv(v_p),
            pp(p_gdn, GDN_CONV_DIM)[:, SEQ - (CONV_WIDTH - 1):], ssm_p,
            pd(p_rwkv, RWKV_PROJ)[:, -1], wkv_d,
            jnp.concatenate([cache_swa_k[l], kv(k_d)], axis=1)[:, -wb:],
            jnp.concatenate([cache_swa_v[l], kv(v_d)], axis=1)[:, -wb:],
            pd(p_gdn, GDN_CONV_DIM)[:, DEC_SEQ - (CONV_WIDTH - 1):], ssm_d)
        for lst, t in zip(outs, layer_out):
            lst.append(t)
    y = rmsnorm(h, final_norm, F32)
    y_prompt = y[:ROWS_P].reshape(BATCH, SEQ, D_MODEL)
    y_sample = y[ROWS_P:].reshape(DEC_BATCH, DEC_SEQ, D_MODEL)
    return (y_prompt, y_sample) + tuple(jnp.stack(lst) for lst in outs)
```

```python
import functools

import jax
import jax.numpy as jnp
from jax import lax
from jax.experimental import pallas as pl
from jax.experimental.pallas import tpu as pltpu

D_MODEL = 4096
BATCH = 4
SEQ = 2048
DEPTH = 4
DEC_BATCH = 32
DEC_SEQ = 8
NORM_EPS = 1e-6
L2_EPS = 1e-6
RWKV_WIDTH = 1024
RWKV_HEAD_DIM = 64
RWKV_HEADS = 16
DECAY_LORA = 64
AAA_LORA = 64
GATE_LORA = 128
RWKV_PROJ = 3 * RWKV_WIDTH + DECAY_LORA + AAA_LORA + GATE_LORA
RWKV_LN_EPS = 64e-5
SWA_WIDTH = 1024
SWA_HEAD_DIM = 64
SWA_HEADS = 16
SWA_KV_HEADS = 2
SWA_GROUP = 8
SWA_KV_WIDTH = 128
SWA_PROJ = SWA_WIDTH + 2 * SWA_KV_WIDTH
WINDOW = 128
SWA_SCALE = SWA_HEAD_DIM ** -0.5
GDN_WIDTH = 2048
GDN_HEAD_DIM = 128
GDN_HEADS = 16
GDN_CONV_DIM = 3 * GDN_WIDTH
CONV_WIDTH = 4
GDN_CHUNK = 64
GDN_MAIN = GDN_CONV_DIM + GDN_WIDTH
D_FF = 11008

ROWS_P = BATCH * SEQ
ROWS_D = DEC_BATCH * DEC_SEQ
ROWS = ROWS_P + ROWS_D

LANES = 128
SUBLANES = 8
RWKV_PAD = 3584
SWA_PAD = 1536
BA_COL = SWA_PROJ
VMEM_LIMIT = 56 * 1024 * 1024

F32 = jnp.float32
BF16 = jnp.bfloat16


def _cparams(n_axes):
    return pltpu.CompilerParams(dimension_semantics=("arbitrary",) * n_axes, vmem_limit_bytes=VMEM_LIMIT)


def _split(x):
    hi = x.astype(BF16)
    lo = (x - hi.astype(F32)).astype(BF16)
    return hi, lo


def _dot(a, b):
    return jnp.dot(a, b, preferred_element_type=F32)


def _dot_lhs2(x, m_bf16):
    hi, lo = _split(x)
    return _dot(hi, m_bf16) + _dot(lo, m_bf16)


def _dot3(x, w):
    xh, xl = _split(x)
    wh, wl = _split(w)
    return _dot(xh, wh) + (_dot(xl, wh) + _dot(xh, wl))


def _segsum(x, j_bf16):
    n = x.shape[-1] // LANES
    return jnp.concatenate([_dot_lhs2(x[:, c * LANES:(c + 1) * LANES], j_bf16) for c in range(n)], axis=-1)


def _sigmoid(x):
    return 1.0 / (1.0 + jnp.exp(-x))


def _softplus(x):
    return jnp.maximum(x, 0.0) + jnp.log(1.0 + jnp.exp(-jnp.abs(x)))


def _rmsnorm_kernel(x_ref, w_ref, o_ref):
    x = x_ref[...]
    ms = jnp.mean(x * x, axis=-1, keepdims=True)
    o_ref[...] = (x * lax.rsqrt(ms + NORM_EPS) * w_ref[...]).astype(o_ref.dtype)


def rmsnorm(x, w, out_dtype, br=256):
    r, d = x.shape
    return pl.pallas_call(
        _rmsnorm_kernel,
        grid=(r // br,),
        in_specs=[pl.BlockSpec((br, d), lambda i: (i, 0)), pl.BlockSpec((1, d), lambda i: (0, 0))],
        out_specs=pl.BlockSpec((br, d), lambda i: (i, 0)),
        out_shape=jax.ShapeDtypeStruct((r, d), out_dtype),
        compiler_params=_cparams(1),
        name="rmsnorm",
    )(x, w.reshape(1, d))


def _rmsnorm_join_kernel(xp_ref, xs_ref, w_ref, o_ref, h_ref, *, n_p):
    x = jnp.where(pl.program_id(0) < n_p, xp_ref[...], xs_ref[...])
    h_ref[...] = x
    ms = jnp.mean(x * x, axis=-1, keepdims=True)
    o_ref[...] = (x * lax.rsqrt(ms + NORM_EPS) * w_ref[...]).astype(o_ref.dtype)


def rmsnorm_join(xp, xs, w, out_dtype):
    (rp, d), rs = xp.shape, xs.shape[0]
    br = rs
    assert rp % br == 0
    n_p = rp // br
    spec = pl.BlockSpec((br, d), lambda i: (i, 0))
    return pl.pallas_call(
        functools.partial(_rmsnorm_join_kernel, n_p=n_p),
        grid=(n_p + 1,),
        in_specs=[pl.BlockSpec((br, d), lambda i: (jnp.minimum(i, n_p - 1), 0)),
                  pl.BlockSpec((br, d), lambda i: (0, 0)), pl.BlockSpec((1, d), lambda i: (0, 0))],
        out_specs=[spec, spec],
        out_shape=[jax.ShapeDtypeStruct((rp + rs, d), out_dtype), jax.ShapeDtypeStruct((rp + rs, d), xp.dtype)],
        compiler_params=_cparams(1),
        name="rmsnorm_join",
    )(xp, xs, w.reshape(1, d))


def _mm_kernel(a_ref, b_ref, o_ref):
    o_ref[...] = _dot(a_ref[...], b_ref[...]).astype(o_ref.dtype)


def matmul(a, w, layer, bm, bn, out_dtype=F32):
    r, k = a.shape
    n = w.shape[2]
    return pl.pallas_call(
        _mm_kernel,
        grid=(r // bm, n // bn),
        in_specs=[pl.BlockSpec((bm, k), lambda i, j: (i, 0)),
                  pl.BlockSpec((None, k, bn), lambda i, j: (layer, 0, j))],
        out_specs=pl.BlockSpec((bm, bn), lambda i, j: (i, j)),
        out_shape=jax.ShapeDtypeStruct((r, n), out_dtype),
        compiler_params=_cparams(2),
        name="proj_in",
    )(a, w)


def _swiglu_kernel(a_ref, g_ref, u_ref, o_ref):
    a = a_ref[...]
    g = _dot(a, g_ref[...])
    u = _dot(a, u_ref[...])
    o_ref[...] = (g * _sigmoid(g) * u).astype(o_ref.dtype)


def swiglu_matmul(a, wg, wu, layer, bm, bn):
    r, k = a.shape
    n = wg.shape[2]
    wspec = pl.BlockSpec((None, k, bn), lambda i, j: (layer, 0, j))
    return pl.pallas_call(
        _swiglu_kernel,
        grid=(r // bm, pl.cdiv(n, bn)),
        in_specs=[pl.BlockSpec((bm, k), lambda i, j: (i, 0)), wspec, wspec],
        out_specs=pl.BlockSpec((bm, bn), lambda i, j: (i, j)),
        out_shape=jax.ShapeDtypeStruct((r, n), BF16),
        compiler_params=_cparams(2),
        name="ffn_swiglu",
    )(a, wg, wu)


def _mm_res_kernel(*refs, n_a):
    a_refs, b_refs, res_ref, o_ref = refs[:n_a], refs[n_a:2 * n_a], refs[2 * n_a], refs[2 * n_a + 1]
    kk = pl.program_id(2)

    @pl.when(kk == 0)
    def _():
        o_ref[...] = res_ref[...]

    acc = _dot(a_refs[0][...], b_refs[0][...])
    for a_ref, b_ref in zip(a_refs[1:], b_refs[1:]):
        acc = acc + _dot(a_ref[...], b_ref[...])
    o_ref[...] += acc


def matmul_residual(a_list, w, layer, res, bm, bn, bk_list):
    r = res.shape[0]
    n = w.shape[2]
    nk = a_list[0].shape[1] // bk_list[0]
    in_specs, row0 = [], 0
    for a, bk in zip(a_list, bk_list):
        assert a.shape[1] == nk * bk
        in_specs.append(pl.BlockSpec((bm, bk), lambda i, j, k: (i, k)))
    for a, bk in zip(a_list, bk_list):
        assert row0 % bk == 0
        in_specs.append(pl.BlockSpec((None, bk, bn), functools.partial(
            lambda i, j, k, off: (layer, off + k, j), off=row0 // bk)))
        row0 += a.shape[1]
    in_specs.append(pl.BlockSpec((bm, bn), lambda i, j, k: (i, j)))
    return pl.pallas_call(
        functools.partial(_mm_res_kernel, n_a=len(a_list)),
        grid=(r // bm, n // bn, nk),
        in_specs=in_specs,
        out_specs=pl.BlockSpec((bm, bn), lambda i, j, k: (i, j)),
        out_shape=jax.ShapeDtypeStruct((r, n), F32),
        compiler_params=_cparams(3),
        name="proj_residual",
    )(*a_list, *([w] * len(a_list)), res)


def _shifted(prev8, x, k):
    ext = jnp.concatenate([prev8, x], axis=0)
    return pltpu.roll(ext, k, axis=0)[SUBLANES:]


def _seq_specs(width, col_block, tb, n_tb, row0):
    assert row0 % tb == 0 and tb % SUBLANES == 0
    cur = pl.BlockSpec((tb, width), lambda b, t: (row0 // tb + b * n_tb + t, col_block))
    prev = pl.BlockSpec((SUBLANES, width), lambda b, t: (
        jnp.maximum((row0 + (b * n_tb + t) * tb) // SUBLANES - 1, 0), col_block))
    state = pl.BlockSpec((SUBLANES, width), lambda b, t: (b, 0))
    return cur, prev, state


def _row_spec(width, tb, n_tb, row0=0, col_block=0):
    return pl.BlockSpec((tb, width), lambda b, t: (row0 // tb + b * n_tb + t, col_block))


def _const_spec(shape):
    return pl.BlockSpec(shape, lambda b, t: (0,) * len(shape))


def _pad_state_rows(st):
    b, k, c = st.shape
    return jnp.pad(st, ((0, 0), (SUBLANES - k, 0), (0, 0))).reshape(b * SUBLANES, c)


def _rwkv_prep_kernel(cur_ref, prev_ref, st_ref, mix_ref, w0_ref, a0_ref, wwa_ref, g2_ref, kkw_ref, kaw_ref, j_ref,
                      r_o, kk_o, d_o, b_o, k_o, v_o, g_o):
    x = cur_ref[...]
    prev8 = jnp.where(pl.program_id(1) == 0, st_ref[...], prev_ref[...])
    xs = x + (_shifted(prev8, x, 1) - x) * mix_ref[...]
    w3 = RWKV_WIDTH
    r, k, v = xs[:, 0:w3], xs[:, w3:2 * w3], xs[:, 2 * w3:3 * w3]
    wa = xs[:, 3 * w3:3 * w3 + LANES]
    glo = xs[:, 3 * w3 + LANES:3 * w3 + 2 * LANES]
    lane = lax.broadcasted_iota(jnp.int32, wa.shape, 1)
    lora = _dot3(jnp.where(lane < DECAY_LORA, jnp.tanh(wa), wa), wwa_ref[...])
    w = -_softplus(-(w0_ref[...] + lora[:, :w3])) - 0.5
    d = jnp.exp(-jnp.exp(w))
    a = _sigmoid(a0_ref[...] + lora[:, w3:])
    g = _dot3(_sigmoid(glo), g2_ref[...])
    kn = k * kkw_ref[...]
    kk = kn * lax.rsqrt(_segsum(kn * kn, j_ref[...]) + L2_EPS)
    r_o[...] = r
    kk_o[...] = kk
    d_o[...] = d
    b_o[...] = kk * a
    k_o[...] = k * (1.0 + (a - 1.0) * kaw_ref[...])
    v_o[...] = v
    g_o[...] = g


def rwkv_prep(p_rwkv, st8, prm, n_seq, t_len, tb, row0):
    n_tb = t_len // tb
    cur, prev, state = _seq_specs(RWKV_PROJ, 0, tb, n_tb, row0)
    out = jax.ShapeDtypeStruct((n_seq * t_len, RWKV_WIDTH), F32)
    ospec = _row_spec(RWKV_WIDTH, tb, n_tb)
    return pl.pallas_call(
        _rwkv_prep_kernel,
        grid=(n_seq, n_tb),
        in_specs=[cur, prev, state, _const_spec((1, RWKV_PROJ)), _const_spec((1, RWKV_WIDTH)),
                  _const_spec((1, RWKV_WIDTH)), _const_spec((LANES, 2 * RWKV_WIDTH)),
                  _const_spec((GATE_LORA, RWKV_WIDTH)), _const_spec((1, RWKV_WIDTH)), _const_spec((1, RWKV_WIDTH)),
                  _const_spec((LANES, LANES))],
        out_specs=[ospec] * 7,
        out_shape=[out] * 7,
        compiler_params=_cparams(2),
        name="rwkv_prep",
    )(p_rwkv, p_rwkv, st8, prm["mix"], prm["w0"], prm["a0"], prm["wwa"], prm["g2"], prm["k_k"], prm["k_a"],
      prm["j64"])


def _rwkv_post_kernel(y_ref, r_ref, k_ref, v_ref, g_ref, lnw_ref, lnb_ref, rk_ref, j_ref, o_ref):
    j = j_ref[...]
    y = y_ref[...]
    inv_n = 1.0 / RWKV_HEAD_DIM
    yc = y - _segsum(y, j) * inv_n
    var = _segsum(yc * yc, j) * inv_n
    out = yc * lax.rsqrt(var + RWKV_LN_EPS) * lnw_ref[...] + lnb_ref[...]
    out = out + _segsum(r_ref[...] * k_ref[...] * rk_ref[...], j) * v_ref[...]
    o_ref[...] = (out * g_ref[...]).astype(o_ref.dtype)


def rwkv_post(y, r, k, v, g, prm, tb):
    rows = y.shape[0]
    spec = pl.BlockSpec((tb, RWKV_WIDTH), lambda i: (i, 0))
    cspec = pl.BlockSpec((1, RWKV_WIDTH), lambda i: (0, 0))
    return pl.pallas_call(
        _rwkv_post_kernel,
        grid=(rows // tb,),
        in_specs=[spec] * 5 + [cspec] * 3 + [pl.BlockSpec((LANES, LANES), lambda i: (0, 0))],
        out_specs=spec,
        out_shape=jax.ShapeDtypeStruct((rows, RWKV_WIDTH), BF16),
        compiler_params=_cparams(1),
        name="rwkv_post",
    )(y, r, k, v, g, prm["ln_w"], prm["ln_b"], prm["r_k"], prm["j64"])


def _scan_kernel(kk_ref, d_ref, b_ref, k_ref, r_ref, v_ref, s0_ref, y_ref, s_ref, *, tb, dj, di):
    ng = di // SUBLANES

    @pl.when(pl.program_id(1) == 0)
    def _():
        s_ref[...] = s0_ref[...]

    def row(ref, t, j):
        return ref[0, t, pl.ds(j, 1), :]

    def sl(g):
        return pl.ds(g * SUBLANES, SUBLANES)

    zero = tuple(jnp.zeros((SUBLANES, LANES), F32) for _ in range(ng))

    def first_dot(j, acc):
        kkj = row(kk_ref, 0, j)
        return tuple(acc[g] + s_ref[0, j, sl(g), :] * kkj for g in range(ng))

    def step(t, s_kk):
        sa = [-a for a in s_kk]
        v = [v_ref[0, t, sl(g), :] for g in range(ng)]
        t_next = jnp.minimum(t + 1, tb - 1)

        def update(j, carry):
            yacc, nacc = carry
            dj_, bj, kj, rj = row(d_ref, t, j), row(b_ref, t, j), row(k_ref, t, j), row(r_ref, t, j)
            kkn = row(kk_ref, t_next, j)
            y_out, n_out = [], []
            for g in range(ng):
                s = s_ref[0, j, sl(g), :] * dj_ + sa[g] * bj + v[g] * kj
                s_ref[0, j, sl(g), :] = s
                y_out.append(yacc[g] + s * rj)
                n_out.append(nacc[g] + s * kkn)
            return tuple(y_out), tuple(n_out)

        yacc, nacc = lax.fori_loop(0, dj, update, (zero, zero), unroll=8)
        for g in range(ng):
            y_ref[0, t, sl(g), :] = yacc[g]
        return nacc

    lax.fori_loop(0, tb, step, lax.fori_loop(0, dj, first_dot, zero, unroll=8))


def dplr_scan(kk, d, b, k, r, v, s0, tb):
    g_n, t_len, dj, _ = kk.shape
    di = v.shape[2]
    jspec = pl.BlockSpec((1, tb, dj, LANES), lambda g, t: (g, t, 0, 0))
    ispec = pl.BlockSpec((1, tb, di, LANES), lambda g, t: (g, t, 0, 0))
    sspec = pl.BlockSpec((1, dj, di, LANES), lambda g, t: (g, 0, 0, 0))
    return pl.pallas_call(
        functools.partial(_scan_kernel, tb=tb, dj=dj, di=di),
        grid=(g_n, t_len // tb),
        in_specs=[jspec] * 5 + [ispec, sspec],
        out_specs=[ispec, sspec],
        out_shape=[jax.ShapeDtypeStruct(v.shape, F32), jax.ShapeDtypeStruct(s0.shape, F32)],
        compiler_params=_cparams(2),
        name="dplr_scan",
    )(kk, d, b, k, r, v, s0)


def _to_key_layout(x, n_seq, t_len, heads, dj, dup):
    x = jnp.broadcast_to(x.reshape(1, n_seq, t_len, heads, dj), (dup, n_seq, t_len, heads, dj))
    x = x.transpose(2, 4, 0, 1, 3)
    g_n = dup * n_seq * heads // LANES
    return x.reshape(t_len, dj, g_n, LANES).transpose(2, 0, 1, 3)


def _to_value_layout(x, n_seq, t_len, heads, dv, dup):
    di = dv // dup
    x = x.reshape(n_seq, t_len, heads, dup, di).transpose(1, 4, 3, 0, 2)
    g_n = dup * n_seq * heads // LANES
    return x.reshape(t_len, di, g_n, LANES).transpose(2, 0, 1, 3)


def _from_value_layout(y, n_seq, t_len, heads, dv, dup):
    g_n, _, di, _ = y.shape
    y = y.transpose(1, 2, 0, 3).reshape(t_len, di, dup, n_seq, heads)
    return y.transpose(3, 0, 4, 2, 1).reshape(n_seq * t_len, heads * dv)


def _state_to_layout(s, dup, value_major):
    bsz, heads = s.shape[:2]
    if value_major:
        dv, dj = s.shape[2:]
        s = s.reshape(bsz, heads, dup, dv // dup, dj).transpose(4, 3, 2, 0, 1)
    else:
        dj, dv = s.shape[2:]
        s = s.reshape(bsz, heads, dj, dup, dv // dup).transpose(2, 4, 3, 0, 1)
    g_n = dup * bsz * heads // LANES
    return s.reshape(dj, dv // dup, g_n, LANES).transpose(2, 0, 1, 3)


def _state_from_layout(s, bsz, heads, dup, value_major):
    g_n, dj, di, _ = s.shape
    s = s.transpose(1, 2, 0, 3).reshape(dj, di, dup, bsz, heads)
    if value_major:
        return s.transpose(3, 4, 2, 1, 0).reshape(bsz, heads, dup * di, dj)
    return s.transpose(3, 4, 0, 2, 1).reshape(bsz, heads, dj, dup * di)


def _swa_kernel(sink_ref, q_ref, kc_ref, vc_ref, kp_ref, vp_ref, o_ref, *, tq, has_cache):
    s_len = WINDOW + tq
    kfull = jnp.concatenate([kp_ref[...], kc_ref[...]], axis=0)
    vfull = jnp.concatenate([vp_ref[...], vc_ref[...]], axis=0)
    lane = lax.broadcasted_iota(jnp.int32, kfull.shape, 1)
    low = lane < SWA_HEAD_DIM
    kswap = pltpu.roll(kfull, SWA_HEAD_DIM, axis=1)
    vswap = pltpu.roll(vfull, SWA_HEAD_DIM, axis=1)
    k_lo = [jnp.where(low, kfull, 0.0), jnp.where(low, kswap, 0.0)]
    k_hi = [jnp.where(low, 0.0, kswap), jnp.where(low, 0.0, kfull)]
    v_lo = [jnp.where(low, vfull, 0.0), jnp.where(low, vswap, 0.0)]
    v_hi = [jnp.where(low, 0.0, vswap), jnp.where(low, 0.0, vfull)]
    t_idx = lax.broadcasted_iota(jnp.int32, (tq, s_len), 0)
    s_idx = lax.broadcasted_iota(jnp.int32, (tq, s_len), 1)
    delta = WINDOW + t_idx - s_idx
    valid = (delta >= 0) & (delta <= WINDOW)
    if not has_cache:
        valid = valid & ((s_idx >= WINDOW) | (pl.program_id(1) > 0))
    contract_last = (((1,), (1,)), ((), ()))
    for hp in range(SWA_HEADS // 2):
        qp = q_ref[:, hp * LANES:(hp + 1) * LANES]
        kv = (2 * hp) // SWA_GROUP
        acc = None
        for half, (kmat, vmat) in enumerate(((k_lo[kv], v_lo[kv]), (k_hi[kv], v_hi[kv]))):
            sink = sink_ref[2 * hp + half]
            s = lax.dot_general(qp, kmat, contract_last, preferred_element_type=F32) * SWA_SCALE
            s = jnp.where(valid, s, -jnp.inf)
            m = jnp.maximum(jnp.max(s, axis=-1, keepdims=True), sink)
            e = jnp.exp(s - m)
            den = jnp.sum(e, axis=-1, keepdims=True) + jnp.exp(sink - m)
            o = _dot(e, vmat) / den
            acc = o if acc is None else acc + o
        o_ref[:, hp * LANES:(hp + 1) * LANES] = acc.astype(o_ref.dtype)


def swa_attention(sinks, p_swa, n_seq, t_len, tq, row0, cache_k=None, cache_v=None):
    n_blk = t_len // tq
    has_cache = cache_k is not None
    kcol, vcol = SWA_WIDTH // LANES, SWA_WIDTH // LANES + 1
    q_spec = _row_spec(SWA_WIDTH, tq, n_blk, row0)
    kc_spec = _row_spec(LANES, tq, n_blk, row0, kcol)
    vc_spec = _row_spec(LANES, tq, n_blk, row0, vcol)
    if has_cache:
        kp_spec = vp_spec = pl.BlockSpec((WINDOW, LANES), lambda b, t: (b, 0))
        kp_arr, vp_arr = cache_k, cache_v
    else:
        assert tq == WINDOW and row0 == 0

        def prev_rows(b, t):
            return jnp.maximum(b * n_blk + t - 1, 0)

        kp_spec = pl.BlockSpec((WINDOW, LANES), lambda b, t: (prev_rows(b, t), kcol))
        vp_spec = pl.BlockSpec((WINDOW, LANES), lambda b, t: (prev_rows(b, t), vcol))
        kp_arr = vp_arr = p_swa
    return pl.pallas_call(
        functools.partial(_swa_kernel, tq=tq, has_cache=has_cache),
        grid=(n_seq, n_blk),
        in_specs=[pl.BlockSpec(memory_space=pltpu.SMEM), q_spec, kc_spec, vc_spec, kp_spec, vp_spec],
        out_specs=_row_spec(SWA_WIDTH, tq, n_blk),
        out_shape=jax.ShapeDtypeStruct((n_seq * t_len, SWA_WIDTH), BF16),
        compiler_params=_cparams(2),
        name="swa",
    )(sinks, p_swa, p_swa, p_swa, kp_arr, vp_arr)


def _gdn_prep_kernel(cur_ref, prev_ref, st_ref, ba_ref, cw_ref, alog_ref, dt_ref, eb_ref, eg_ref, j_ref,
                     kn_o, d_o, b_o, v_o, q_o, *, chunked):
    x = cur_ref[...]
    prev8 = jnp.where(pl.program_id(1) == 0, st_ref[...], prev_ref[...])
    y = x * cw_ref[3:4, :]
    for k in range(1, CONV_WIDTH):
        y = y + _shifted(prev8, x, k) * cw_ref[3 - k:4 - k, :]
    act = y * _sigmoid(y)
    w2 = GDN_WIDTH
    q, k, v = act[:, :w2], act[:, w2:2 * w2], act[:, 2 * w2:]
    j = j_ref[...]
    qn = q * lax.rsqrt(_segsum(q * q, j) + L2_EPS) * (GDN_HEAD_DIM ** -0.5)
    kn = k * lax.rsqrt(_segsum(k * k, j) + L2_EPS)
    ba = ba_ref[...]
    beta = _dot_lhs2(_sigmoid(ba), eb_ref[...])
    gate = _dot_lhs2(-jnp.exp(alog_ref[...]) * _softplus(ba + dt_ref[...]), eg_ref[...])
    kn_o[...] = kn
    q_o[...] = qn
    if chunked:
        d_o[...] = gate
        b_o[...] = beta
        v_o[...] = v
    else:
        eg = jnp.exp(gate)
        d_o[...] = eg
        b_o[...] = eg * beta * kn
        v_o[...] = beta * v


def gdn_prep(p_gdn, p_swa, st8, prm, n_seq, t_len, tb, row0, chunked):
    n_tb = t_len // tb
    cur, prev, state = _seq_specs(GDN_CONV_DIM, 0, tb, n_tb, row0)
    ba_spec = _row_spec(2 * LANES, tb, n_tb, row0, BA_COL // (2 * LANES))
    out = jax.ShapeDtypeStruct((n_seq * t_len, GDN_WIDTH), F32)
    ospec = _row_spec(GDN_WIDTH, tb, n_tb)
    return pl.pallas_call(
        functools.partial(_gdn_prep_kernel, chunked=chunked),
        grid=(n_seq, n_tb),
        in_specs=[cur, prev, state, ba_spec, _const_spec((SUBLANES, GDN_CONV_DIM)), _const_spec((1, 2 * LANES)),
                  _const_spec((1, 2 * LANES)), _const_spec((2 * LANES, GDN_WIDTH)),
                  _const_spec((2 * LANES, GDN_WIDTH)), _const_spec((LANES, LANES))],
        out_specs=[ospec] * 5,
        out_shape=[out] * 5,
        compiler_params=_cparams(2),
        name="gdn_prep",
    )(p_gdn, p_gdn, st8, p_swa, prm["conv_w"], prm["a_log"], prm["dt"], prm["e_beta"], prm["e_gate"], prm["j128"])


def _gdn_post_kernel(o_ref_in, z_ref, w_ref, j_ref, o_ref):
    o = o_ref_in[...]
    z = z_ref[...]
    ms = _segsum(o * o, j_ref[...]) * (1.0 / GDN_HEAD_DIM)
    y = o * lax.rsqrt(ms + NORM_EPS) * w_ref[...]
    o_ref[...] = (y * (z * _sigmoid(z))).astype(o_ref.dtype)


def gdn_post(o, p_gdn, prm, tb, row0):
    rows = o.shape[0]
    spec = pl.BlockSpec((tb, GDN_WIDTH), lambda i: (i, 0))
    zspec = pl.BlockSpec((tb, GDN_WIDTH), lambda i: (row0 // tb + i, GDN_CONV_DIM // GDN_WIDTH))
    return pl.pallas_call(
        _gdn_post_kernel,
        grid=(rows // tb,),
        in_specs=[spec, zspec, pl.BlockSpec((1, GDN_WIDTH), lambda i: (0, 0)),
                  pl.BlockSpec((LANES, LANES), lambda i: (0, 0))],
        out_specs=spec,
        out_shape=jax.ShapeDtypeStruct((rows, GDN_WIDTH), BF16),
        compiler_params=_cparams(1),
        name="gdn_post",
    )(o, p_gdn, prm["norm_w"], prm["j128"])


def _split3(x):
    hi = x.astype(BF16)
    r = x - hi.astype(F32)
    mid = r.astype(BF16)
    return hi, mid, (r - mid.astype(F32)).astype(BF16)


def _dotb(a, b):
    return _dot(a.astype(BF16), b.astype(BF16))


def _dot3_packed(x, w, c):
    xh = x.astype(BF16).astype(F32)
    hi_lo = xh + pltpu.roll(x - xh, c, axis=1)
    wh, wl = _split(w)
    lhs = jnp.concatenate([hi_lo, xh], axis=1).astype(BF16)
    rhs = jnp.concatenate([wh, wh, wl, jnp.zeros_like(wh)], axis=0)
    return _dot(lhs, rhs)


def _pad_rows(x, rows):
    return jnp.concatenate([x, jnp.zeros((rows - x.shape[0], x.shape[1]), x.dtype)], axis=0)


def _gdn_chunk_kernel(q_ref, k_ref, v_ref, beta_ref, g_ref, s0_ref, o_ref, s_ref, *, hb, nc):
    c = GDN_CHUNK

    @pl.when(pl.program_id(2) == 0)
    def _():
        s_ref[...] = s0_ref[...]

    row = lax.broadcasted_iota(jnp.int32, (c, LANES), 0)
    col = lax.broadcasted_iota(jnp.int32, (c, LANES), 1)
    incl, strict = row >= col, row > col
    eye = (row == col).astype(F32)
    row3 = lax.broadcasted_iota(jnp.int32, (c, 2 * LANES), 0)
    col3 = lax.broadcasted_iota(jnp.int32, (c, 2 * LANES), 1)
    tri3 = ((row3 >= col3 % c) & (col3 < 3 * c)).astype(BF16)
    lane0_3 = (lax.broadcasted_iota(jnp.int32, (c, 3 * LANES), 1) % LANES == 0).astype(BF16)
    zero_c = jnp.zeros((c, LANES), BF16)
    contract_last = (((1,), (1,)), ((), ()))
    contract_first = (((0,), (0,)), ((), ()))

    units = [(h, ci) for ci in range(nc) for h in range(hb)]
    n_u = len(units)

    def load(ref):
        return [ref[ci * c:(ci + 1) * c, h * LANES:(h + 1) * LANES] for h, ci in units]

    q, k, v, beta, g = load(q_ref), load(k_ref), load(v_ref), load(beta_ref), load(g_ref)
    lanes = lambda x, u: x[:, u * LANES:(u + 1) * LANES]
    g_parts = jnp.concatenate([jnp.concatenate(list(_split3(x)) + [zero_c], axis=0) for x in g], axis=1)
    gc_all = _dot(tri3, g_parts)
    gc = [lanes(gc_all, u) for u in range(n_u)]
    gc_parts = jnp.concatenate([_pad_rows(jnp.concatenate(_split3(x), axis=1), LANES) for x in gc], axis=0)
    gc_row_all = lax.dot_general(lane0_3, gc_parts, contract_last, preferred_element_type=F32)
    dec = [jnp.where(incl, jnp.exp(jnp.where(incl, gc[u] - lanes(gc_row_all, u), 0.0)), 0.0) for u in range(n_u)]
    kb = [k[u] * beta[u] for u in range(n_u)]
    gram = [lax.dot_general(jnp.concatenate([kb[u], q[u]], axis=0).astype(BF16), _pad_rows(k[u].astype(BF16), LANES),
                            contract_last, preferred_element_type=F32) for u in range(n_u)]
    a = [jnp.where(strict, gram[u][:c] * dec[u], 0.0) for u in range(n_u)]
    attn = [jnp.where(incl, gram[u][c:] * dec[u], 0.0)[:, :c].astype(BF16) for u in range(n_u)]
    tinv = [eye - x for x in a]
    p = [_dot3_packed(x, x, c) for x in a]
    n_levels = c.bit_length() - 2
    for lvl in range(n_levels):
        last = lvl + 1 == n_levels
        lhs = tinv if last else [jnp.concatenate([tinv[u], p[u]], axis=0) for u in range(n_u)]
        prod = [_dot3_packed(lhs[u], p[u], c) for u in range(n_u)]
        tinv = [tinv[u] + prod[u][:c] for u in range(n_u)]
        if not last:
            p = [prod[u][c:] for u in range(n_u)]
    eg = [jnp.exp(x) for x in gc]
    uw = [_dotb(tinv[u][:, :c], jnp.concatenate([v[u] * beta[u], kb[u] * eg[u]], axis=1)) for u in range(n_u)]
    w_qe = [jnp.concatenate([uw[u][:, LANES:], q[u] * eg[u]], axis=0).astype(BF16) for u in range(n_u)]
    g_last = [x[c - 1:c, :] for x in gc]
    kd = [(k[u] * jnp.exp(g_last[u] - gc[u])).astype(BF16) for u in range(n_u)]
    eg_last = [jnp.exp(x) for x in g_last]

    s = [s_ref[0, h] for h in range(hb)]
    o_rows = []
    for ci in range(nc):
        us = [ci * hb + h for h in range(hb)]
        wq = [_dot(w_qe[u], s[h].astype(BF16)) for h, u in enumerate(us)]
        v_new = [(uw[u][:, :LANES] - wq[h][:c]).astype(BF16) for h, u in enumerate(us)]
        o_rows.append(jnp.concatenate([wq[h][c:] + _dot(attn[u], v_new[h]) for h, u in enumerate(us)], axis=1))
        s = [s[h] * eg_last[u] + lax.dot_general(kd[u], v_new[h], contract_first, preferred_element_type=F32)
             for h, u in enumerate(us)]
    o_ref[...] = jnp.concatenate(o_rows, axis=0)
    s_ref[0] = jnp.stack(s)


def gdn_chunked(q, k, v, beta, g, s0, n_seq, t_len, hb, nc):
    n_ch = t_len // (GDN_CHUNK * nc)
    spec = pl.BlockSpec((GDN_CHUNK * nc, hb * LANES), lambda b, h, c: (b * n_ch + c, h))
    sspec = pl.BlockSpec((1, hb, GDN_HEAD_DIM, GDN_HEAD_DIM), lambda b, h, c: (b, h, 0, 0))
    return pl.pallas_call(
        functools.partial(_gdn_chunk_kernel, hb=hb, nc=nc),
        grid=(n_seq, GDN_HEADS // hb, n_ch),
        in_specs=[spec] * 5 + [sspec],
        out_specs=[spec, sspec],
        out_shape=[jax.ShapeDtypeStruct(q.shape, F32), jax.ShapeDtypeStruct(s0.shape, F32)],
        compiler_params=_cparams(3),
        name="gdn_chunk",
    )(q, k, v, beta, g, s0)


def rwkv_mix(p_rwkv, shift_state, wkv_state, prm, n_seq, t_len, row0, tb_prep, tb_scan):
    st8 = _pad_state_rows(shift_state[:, None, :])
    r, kk, d, b, k, v, g = rwkv_prep(p_rwkv, st8, prm, n_seq, t_len, tb_prep, row0)
    dup = max(1, LANES // (n_seq * RWKV_HEADS))
    kl = functools.partial(_to_key_layout, n_seq=n_seq, t_len=t_len, heads=RWKV_HEADS, dj=RWKV_HEAD_DIM, dup=dup)
    vt = _to_value_layout(v, n_seq, t_len, RWKV_HEADS, RWKV_HEAD_DIM, dup)
    s0 = _state_to_layout(wkv_state, dup, value_major=True)
    y, s1 = dplr_scan(kl(kk), kl(d), kl(b), kl(k), kl(r), vt, s0, tb_scan)
    y = _from_value_layout(y, n_seq, t_len, RWKV_HEADS, RWKV_HEAD_DIM, dup)
    out = rwkv_post(y, r, k, v, g, prm, tb_prep)
    return out, _state_from_layout(s1, n_seq, RWKV_HEADS, dup, value_major=True)


def gdn_mix(p_gdn, p_swa, conv_state, ssm_state, prm, n_seq, t_len, row0, tb_prep, tb_scan):
    st8 = _pad_state_rows(conv_state)
    if t_len % GDN_CHUNK == 0:
        kn, g, beta, v, q = gdn_prep(p_gdn, p_swa, st8, prm, n_seq, t_len, tb_prep, row0, chunked=True)
        nc = 4 if t_len % (4 * GDN_CHUNK) == 0 else 1
        o, s1 = gdn_chunked(q, kn, v, beta, g, ssm_state, n_seq, t_len, hb=4, nc=nc)
        return gdn_post(o, p_gdn, prm, tb_prep, row0), s1
    kn, d, b, v, q = gdn_prep(p_gdn, p_swa, st8, prm, n_seq, t_len, tb_prep, row0, chunked=False)
    dup = 2 * max(1, LANES // (2 * n_seq * GDN_HEADS))
    kl = functools.partial(_to_key_layout, n_seq=n_seq, t_len=t_len, heads=GDN_HEADS, dj=GDN_HEAD_DIM, dup=dup)
    vt = _to_value_layout(v, n_seq, t_len, GDN_HEADS, GDN_HEAD_DIM, dup)
    s0 = _state_to_layout(ssm_state, dup, value_major=False)
    knl = kl(kn)
    o, s1 = dplr_scan(knl, kl(d), kl(b), knl, kl(q), vt, s0, tb_scan)
    o = _from_value_layout(o, n_seq, t_len, GDN_HEADS, GDN_HEAD_DIM, dup)
    out = gdn_post(o, p_gdn, prm, tb_prep, row0)
    return out, _state_from_layout(s1, n_seq, GDN_HEADS, dup, value_major=False)


def _block_ones(seg):
    i = jnp.arange(LANES)
    return (i[:, None] // seg == i[None, :] // seg).astype(BF16)


def _head_expander(first_row):
    rows = jnp.arange(2 * LANES)[:, None]
    cols = jnp.arange(GDN_WIDTH)[None, :] // GDN_HEAD_DIM
    return (rows == cols + first_row).astype(BF16)


def _layer_params(l, W):
    row = lambda x: x.reshape(1, -1)
    zeros = jnp.zeros((DECAY_LORA, RWKV_WIDTH), F32)
    wwa = jnp.concatenate([jnp.concatenate([W["rwkv_w2"][l], zeros], axis=1),
                           jnp.concatenate([zeros, W["rwkv_a2"][l]], axis=1)], axis=0)
    lane_row = lambda x, off: jnp.pad(x, (off, 2 * LANES - off - x.shape[0])).reshape(1, 2 * LANES)
    rwkv = dict(mix=row(W["rwkv_shift_mix"][l]), w0=row(W["rwkv_w0"][l]), a0=row(W["rwkv_a0"][l]), wwa=wwa,
                g2=W["rwkv_g2"][l], k_k=row(W["rwkv_k_k"][l]), k_a=row(W["rwkv_k_a"][l]),
                r_k=row(W["rwkv_r_k"][l]), ln_w=row(W["rwkv_ln_w"][l]), ln_b=row(W["rwkv_ln_b"][l]),
                j64=_block_ones(RWKV_HEAD_DIM))
    gdn = dict(conv_w=jnp.pad(W["gdn_conv_w"][l], ((0, SUBLANES - CONV_WIDTH), (0, 0))),
               a_log=lane_row(W["gdn_A_log"][l], GDN_HEADS), dt=lane_row(W["gdn_dt_bias"][l], GDN_HEADS),
               e_beta=_head_expander(0), e_gate=_head_expander(GDN_HEADS),
               norm_w=jnp.tile(W["gdn_norm_w"][l], GDN_HEADS).reshape(1, GDN_WIDTH), j128=_block_ones(LANES))
    return rwkv, gdn


def kernel(x_prompt, x_sample, state_rwkv_shift, state_rwkv_wkv, cache_swa_k, cache_swa_v, state_gdn_conv,
           state_gdn_ssm, norm_mix, w_in, rwkv_shift_mix, rwkv_w0, rwkv_w2, rwkv_a0, rwkv_a2, rwkv_g2, rwkv_k_k,
           rwkv_k_a, rwkv_r_k, rwkv_ln_w, rwkv_ln_b, swa_sinks, gdn_conv_w, gdn_A_log, gdn_dt_bias, gdn_norm_w,
           w_out, norm_ffn, w_gate, w_up, w_down, final_norm):
    W = dict(rwkv_shift_mix=rwkv_shift_mix, rwkv_w0=rwkv_w0, rwkv_w2=rwkv_w2, rwkv_a0=rwkv_a0, rwkv_a2=rwkv_a2,
             rwkv_g2=rwkv_g2, rwkv_k_k=rwkv_k_k, rwkv_k_a=rwkv_k_a, rwkv_r_k=rwkv_r_k, rwkv_ln_w=rwkv_ln_w,
             rwkv_ln_b=rwkv_ln_b, gdn_conv_w=gdn_conv_w, gdn_A_log=gdn_A_log, gdn_dt_bias=gdn_dt_bias,
             gdn_norm_w=gdn_norm_w)
    r0, s0 = RWKV_PROJ, RWKV_PROJ + SWA_PROJ
    w_rwkv = jnp.pad(w_in[:, :, :r0], ((0, 0), (0, 0), (0, RWKV_PAD - RWKV_PROJ))).astype(BF16)
    w_swa = jnp.pad(jnp.concatenate([w_in[:, :, r0:s0], w_in[:, :, s0 + GDN_MAIN:]], axis=2),
                    ((0, 0), (0, 0), (0, SWA_PAD - SWA_PROJ - 2 * GDN_HEADS))).astype(BF16)
    w_gdn = w_in[:, :, s0:s0 + GDN_MAIN].astype(BF16)
    w_out_b, w_gate_b, w_up_b, w_down_b = (w.astype(BF16) for w in (w_out, w_gate, w_up, w_down))

    u, h = rmsnorm_join(x_prompt.reshape(ROWS_P, D_MODEL), x_sample.reshape(ROWS_D, D_MODEL), norm_mix[0], BF16)
    zero_shift = jnp.zeros((BATCH, RWKV_PROJ), F32)
    zero_wkv = jnp.zeros((BATCH, RWKV_HEADS, RWKV_HEAD_DIM, RWKV_HEAD_DIM), F32)
    zero_conv = jnp.zeros((BATCH, CONV_WIDTH - 1, GDN_CONV_DIM), F32)
    zero_ssm = jnp.zeros((BATCH, GDN_HEADS, GDN_HEAD_DIM, GDN_HEAD_DIM), F32)
    wb = cache_swa_k.shape[2]
    assert wb == WINDOW
    outs = [[] for _ in range(12)]
    bm = 1056
    for l in range(DEPTH):
        rw, gd = _layer_params(l, W)
        if l > 0:
            u = rmsnorm(h, norm_mix[l], BF16)
        p_rwkv = matmul(u, w_rwkv, l, bm, 512)
        p_swa = matmul(u, w_swa, l, bm, 512)
        p_gdn = matmul(u, w_gdn, l, bm, 512)

        o_rp, wkv_p = rwkv_mix(p_rwkv, zero_shift, zero_wkv, rw, BATCH, SEQ, 0, 128, 64)
        o_rd, wkv_d = rwkv_mix(p_rwkv, state_rwkv_shift[l], state_rwkv_wkv[l], rw, DEC_BATCH, DEC_SEQ, ROWS_P,
                               DEC_SEQ, DEC_SEQ)
        o_sp = swa_attention(swa_sinks[l], p_swa, BATCH, SEQ, WINDOW, 0)
        o_sd = swa_attention(swa_sinks[l], p_swa, DEC_BATCH, DEC_SEQ, DEC_SEQ, ROWS_P,
                             cache_swa_k[l].reshape(DEC_BATCH * wb, SWA_KV_WIDTH),
                             cache_swa_v[l].reshape(DEC_BATCH * wb, SWA_KV_WIDTH))
        o_gp, ssm_p = gdn_mix(p_gdn, p_swa, zero_conv, zero_ssm, gd, BATCH, SEQ, 0, 64, 32)
        o_gd, ssm_d = gdn_mix(p_gdn, p_swa, state_gdn_conv[l], state_gdn_ssm[l], gd, DEC_BATCH, DEC_SEQ, ROWS_P,
                              DEC_SEQ, DEC_SEQ)
        o_r = jnp.concatenate([o_rp, o_rd], axis=0)
        o_s = jnp.concatenate([o_sp, o_sd], axis=0)
        o_g = jnp.concatenate([o_gp, o_gd], axis=0)
        h = matmul_residual([o_r, o_s, o_g], w_out_b, l, h, bm, 512, [RWKV_WIDTH, SWA_WIDTH, GDN_WIDTH])
        u = rmsnorm(h, norm_ffn[l], BF16)
        h1 = swiglu_matmul(u, w_gate_b, w_up_b, l, bm, 512)
        h = matmul_residual([h1], w_down_b, l, h, bm, 512, [D_FF // 2])

        def tail_p(p, n_rows, c0, c1):
            return jnp.stack([lax.slice(p, ((b + 1) * SEQ - n_rows, c0), ((b + 1) * SEQ, c1)) for b in range(BATCH)])

        def tail_d(p, n_rows, c0, c1):
            x = lax.slice(p, (ROWS_P, c0), (ROWS, c1)).reshape(DEC_BATCH, DEC_SEQ, c1 - c0)
            return x[:, DEC_SEQ - n_rows:]

        kv = lambda x: x.reshape(x.shape[0], x.shape[1], SWA_KV_HEADS, SWA_HEAD_DIM)
        k0, k1, v1 = SWA_WIDTH, SWA_WIDTH + SWA_KV_WIDTH, SWA_PROJ
        layer_out = (
            tail_p(p_rwkv, 1, 0, RWKV_PROJ)[:, 0], wkv_p, kv(tail_p(p_swa, wb, k0, k1)), kv(tail_p(p_swa, wb, k1, v1)),
            tail_p(p_gdn, CONV_WIDTH - 1, 0, GDN_CONV_DIM), ssm_p,
            tail_d(p_rwkv, 1, 0, RWKV_PROJ)[:, 0], wkv_d,
            jnp.concatenate([cache_swa_k[l], kv(tail_d(p_swa, DEC_SEQ, k0, k1))], axis=1)[:, -wb:],
            jnp.concatenate([cache_swa_v[l], kv(tail_d(p_swa, DEC_SEQ, k1, v1))], axis=1)[:, -wb:],
            tail_d(p_gdn, CONV_WIDTH - 1, 0, GDN_CONV_DIM), ssm_d)
        for lst, t in zip(outs, layer_out):
            lst.append(t)
    y = rmsnorm(h, final_norm, F32)
    y_prompt = y[:ROWS_P].reshape(BATCH, SEQ, D_MODEL)
    y_sample = y[ROWS_P:].reshape(DEC_BATCH, DEC_SEQ, D_MODEL)
    return (y_prompt, y_sample) + tuple(jnp.stack(lst) for lst in outs)
```

```python
import functools

import jax
import jax.numpy as jnp
from jax import lax
from jax.experimental import pallas as pl
from jax.experimental.pallas import tpu as pltpu

D_MODEL = 4096
BATCH = 4
SEQ = 2048
DEPTH = 4
DEC_BATCH = 32
DEC_SEQ = 8
NORM_EPS = 1e-6
L2_EPS = 1e-6
RWKV_WIDTH = 1024
RWKV_HEAD_DIM = 64
RWKV_HEADS = 16
DECAY_LORA = 64
AAA_LORA = 64
GATE_LORA = 128
RWKV_PROJ = 3 * RWKV_WIDTH + DECAY_LORA + AAA_LORA + GATE_LORA
RWKV_LN_EPS = 64e-5
SWA_WIDTH = 1024
SWA_HEAD_DIM = 64
SWA_HEADS = 16
SWA_KV_HEADS = 2
SWA_GROUP = 8
SWA_KV_WIDTH = 128
SWA_PROJ = SWA_WIDTH + 2 * SWA_KV_WIDTH
WINDOW = 128
SWA_SCALE = SWA_HEAD_DIM ** -0.5
GDN_WIDTH = 2048
GDN_HEAD_DIM = 128
GDN_HEADS = 16
GDN_CONV_DIM = 3 * GDN_WIDTH
CONV_WIDTH = 4
GDN_CHUNK = 64
GDN_MAIN = GDN_CONV_DIM + GDN_WIDTH
D_FF = 11008

ROWS_P = BATCH * SEQ
ROWS_D = DEC_BATCH * DEC_SEQ
ROWS = ROWS_P + ROWS_D

LANES = 128
SUBLANES = 8
VMEM_LIMIT = 56 * 1024 * 1024

F32 = jnp.float32
BF16 = jnp.bfloat16


def _cparams(n_axes):
    return pltpu.CompilerParams(dimension_semantics=("arbitrary",) * n_axes, vmem_limit_bytes=VMEM_LIMIT)


def _split(x):
    hi = x.astype(BF16)
    lo = (x - hi.astype(F32)).astype(BF16)
    return hi, lo


def _dot(a, b):
    return jnp.dot(a, b, preferred_element_type=F32)


def _dot_lhs2(x, m_bf16):
    hi, lo = _split(x)
    return _dot(hi, m_bf16) + _dot(lo, m_bf16)


def _dot3(x, w):
    xh, xl = _split(x)
    wh, wl = _split(w)
    return _dot(xh, wh) + (_dot(xl, wh) + _dot(xh, wl))


def _segsum(x, j_bf16):
    n = x.shape[-1] // LANES
    return jnp.concatenate([_dot_lhs2(x[:, c * LANES:(c + 1) * LANES], j_bf16) for c in range(n)], axis=-1)


def _sigmoid(x):
    return 1.0 / (1.0 + jnp.exp(-x))


def _softplus(x):
    return jnp.maximum(x, 0.0) + jnp.log(1.0 + jnp.exp(-jnp.abs(x)))


def _rmsnorm_kernel(x_ref, w_ref, o_ref):
    x = x_ref[...]
    ms = jnp.mean(x * x, axis=-1, keepdims=True)
    o_ref[...] = (x * lax.rsqrt(ms + NORM_EPS) * w_ref[...]).astype(o_ref.dtype)


def rmsnorm(x, w, out_dtype, br=256):
    r, d = x.shape
    return pl.pallas_call(
        _rmsnorm_kernel,
        grid=(r // br,),
        in_specs=[pl.BlockSpec((br, d), lambda i: (i, 0)), pl.BlockSpec((1, d), lambda i: (0, 0))],
        out_specs=pl.BlockSpec((br, d), lambda i: (i, 0)),
        out_shape=jax.ShapeDtypeStruct((r, d), out_dtype),
        compiler_params=_cparams(1),
        name="rmsnorm",
    )(x, w.reshape(1, d))


def _rmsnorm_join_kernel(xp_ref, xs_ref, w_ref, o_ref, h_ref, *, n_p):
    x = jnp.where(pl.program_id(0) < n_p, xp_ref[...], xs_ref[...])
    h_ref[...] = x
    ms = jnp.mean(x * x, axis=-1, keepdims=True)
    o_ref[...] = (x * lax.rsqrt(ms + NORM_EPS) * w_ref[...]).astype(o_ref.dtype)


def rmsnorm_join(xp, xs, w, out_dtype):
    (rp, d), rs = xp.shape, xs.shape[0]
    br = rs
    assert rp % br == 0
    n_p = rp // br
    spec = pl.BlockSpec((br, d), lambda i: (i, 0))
    return pl.pallas_call(
        functools.partial(_rmsnorm_join_kernel, n_p=n_p),
        grid=(n_p + 1,),
        in_specs=[pl.BlockSpec((br, d), lambda i: (jnp.minimum(i, n_p - 1), 0)),
                  pl.BlockSpec((br, d), lambda i: (0, 0)), pl.BlockSpec((1, d), lambda i: (0, 0))],
        out_specs=[spec, spec],
        out_shape=[jax.ShapeDtypeStruct((rp + rs, d), out_dtype), jax.ShapeDtypeStruct((rp + rs, d), xp.dtype)],
        compiler_params=_cparams(1),
        name="rmsnorm_join",
    )(xp, xs, w.reshape(1, d))


def _mm_kernel(a_ref, b_ref, o_ref):
    o_ref[...] = _dot(a_ref[...], b_ref[...]).astype(o_ref.dtype)


def matmul(a, w, layer, bm, bn, col0, n_out, out_dtype=F32):
    r, k = a.shape
    assert col0 % bn == 0 and n_out % bn == 0
    return pl.pallas_call(
        _mm_kernel,
        grid=(r // bm, n_out // bn),
        in_specs=[pl.BlockSpec((bm, k), lambda i, j: (i, 0)),
                  pl.BlockSpec((None, k, bn), lambda i, j: (layer, 0, col0 // bn + j))],
        out_specs=pl.BlockSpec((bm, bn), lambda i, j: (i, j)),
        out_shape=jax.ShapeDtypeStruct((r, n_out), out_dtype),
        compiler_params=_cparams(2),
        name="proj_in",
    )(a, w)


def _swiglu_kernel(a_ref, g_ref, u_ref, o_ref):
    a = a_ref[...]
    g = _dot(a, g_ref[...])
    u = _dot(a, u_ref[...])
    o_ref[...] = (g * _sigmoid(g) * u).astype(o_ref.dtype)


def swiglu_matmul(a, wg, wu, layer, bm, bn):
    r, k = a.shape
    n = wg.shape[2]
    wspec = pl.BlockSpec((None, k, bn), lambda i, j: (layer, 0, j))
    return pl.pallas_call(
        _swiglu_kernel,
        grid=(r // bm, pl.cdiv(n, bn)),
        in_specs=[pl.BlockSpec((bm, k), lambda i, j: (i, 0)), wspec, wspec],
        out_specs=pl.BlockSpec((bm, bn), lambda i, j: (i, j)),
        out_shape=jax.ShapeDtypeStruct((r, n), BF16),
        compiler_params=_cparams(2),
        name="ffn_swiglu",
    )(a, wg, wu)


def _mm_res_kernel(*refs, n_a):
    a_refs, b_refs, res_ref, o_ref = refs[:n_a], refs[n_a:2 * n_a], refs[2 * n_a], refs[2 * n_a + 1]
    kk = pl.program_id(2)

    @pl.when(kk == 0)
    def _():
        o_ref[...] = res_ref[...]

    acc = _dot(a_refs[0][...], b_refs[0][...])
    for a_ref, b_ref in zip(a_refs[1:], b_refs[1:]):
        acc = acc + _dot(a_ref[...], b_ref[...])
    o_ref[...] += acc


def matmul_residual(a_list, w, layer, res, bm, bn, bk_list):
    r = res.shape[0]
    n = w.shape[2]
    nk = a_list[0].shape[1] // bk_list[0]
    in_specs, row0 = [], 0
    for a, bk in zip(a_list, bk_list):
        assert a.shape[1] == nk * bk
        in_specs.append(pl.BlockSpec((bm, bk), lambda i, j, k: (i, k)))
    for a, bk in zip(a_list, bk_list):
        assert row0 % bk == 0
        in_specs.append(pl.BlockSpec((None, bk, bn), functools.partial(
            lambda i, j, k, off: (layer, off + k, j), off=row0 // bk)))
        row0 += a.shape[1]
    in_specs.append(pl.BlockSpec((bm, bn), lambda i, j, k: (i, j)))
    return pl.pallas_call(
        functools.partial(_mm_res_kernel, n_a=len(a_list)),
        grid=(r // bm, n // bn, nk),
        in_specs=in_specs,
        out_specs=pl.BlockSpec((bm, bn), lambda i, j, k: (i, j)),
        out_shape=jax.ShapeDtypeStruct((r, n), F32),
        compiler_params=_cparams(3),
        name="proj_residual",
    )(*a_list, *([w] * len(a_list)), res)


def _shifted(prev8, x, k):
    ext = jnp.concatenate([prev8, x], axis=0)
    return pltpu.roll(ext, k, axis=0)[SUBLANES:]


def _seq_specs(width, col_block, tb, n_tb, row0):
    assert row0 % tb == 0 and tb % SUBLANES == 0
    cur = pl.BlockSpec((tb, width), lambda b, t: (row0 // tb + b * n_tb + t, col_block))
    prev = pl.BlockSpec((SUBLANES, width), lambda b, t: (
        jnp.maximum((row0 + (b * n_tb + t) * tb) // SUBLANES - 1, 0), col_block))
    state = pl.BlockSpec((SUBLANES, width), lambda b, t: (b, 0))
    return cur, prev, state


def _row_spec(width, tb, n_tb, row0=0, col_block=0):
    return pl.BlockSpec((tb, width), lambda b, t: (row0 // tb + b * n_tb + t, col_block))


def _const_spec(shape):
    return pl.BlockSpec(shape, lambda b, t: (0,) * len(shape))


def _pad_state_rows(st):
    b, k, c = st.shape
    return jnp.pad(st, ((0, 0), (SUBLANES - k, 0), (0, 0))).reshape(b * SUBLANES, c)


def _rwkv_prep_kernel(cur_ref, prev_ref, st_ref, mix_ref, w0_ref, a0_ref, wwa_ref, g2_ref, kkw_ref, kaw_ref, j_ref,
                      r_o, kk_o, d_o, b_o, k_o, v_o, g_o):
    x = cur_ref[...]
    prev8 = jnp.where(pl.program_id(1) == 0, st_ref[...], prev_ref[...])
    xs = x + (_shifted(prev8, x, 1) - x) * mix_ref[...]
    w3 = RWKV_WIDTH
    r, k, v = xs[:, 0:w3], xs[:, w3:2 * w3], xs[:, 2 * w3:3 * w3]
    wa = xs[:, 3 * w3:3 * w3 + LANES]
    glo = xs[:, 3 * w3 + LANES:3 * w3 + 2 * LANES]
    lane = lax.broadcasted_iota(jnp.int32, wa.shape, 1)
    lora = _dot3(jnp.where(lane < DECAY_LORA, jnp.tanh(wa), wa), wwa_ref[...])
    w = -_softplus(-(w0_ref[...] + lora[:, :w3])) - 0.5
    d = jnp.exp(-jnp.exp(w))
    a = _sigmoid(a0_ref[...] + lora[:, w3:])
    g = _dot3(_sigmoid(glo), g2_ref[...])
    kn = k * kkw_ref[...]
    kk = kn * lax.rsqrt(_segsum(kn * kn, j_ref[...]) + L2_EPS)
    r_o[...] = r
    kk_o[...] = kk
    d_o[...] = d
    b_o[...] = kk * a
    k_o[...] = k * (1.0 + (a - 1.0) * kaw_ref[...])
    v_o[...] = v
    g_o[...] = g


def rwkv_prep(p_rwkv, st8, prm, n_seq, t_len, tb, row0):
    n_tb = t_len // tb
    cur, prev, state = _seq_specs(RWKV_PROJ, 0, tb, n_tb, row0)
    out = jax.ShapeDtypeStruct((n_seq * t_len, RWKV_WIDTH), F32)
    ospec = _row_spec(RWKV_WIDTH, tb, n_tb)
    return pl.pallas_call(
        _rwkv_prep_kernel,
        grid=(n_seq, n_tb),
        in_specs=[cur, prev, state, _const_spec((1, RWKV_PROJ)), _const_spec((1, RWKV_WIDTH)),
                  _const_spec((1, RWKV_WIDTH)), _const_spec((LANES, 2 * RWKV_WIDTH)),
                  _const_spec((GATE_LORA, RWKV_WIDTH)), _const_spec((1, RWKV_WIDTH)), _const_spec((1, RWKV_WIDTH)),
                  _const_spec((LANES, LANES))],
        out_specs=[ospec] * 7,
        out_shape=[out] * 7,
        compiler_params=_cparams(2),
        name="rwkv_prep",
    )(p_rwkv, p_rwkv, st8, prm["mix"], prm["w0"], prm["a0"], prm["wwa"], prm["g2"], prm["k_k"], prm["k_a"],
      prm["j64"])


def _rwkv_post_kernel(y_ref, r_ref, k_ref, v_ref, g_ref, lnw_ref, lnb_ref, rk_ref, j_ref, o_ref):
    j = j_ref[...]
    y = y_ref[...]
    inv_n = 1.0 / RWKV_HEAD_DIM
    yc = y - _segsum(y, j) * inv_n
    var = _segsum(yc * yc, j) * inv_n
    out = yc * lax.rsqrt(var + RWKV_LN_EPS) * lnw_ref[...] + lnb_ref[...]
    out = out + _segsum(r_ref[...] * k_ref[...] * rk_ref[...], j) * v_ref[...]
    o_ref[...] = (out * g_ref[...]).astype(o_ref.dtype)


def rwkv_post(y, r, k, v, g, prm, tb):
    rows = y.shape[0]
    spec = pl.BlockSpec((tb, RWKV_WIDTH), lambda i: (i, 0))
    cspec = pl.BlockSpec((1, RWKV_WIDTH), lambda i: (0, 0))
    return pl.pallas_call(
        _rwkv_post_kernel,
        grid=(rows // tb,),
        in_specs=[spec] * 5 + [cspec] * 3 + [pl.BlockSpec((LANES, LANES), lambda i: (0, 0))],
        out_specs=spec,
        out_shape=jax.ShapeDtypeStruct((rows, RWKV_WIDTH), BF16),
        compiler_params=_cparams(1),
        name="rwkv_post",
    )(y, r, k, v, g, prm["ln_w"], prm["ln_b"], prm["r_k"], prm["j64"])


def _scan_kernel(kk_ref, d_ref, b_ref, k_ref, r_ref, v_ref, s0_ref, y_ref, s_ref, *scratch, tb, dj, di, lane_dup):
    ng = di // SUBLANES
    half = LANES // 2

    @pl.when(pl.program_id(1) == 0)
    def _():
        s_ref[...] = s0_ref[...]

    if lane_dup:
        key_refs, v_s, y_s = scratch[:5], scratch[5], scratch[6]
        for src, dst in zip((kk_ref, d_ref, b_ref, k_ref, r_ref), key_refs):
            x = src[0].reshape(tb * dj, half)
            dst[...] = jnp.concatenate([x, x], axis=1)
        vv = v_ref[0]
        v_s[...] = jnp.concatenate([vv[:, :di, :], vv[:, di:, :]], axis=2).reshape(tb * di, LANES)
        kk_s, d_s, b_s, k_s, r_s = key_refs

        def row(ref, t, j):
            return ref[pl.ds(t * dj + j, 1), :]

        def vrows(t, g):
            return pl.ds(pl.multiple_of(t * di + g * SUBLANES, SUBLANES), SUBLANES)

        load_v = lambda t, g: v_s[vrows(t, g), :]

        def store_y(t, g, val):
            y_s[vrows(t, g), :] = val
    else:
        kk_s, d_s, b_s, k_s, r_s = kk_ref, d_ref, b_ref, k_ref, r_ref

        def row(ref, t, j):
            return ref[0, t, pl.ds(j, 1), :]

        load_v = lambda t, g: v_ref[0, t, pl.ds(g * SUBLANES, SUBLANES), :]

        def store_y(t, g, val):
            y_ref[0, t, pl.ds(g * SUBLANES, SUBLANES), :] = val

    def sl(g):
        return pl.ds(g * SUBLANES, SUBLANES)

    zero = tuple(jnp.zeros((SUBLANES, LANES), F32) for _ in range(ng))

    def first_dot(j, acc):
        kkj = row(kk_s, 0, j)
        return tuple(acc[g] + s_ref[0, j, sl(g), :] * kkj for g in range(ng))

    def step(t, s_kk):
        sa = [-a for a in s_kk]
        v = [load_v(t, g) for g in range(ng)]
        t_next = jnp.minimum(t + 1, tb - 1)

        def update(j, carry):
            yacc, nacc = carry
            dj_, bj, kj, rj = row(d_s, t, j), row(b_s, t, j), row(k_s, t, j), row(r_s, t, j)
            kkn = row(kk_s, t_next, j)
            y_out, n_out = [], []
            for g in range(ng):
                s = s_ref[0, j, sl(g), :] * dj_ + sa[g] * bj + v[g] * kj
                s_ref[0, j, sl(g), :] = s
                y_out.append(yacc[g] + s * rj)
                n_out.append(nacc[g] + s * kkn)
            return tuple(y_out), tuple(n_out)

        yacc, nacc = lax.fori_loop(0, dj, update, (zero, zero), unroll=8)
        for g in range(ng):
            store_y(t, g, yacc[g])
        return nacc

    lax.fori_loop(0, tb, step, lax.fori_loop(0, dj, first_dot, zero, unroll=8))
    if lane_dup:
        ys = y_s[...].reshape(tb, di, LANES)
        y_ref[0] = jnp.concatenate([ys[:, :, :half], ys[:, :, half:]], axis=1)


def dplr_scan(kk, d, b, k, r, v, s0, tb):
    g_n, t_len, dj, width = kk.shape
    di = s0.shape[2]
    lane_dup = width != LANES
    assert v.shape[2] == (2 * di if lane_dup else di) and width == (LANES // 2 if lane_dup else LANES)
    jspec = pl.BlockSpec((1, tb, dj, width), lambda g, t: (g, t, 0, 0))
    ispec = pl.BlockSpec((1, tb, v.shape[2], width), lambda g, t: (g, t, 0, 0))
    sspec = pl.BlockSpec((1, dj, di, LANES), lambda g, t: (g, 0, 0, 0))
    scratch = []
    if lane_dup:
        scratch = [pltpu.VMEM((tb * dj, LANES), F32)] * 5 + [pltpu.VMEM((tb * di, LANES), F32)] * 2
    return pl.pallas_call(
        functools.partial(_scan_kernel, tb=tb, dj=dj, di=di, lane_dup=lane_dup),
        grid=(g_n, t_len // tb),
        in_specs=[jspec] * 5 + [ispec, sspec],
        out_specs=[ispec, sspec],
        out_shape=[jax.ShapeDtypeStruct(v.shape, F32), jax.ShapeDtypeStruct(s0.shape, F32)],
        scratch_shapes=scratch,
        compiler_params=_cparams(2),
        name="dplr_scan",
    )(kk, d, b, k, r, v, s0)


def _to_key_layout(x, n_seq, t_len, heads, dj, dup):
    x = jnp.broadcast_to(x.reshape(1, n_seq, t_len, heads, dj), (dup, n_seq, t_len, heads, dj))
    x = x.transpose(2, 4, 0, 1, 3)
    g_n = dup * n_seq * heads // LANES
    return x.reshape(t_len, dj, g_n, LANES).transpose(2, 0, 1, 3)


def _to_value_layout(x, n_seq, t_len, heads, dv, dup):
    di = dv // dup
    x = x.reshape(n_seq, t_len, heads, dup, di).transpose(1, 4, 3, 0, 2)
    g_n = dup * n_seq * heads // LANES
    return x.reshape(t_len, di, g_n, LANES).transpose(2, 0, 1, 3)


def _from_value_layout(y, n_seq, t_len, heads, dv, dup):
    g_n, _, di, _ = y.shape
    y = y.transpose(1, 2, 0, 3).reshape(t_len, di, dup, n_seq, heads)
    return y.transpose(3, 0, 4, 2, 1).reshape(n_seq * t_len, heads * dv)


def _state_to_layout(s, dup, value_major):
    bsz, heads = s.shape[:2]
    if value_major:
        dv, dj = s.shape[2:]
        s = s.reshape(bsz, heads, dup, dv // dup, dj).transpose(4, 3, 2, 0, 1)
    else:
        dj, dv = s.shape[2:]
        s = s.reshape(bsz, heads, dj, dup, dv // dup).transpose(2, 4, 3, 0, 1)
    g_n = dup * bsz * heads // LANES
    return s.reshape(dj, dv // dup, g_n, LANES).transpose(2, 0, 1, 3)


def _state_from_layout(s, bsz, heads, dup, value_major):
    g_n, dj, di, _ = s.shape
    s = s.transpose(1, 2, 0, 3).reshape(dj, di, dup, bsz, heads)
    if value_major:
        return s.transpose(3, 4, 2, 1, 0).reshape(bsz, heads, dup * di, dj)
    return s.transpose(3, 4, 0, 2, 1).reshape(bsz, heads, dj, dup * di)


def _swa_kernel(sink_ref, q_ref, kc_ref, vc_ref, kp_ref, vp_ref, o_ref, *, tq, has_cache):
    s_len = WINDOW + tq
    kfull = jnp.concatenate([kp_ref[...], kc_ref[...]], axis=0)
    vfull = jnp.concatenate([vp_ref[...], vc_ref[...]], axis=0)
    lane = lax.broadcasted_iota(jnp.int32, kfull.shape, 1)
    low = lane < SWA_HEAD_DIM
    kswap = pltpu.roll(kfull, SWA_HEAD_DIM, axis=1)
    vswap = pltpu.roll(vfull, SWA_HEAD_DIM, axis=1)
    k_lo = [jnp.where(low, kfull, 0.0), jnp.where(low, kswap, 0.0)]
    k_hi = [jnp.where(low, 0.0, kswap), jnp.where(low, 0.0, kfull)]
    v_lo = [jnp.where(low, vfull, 0.0), jnp.where(low, vswap, 0.0)]
    v_hi = [jnp.where(low, 0.0, vswap), jnp.where(low, 0.0, vfull)]
    t_idx = lax.broadcasted_iota(jnp.int32, (tq, s_len), 0)
    s_idx = lax.broadcasted_iota(jnp.int32, (tq, s_len), 1)
    delta = WINDOW + t_idx - s_idx
    valid = (delta >= 0) & (delta <= WINDOW)
    if not has_cache:
        valid = valid & ((s_idx >= WINDOW) | (pl.program_id(1) > 0))
    contract_last = (((1,), (1,)), ((), ()))
    for hp in range(SWA_HEADS // 2):
        qp = q_ref[:, hp * LANES:(hp + 1) * LANES]
        kv = (2 * hp) // SWA_GROUP
        acc = None
        for half, (kmat, vmat) in enumerate(((k_lo[kv], v_lo[kv]), (k_hi[kv], v_hi[kv]))):
            sink = sink_ref[2 * hp + half]
            s = lax.dot_general(qp, kmat, contract_last, preferred_element_type=F32) * SWA_SCALE
            s = jnp.where(valid, s, -jnp.inf)
            m = jnp.maximum(jnp.max(s, axis=-1, keepdims=True), sink)
            e = jnp.exp(s - m)
            den = jnp.sum(e, axis=-1, keepdims=True) + jnp.exp(sink - m)
            o = _dot(e, vmat) / den
            acc = o if acc is None else acc + o
        o_ref[:, hp * LANES:(hp + 1) * LANES] = acc.astype(o_ref.dtype)


def swa_attention(sinks, p_swa, n_seq, t_len, tq, row0, cache_k=None, cache_v=None):
    n_blk = t_len // tq
    has_cache = cache_k is not None
    kcol, vcol = SWA_WIDTH // LANES, SWA_WIDTH // LANES + 1
    q_spec = _row_spec(SWA_WIDTH, tq, n_blk, row0)
    kc_spec = _row_spec(LANES, tq, n_blk, row0, kcol)
    vc_spec = _row_spec(LANES, tq, n_blk, row0, vcol)
    if has_cache:
        kp_spec = vp_spec = pl.BlockSpec((WINDOW, LANES), lambda b, t: (b, 0))
        kp_arr, vp_arr = cache_k, cache_v
    else:
        assert tq == WINDOW and row0 == 0

        def prev_rows(b, t):
            return jnp.maximum(b * n_blk + t - 1, 0)

        kp_spec = pl.BlockSpec((WINDOW, LANES), lambda b, t: (prev_rows(b, t), kcol))
        vp_spec = pl.BlockSpec((WINDOW, LANES), lambda b, t: (prev_rows(b, t), vcol))
        kp_arr = vp_arr = p_swa
    return pl.pallas_call(
        functools.partial(_swa_kernel, tq=tq, has_cache=has_cache),
        grid=(n_seq, n_blk),
        in_specs=[pl.BlockSpec(memory_space=pltpu.SMEM), q_spec, kc_spec, vc_spec, kp_spec, vp_spec],
        out_specs=_row_spec(SWA_WIDTH, tq, n_blk),
        out_shape=jax.ShapeDtypeStruct((n_seq * t_len, SWA_WIDTH), BF16),
        compiler_params=_cparams(2),
        name="swa",
    )(sinks, p_swa, p_swa, p_swa, kp_arr, vp_arr)


def _gdn_prep_kernel(cur_ref, prev_ref, st_ref, ba_ref, cw_ref, alog_ref, dt_ref, eb_ref, eg_ref, j_ref,
                     kn_o, d_o, b_o, v_o, q_o, *, chunked):
    x = cur_ref[...]
    prev8 = jnp.where(pl.program_id(1) == 0, st_ref[...], prev_ref[...])
    y = x * cw_ref[3:4, :]
    for k in range(1, CONV_WIDTH):
        y = y + _shifted(prev8, x, k) * cw_ref[3 - k:4 - k, :]
    act = y * _sigmoid(y)
    w2 = GDN_WIDTH
    q, k, v = act[:, :w2], act[:, w2:2 * w2], act[:, 2 * w2:]
    j = j_ref[...]
    qn = q * lax.rsqrt(_segsum(q * q, j) + L2_EPS) * (GDN_HEAD_DIM ** -0.5)
    kn = k * lax.rsqrt(_segsum(k * k, j) + L2_EPS)
    ba = ba_ref[...]
    ba = jnp.where(lax.broadcasted_iota(jnp.int32, ba.shape, 1) < 2 * GDN_HEADS, ba, 0.0)
    beta = _dot_lhs2(_sigmoid(ba), eb_ref[...])
    gate = _dot_lhs2(-jnp.exp(alog_ref[...]) * _softplus(ba + dt_ref[...]), eg_ref[...])
    kn_o[...] = kn
    q_o[...] = qn
    if chunked:
        d_o[...] = gate
        b_o[...] = beta
        v_o[...] = v
    else:
        eg = jnp.exp(gate)
        d_o[...] = eg
        b_o[...] = eg * beta * kn
        v_o[...] = beta * v


def gdn_prep(p_gdn, p_swa, st8, prm, n_seq, t_len, tb, row0, chunked):
    n_tb = t_len // tb
    cur, prev, state = _seq_specs(GDN_CONV_DIM, 0, tb, n_tb, row0)
    ba_spec = _row_spec(LANES, tb, n_tb, row0)
    out = jax.ShapeDtypeStruct((n_seq * t_len, GDN_WIDTH), F32)
    ospec = _row_spec(GDN_WIDTH, tb, n_tb)
    return pl.pallas_call(
        functools.partial(_gdn_prep_kernel, chunked=chunked),
        grid=(n_seq, n_tb),
        in_specs=[cur, prev, state, ba_spec, _const_spec((SUBLANES, GDN_CONV_DIM)), _const_spec((1, LANES)),
                  _const_spec((1, LANES)), _const_spec((LANES, GDN_WIDTH)),
                  _const_spec((LANES, GDN_WIDTH)), _const_spec((LANES, LANES))],
        out_specs=[ospec] * 5,
        out_shape=[out] * 5,
        compiler_params=_cparams(2),
        name="gdn_prep",
    )(p_gdn, p_gdn, st8, p_swa, prm["conv_w"], prm["a_log"], prm["dt"], prm["e_beta"], prm["e_gate"], prm["j128"])


def _gdn_post_kernel(o_ref_in, z_ref, w_ref, j_ref, o_ref):
    o = o_ref_in[...]
    z = z_ref[...]
    ms = _segsum(o * o, j_ref[...]) * (1.0 / GDN_HEAD_DIM)
    y = o * lax.rsqrt(ms + NORM_EPS) * w_ref[...]
    o_ref[...] = (y * (z * _sigmoid(z))).astype(o_ref.dtype)


def gdn_post(o, p_gdn, prm, tb, row0):
    rows = o.shape[0]
    spec = pl.BlockSpec((tb, GDN_WIDTH), lambda i: (i, 0))
    zspec = pl.BlockSpec((tb, GDN_WIDTH), lambda i: (row0 // tb + i, GDN_CONV_DIM // GDN_WIDTH))
    return pl.pallas_call(
        _gdn_post_kernel,
        grid=(rows // tb,),
        in_specs=[spec, zspec, pl.BlockSpec((1, GDN_WIDTH), lambda i: (0, 0)),
                  pl.BlockSpec((LANES, LANES), lambda i: (0, 0))],
        out_specs=spec,
        out_shape=jax.ShapeDtypeStruct((rows, GDN_WIDTH), BF16),
        compiler_params=_cparams(1),
        name="gdn_post",
    )(o, p_gdn, prm["norm_w"], prm["j128"])


def _split3(x):
    hi = x.astype(BF16)
    r = x - hi.astype(F32)
    mid = r.astype(BF16)
    return hi, mid, (r - mid.astype(F32)).astype(BF16)


def _dotb(a, b):
    return _dot(a.astype(BF16), b.astype(BF16))


def _dot3_packed(x, w, c):
    xh = x.astype(BF16).astype(F32)
    hi_lo = xh + pltpu.roll(x - xh, c, axis=1)
    wh, wl = _split(w)
    lhs = jnp.concatenate([hi_lo, xh], axis=1).astype(BF16)
    rhs = jnp.concatenate([wh, wh, wl, jnp.zeros_like(wh)], axis=0)
    return _dot(lhs, rhs)


def _pad_rows(x, rows):
    return jnp.concatenate([x, jnp.zeros((rows - x.shape[0], x.shape[1]), x.dtype)], axis=0)


def _gdn_chunk_kernel(q_ref, k_ref, v_ref, beta_ref, g_ref, s0_ref, o_ref, s_ref, *, hb, nc):
    c = GDN_CHUNK

    @pl.when(pl.program_id(2) == 0)
    def _():
        s_ref[...] = s0_ref[...]

    row = lax.broadcasted_iota(jnp.int32, (c, LANES), 0)
    col = lax.broadcasted_iota(jnp.int32, (c, LANES), 1)
    incl, strict = row >= col, row > col
    eye = (row == col).astype(F32)
    row3 = lax.broadcasted_iota(jnp.int32, (c, 2 * LANES), 0)
    col3 = lax.broadcasted_iota(jnp.int32, (c, 2 * LANES), 1)
    tri3 = ((row3 >= col3 % c) & (col3 < 3 * c)).astype(BF16)
    lane0_3 = (lax.broadcasted_iota(jnp.int32, (c, 3 * LANES), 1) % LANES == 0).astype(BF16)
    zero_c = jnp.zeros((c, LANES), BF16)
    contract_last = (((1,), (1,)), ((), ()))
    contract_first = (((0,), (0,)), ((), ()))

    units = [(h, ci) for ci in range(nc) for h in range(hb)]
    n_u = len(units)

    def load(ref):
        return [ref[ci * c:(ci + 1) * c, h * LANES:(h + 1) * LANES] for h, ci in units]

    q, k, v, beta, g = load(q_ref), load(k_ref), load(v_ref), load(beta_ref), load(g_ref)
    lanes = lambda x, u: x[:, u * LANES:(u + 1) * LANES]
    g_parts = jnp.concatenate([jnp.concatenate(list(_split3(x)) + [zero_c], axis=0) for x in g], axis=1)
    gc_all = _dot(tri3, g_parts)
    gc = [lanes(gc_all, u) for u in range(n_u)]
    gc_parts = jnp.concatenate([_pad_rows(jnp.concatenate(_split3(x), axis=1), LANES) for x in gc], axis=0)
    gc_row_all = lax.dot_general(lane0_3, gc_parts, contract_last, preferred_element_type=F32)
    dec = [jnp.where(incl, jnp.exp(jnp.where(incl, gc[u] - lanes(gc_row_all, u), 0.0)), 0.0) for u in range(n_u)]
    kb = [k[u] * beta[u] for u in range(n_u)]
    gram = [lax.dot_general(jnp.concatenate([kb[u], q[u]], axis=0).astype(BF16), _pad_rows(k[u].astype(BF16), LANES),
                            contract_last, preferred_element_type=F32) for u in range(n_u)]
    a = [jnp.where(strict, gram[u][:c] * dec[u], 0.0) for u in range(n_u)]
    attn = [jnp.where(incl, gram[u][c:] * dec[u], 0.0)[:, :c].astype(BF16) for u in range(n_u)]
    tinv = [eye - x for x in a]
    p = [_dot3_packed(x, x, c) for x in a]
    n_levels = c.bit_length() - 2
    for lvl in range(n_levels):
        last = lvl + 1 == n_levels
        lhs = tinv if last else [jnp.concatenate([tinv[u], p[u]], axis=0) for u in range(n_u)]
        prod = [_dot3_packed(lhs[u], p[u], c) for u in range(n_u)]
        tinv = [tinv[u] + prod[u][:c] for u in range(n_u)]
        if not last:
            p = [prod[u][c:] for u in range(n_u)]
    eg = [jnp.exp(x) for x in gc]
    uw = [_dotb(tinv[u][:, :c], jnp.concatenate([v[u] * beta[u], kb[u] * eg[u]], axis=1)) for u in range(n_u)]
    w_qe = [jnp.concatenate([uw[u][:, LANES:], q[u] * eg[u]], axis=0).astype(BF16) for u in range(n_u)]
    g_last = [x[c - 1:c, :] for x in gc]
    kd = [(k[u] * jnp.exp(g_last[u] - gc[u])).astype(BF16) for u in range(n_u)]
    eg_last = [jnp.exp(x) for x in g_last]

    s = [s_ref[0, h] for h in range(hb)]
    o_rows = []
    for ci in range(nc):
        us = [ci * hb + h for h in range(hb)]
        wq = [_dot(w_qe[u], s[h].astype(BF16)) for h, u in enumerate(us)]
        v_new = [(uw[u][:, :LANES] - wq[h][:c]).astype(BF16) for h, u in enumerate(us)]
        o_rows.append(jnp.concatenate([wq[h][c:] + _dot(attn[u], v_new[h]) for h, u in enumerate(us)], axis=1))
        s = [s[h] * eg_last[u] + lax.dot_general(kd[u], v_new[h], contract_first, preferred_element_type=F32)
             for h, u in enumerate(us)]
    o_ref[...] = jnp.concatenate(o_rows, axis=0)
    s_ref[0] = jnp.stack(s)


def gdn_chunked(q, k, v, beta, g, s0, n_seq, t_len, hb, nc):
    n_ch = t_len // (GDN_CHUNK * nc)
    spec = pl.BlockSpec((GDN_CHUNK * nc, hb * LANES), lambda b, h, c: (b * n_ch + c, h))
    sspec = pl.BlockSpec((1, hb, GDN_HEAD_DIM, GDN_HEAD_DIM), lambda b, h, c: (b, h, 0, 0))
    return pl.pallas_call(
        functools.partial(_gdn_chunk_kernel, hb=hb, nc=nc),
        grid=(n_seq, GDN_HEADS // hb, n_ch),
        in_specs=[spec] * 5 + [sspec],
        out_specs=[spec, sspec],
        out_shape=[jax.ShapeDtypeStruct(q.shape, F32), jax.ShapeDtypeStruct(s0.shape, F32)],
        compiler_params=_cparams(3),
        name="gdn_chunk",
    )(q, k, v, beta, g, s0)


def rwkv_mix(p_rwkv, shift_state, wkv_state, prm, n_seq, t_len, row0, tb_prep, tb_scan):
    st8 = _pad_state_rows(shift_state[:, None, :])
    r, kk, d, b, k, v, g = rwkv_prep(p_rwkv, st8, prm, n_seq, t_len, tb_prep, row0)
    dup = max(1, LANES // (n_seq * RWKV_HEADS))
    s0 = _state_to_layout(wkv_state, dup, value_major=True)
    if dup == 2:
        n = n_seq * RWKV_HEADS
        tl = lambda x: x.reshape(n_seq, t_len, RWKV_HEADS, RWKV_HEAD_DIM).transpose(1, 3, 0, 2).reshape(
            1, t_len, RWKV_HEAD_DIM, n)
        y, s1 = dplr_scan(tl(kk), tl(d), tl(b), tl(k), tl(r), tl(v), s0, tb_scan)
        y = y.reshape(t_len, RWKV_HEAD_DIM, n_seq, RWKV_HEADS).transpose(2, 0, 3, 1).reshape(
            n_seq * t_len, RWKV_WIDTH)
    else:
        kl = functools.partial(_to_key_layout, n_seq=n_seq, t_len=t_len, heads=RWKV_HEADS, dj=RWKV_HEAD_DIM, dup=dup)
        vt = _to_value_layout(v, n_seq, t_len, RWKV_HEADS, RWKV_HEAD_DIM, dup)
        y, s1 = dplr_scan(kl(kk), kl(d), kl(b), kl(k), kl(r), vt, s0, tb_scan)
        y = _from_value_layout(y, n_seq, t_len, RWKV_HEADS, RWKV_HEAD_DIM, dup)
    out = rwkv_post(y, r, k, v, g, prm, tb_prep)
    return out, _state_from_layout(s1, n_seq, RWKV_HEADS, dup, value_major=True)


def gdn_mix(p_gdn, p_swa, conv_state, ssm_state, prm, n_seq, t_len, row0, tb_prep, tb_scan):
    st8 = _pad_state_rows(conv_state)
    if t_len % GDN_CHUNK == 0:
        kn, g, beta, v, q = gdn_prep(p_gdn, p_swa, st8, prm, n_seq, t_len, tb_prep, row0, chunked=True)
        nc = 4 if t_len % (4 * GDN_CHUNK) == 0 else 1
        o, s1 = gdn_chunked(q, kn, v, beta, g, ssm_state, n_seq, t_len, hb=4, nc=nc)
        return gdn_post(o, p_gdn, prm, tb_prep, row0), s1
    kn, d, b, v, q = gdn_prep(p_gdn, p_swa, st8, prm, n_seq, t_len, tb_prep, row0, chunked=False)
    dup = 2 * max(1, LANES // (2 * n_seq * GDN_HEADS))
    kl = functools.partial(_to_key_layout, n_seq=n_seq, t_len=t_len, heads=GDN_HEADS, dj=GDN_HEAD_DIM, dup=dup)
    vt = _to_value_layout(v, n_seq, t_len, GDN_HEADS, GDN_HEAD_DIM, dup)
    s0 = _state_to_layout(ssm_state, dup, value_major=False)
    knl = kl(kn)
    o, s1 = dplr_scan(knl, kl(d), kl(b), knl, kl(q), vt, s0, tb_scan)
    o = _from_value_layout(o, n_seq, t_len, GDN_HEADS, GDN_HEAD_DIM, dup)
    out = gdn_post(o, p_gdn, prm, tb_prep, row0)
    return out, _state_from_layout(s1, n_seq, GDN_HEADS, dup, value_major=False)


def _block_ones(seg):
    i = jnp.arange(LANES)
    return (i[:, None] // seg == i[None, :] // seg).astype(BF16)


def _head_expander(first_row):
    rows = jnp.arange(LANES)[:, None]
    cols = jnp.arange(GDN_WIDTH)[None, :] // GDN_HEAD_DIM
    return (rows == cols + first_row).astype(BF16)


def _layer_params(l, W):
    row = lambda x: x.reshape(1, -1)
    zeros = jnp.zeros((DECAY_LORA, RWKV_WIDTH), F32)
    wwa = jnp.concatenate([jnp.concatenate([W["rwkv_w2"][l], zeros], axis=1),
                           jnp.concatenate([zeros, W["rwkv_a2"][l]], axis=1)], axis=0)
    lane_row = lambda x, off: jnp.pad(x, (off, LANES - off - x.shape[0])).reshape(1, LANES)
    rwkv = dict(mix=row(W["rwkv_shift_mix"][l]), w0=row(W["rwkv_w0"][l]), a0=row(W["rwkv_a0"][l]), wwa=wwa,
                g2=W["rwkv_g2"][l], k_k=row(W["rwkv_k_k"][l]), k_a=row(W["rwkv_k_a"][l]),
                r_k=row(W["rwkv_r_k"][l]), ln_w=row(W["rwkv_ln_w"][l]), ln_b=row(W["rwkv_ln_b"][l]),
                j64=_block_ones(RWKV_HEAD_DIM))
    gdn = dict(conv_w=jnp.pad(W["gdn_conv_w"][l], ((0, SUBLANES - CONV_WIDTH), (0, 0))),
               a_log=lane_row(W["gdn_A_log"][l], GDN_HEADS), dt=lane_row(W["gdn_dt_bias"][l], GDN_HEADS),
               e_beta=_head_expander(0), e_gate=_head_expander(GDN_HEADS),
               norm_w=jnp.tile(W["gdn_norm_w"][l], GDN_HEADS).reshape(1, GDN_WIDTH), j128=_block_ones(LANES))
    return rwkv, gdn


def kernel(x_prompt, x_sample, state_rwkv_shift, state_rwkv_wkv, cache_swa_k, cache_swa_v, state_gdn_conv,
           state_gdn_ssm, norm_mix, w_in, rwkv_shift_mix, rwkv_w0, rwkv_w2, rwkv_a0, rwkv_a2, rwkv_g2, rwkv_k_k,
           rwkv_k_a, rwkv_r_k, rwkv_ln_w, rwkv_ln_b, swa_sinks, gdn_conv_w, gdn_A_log, gdn_dt_bias, gdn_norm_w,
           w_out, norm_ffn, w_gate, w_up, w_down, final_norm):
    W = dict(rwkv_shift_mix=rwkv_shift_mix, rwkv_w0=rwkv_w0, rwkv_w2=rwkv_w2, rwkv_a0=rwkv_a0, rwkv_a2=rwkv_a2,
             rwkv_g2=rwkv_g2, rwkv_k_k=rwkv_k_k, rwkv_k_a=rwkv_k_a, rwkv_r_k=rwkv_r_k, rwkv_ln_w=rwkv_ln_w,
             rwkv_ln_b=rwkv_ln_b, gdn_conv_w=gdn_conv_w, gdn_A_log=gdn_A_log, gdn_dt_bias=gdn_dt_bias,
             gdn_norm_w=gdn_norm_w)
    w_in_b, w_out_b, w_gate_b, w_up_b, w_down_b = (w.astype(BF16) for w in (w_in, w_out, w_gate, w_up, w_down))
    swa_col, gdn_col, ba_col = RWKV_PROJ, RWKV_PROJ + SWA_PROJ, RWKV_PROJ + SWA_PROJ + GDN_MAIN

    u, h = rmsnorm_join(x_prompt.reshape(ROWS_P, D_MODEL), x_sample.reshape(ROWS_D, D_MODEL), norm_mix[0], BF16)
    zero_shift = jnp.zeros((BATCH, RWKV_PROJ), F32)
    zero_wkv = jnp.zeros((BATCH, RWKV_HEADS, RWKV_HEAD_DIM, RWKV_HEAD_DIM), F32)
    zero_conv = jnp.zeros((BATCH, CONV_WIDTH - 1, GDN_CONV_DIM), F32)
    zero_ssm = jnp.zeros((BATCH, GDN_HEADS, GDN_HEAD_DIM, GDN_HEAD_DIM), F32)
    wb = cache_swa_k.shape[2]
    assert wb == WINDOW
    outs = [[] for _ in range(12)]
    bm = 1056
    for l in range(DEPTH):
        rw, gd = _layer_params(l, W)
        if l > 0:
            u = rmsnorm(h, norm_mix[l], BF16)
        p_rwkv = matmul(u, w_in_b, l, bm, 256, 0, RWKV_PROJ)
        p_swa = matmul(u, w_in_b, l, bm, 256, swa_col, SWA_PROJ)
        p_gdn = matmul(u, w_in_b, l, bm, 512, gdn_col, GDN_MAIN)
        p_ba = matmul(u, w_in_b, l, bm, LANES, ba_col, LANES)

        o_rp, wkv_p = rwkv_mix(p_rwkv, zero_shift, zero_wkv, rw, BATCH, SEQ, 0, 128, 32)
        o_rd, wkv_d = rwkv_mix(p_rwkv, state_rwkv_shift[l], state_rwkv_wkv[l], rw, DEC_BATCH, DEC_SEQ, ROWS_P,
                               DEC_SEQ, DEC_SEQ)
        o_sp = swa_attention(swa_sinks[l], p_swa, BATCH, SEQ, WINDOW, 0)
        o_sd = swa_attention(swa_sinks[l], p_swa, DEC_BATCH, DEC_SEQ, DEC_SEQ, ROWS_P,
                             cache_swa_k[l].reshape(DEC_BATCH * wb, SWA_KV_WIDTH),
                             cache_swa_v[l].reshape(DEC_BATCH * wb, SWA_KV_WIDTH))
        o_gp, ssm_p = gdn_mix(p_gdn, p_ba, zero_conv, zero_ssm, gd, BATCH, SEQ, 0, 64, 32)
        o_gd, ssm_d = gdn_mix(p_gdn, p_ba, state_gdn_conv[l], state_gdn_ssm[l], gd, DEC_BATCH, DEC_SEQ, ROWS_P,
                              DEC_SEQ, DEC_SEQ)
        o_r = jnp.concatenate([o_rp, o_rd], axis=0)
        o_s = jnp.concatenate([o_sp, o_sd], axis=0)
        o_g = jnp.concatenate([o_gp, o_gd], axis=0)
        h = matmul_residual([o_r, o_s, o_g], w_out_b, l, h, bm, 512, [RWKV_WIDTH, SWA_WIDTH, GDN_WIDTH])
        u = rmsnorm(h, norm_ffn[l], BF16)
        h1 = swiglu_matmul(u, w_gate_b, w_up_b, l, bm, 512)
        h = matmul_residual([h1], w_down_b, l, h, bm, 512, [D_FF // 2])

        def tail_p(p, n_rows, c0, c1):
            return jnp.stack([lax.slice(p, ((b + 1) * SEQ - n_rows, c0), ((b + 1) * SEQ, c1)) for b in range(BATCH)])

        def tail_d(p, n_rows, c0, c1):
            x = lax.slice(p, (ROWS_P, c0), (ROWS, c1)).reshape(DEC_BATCH, DEC_SEQ, c1 - c0)
            return x[:, DEC_SEQ - n_rows:]

        kv = lambda x: x.reshape(x.shape[0], x.shape[1], SWA_KV_HEADS, SWA_HEAD_DIM)
        k0, k1, v1 = SWA_WIDTH, SWA_WIDTH + SWA_KV_WIDTH, SWA_PROJ
        layer_out = (
            tail_p(p_rwkv, 1, 0, RWKV_PROJ)[:, 0], wkv_p, kv(tail_p(p_swa, wb, k0, k1)), kv(tail_p(p_swa, wb, k1, v1)),
            tail_p(p_gdn, CONV_WIDTH - 1, 0, GDN_CONV_DIM), ssm_p,
            tail_d(p_rwkv, 1, 0, RWKV_PROJ)[:, 0], wkv_d,
            jnp.concatenate([cache_swa_k[l], kv(tail_d(p_swa, DEC_SEQ, k0, k1))], axis=1)[:, -wb:],
            jnp.concatenate([cache_swa_v[l], kv(tail_d(p_swa, DEC_SEQ, k1, v1))], axis=1)[:, -wb:],
            tail_d(p_gdn, CONV_WIDTH - 1, 0, GDN_CONV_DIM), ssm_d)
        for lst, t in zip(outs, layer_out):
            lst.append(t)
    y = rmsnorm(h, final_norm, F32)
    y_prompt = y[:ROWS_P].reshape(BATCH, SEQ, D_MODEL)
    y_sample = y[ROWS_P:].reshape(DEC_BATCH, DEC_SEQ, D_MODEL)
    return (y_prompt, y_sample) + tuple(jnp.stack(lst) for lst in outs)
```

```python
import functools

import jax
import jax.numpy as jnp
from jax import lax
from jax.experimental import pallas as pl
from jax.experimental.pallas import tpu as pltpu

D_MODEL = 4096
BATCH = 4
SEQ = 2048
DEPTH = 4
DEC_BATCH = 32
DEC_SEQ = 8
NORM_EPS = 1e-6
L2_EPS = 1e-6
RWKV_WIDTH = 1024
RWKV_HEAD_DIM = 64
RWKV_HEADS = 16
DECAY_LORA = 64
AAA_LORA = 64
GATE_LORA = 128
RWKV_PROJ = 3 * RWKV_WIDTH + DECAY_LORA + AAA_LORA + GATE_LORA
RWKV_LN_EPS = 64e-5
SWA_WIDTH = 1024
SWA_HEAD_DIM = 64
SWA_HEADS = 16
SWA_KV_HEADS = 2
SWA_GROUP = 8
SWA_KV_WIDTH = 128
SWA_PROJ = SWA_WIDTH + 2 * SWA_KV_WIDTH
WINDOW = 128
SWA_SCALE = SWA_HEAD_DIM ** -0.5
GDN_WIDTH = 2048
GDN_HEAD_DIM = 128
GDN_HEADS = 16
GDN_CONV_DIM = 3 * GDN_WIDTH
CONV_WIDTH = 4
GDN_CHUNK = 64
GDN_MAIN = GDN_CONV_DIM + GDN_WIDTH
D_FF = 11008

ROWS_P = BATCH * SEQ
ROWS_D = DEC_BATCH * DEC_SEQ
ROWS = ROWS_P + ROWS_D

LANES = 128
SUBLANES = 8
VMEM_LIMIT = 56 * 1024 * 1024

F32 = jnp.float32
BF16 = jnp.bfloat16


def _cparams(n_axes):
    return pltpu.CompilerParams(dimension_semantics=("arbitrary",) * n_axes, vmem_limit_bytes=VMEM_LIMIT)


def _split(x):
    hi = x.astype(BF16)
    lo = (x - hi.astype(F32)).astype(BF16)
    return hi, lo


def _dot(a, b):
    return jnp.dot(a, b, preferred_element_type=F32)


def _dot_lhs2(x, m_bf16):
    hi, lo = _split(x)
    return _dot(hi, m_bf16) + _dot(lo, m_bf16)


def _dot3(x, w):
    xh, xl = _split(x)
    wh, wl = _split(w)
    return _dot(xh, wh) + (_dot(xl, wh) + _dot(xh, wl))


def _segsum(x, j_bf16):
    n = x.shape[-1] // LANES
    return jnp.concatenate([_dot_lhs2(x[:, c * LANES:(c + 1) * LANES], j_bf16) for c in range(n)], axis=-1)


def _sigmoid(x):
    return 1.0 / (1.0 + jnp.exp(-x))


def _softplus(x):
    return jnp.maximum(x, 0.0) + jnp.log(1.0 + jnp.exp(-jnp.abs(x)))


def _rmsnorm_kernel(x_ref, w_ref, o_ref):
    x = x_ref[...]
    ms = jnp.mean(x * x, axis=-1, keepdims=True)
    o_ref[...] = (x * lax.rsqrt(ms + NORM_EPS) * w_ref[...]).astype(o_ref.dtype)


def rmsnorm(x, w, out_dtype, br=256):
    r, d = x.shape
    return pl.pallas_call(
        _rmsnorm_kernel,
        grid=(r // br,),
        in_specs=[pl.BlockSpec((br, d), lambda i: (i, 0)), pl.BlockSpec((1, d), lambda i: (0, 0))],
        out_specs=pl.BlockSpec((br, d), lambda i: (i, 0)),
        out_shape=jax.ShapeDtypeStruct((r, d), out_dtype),
        compiler_params=_cparams(1),
        name="rmsnorm",
    )(x, w.reshape(1, d))


def _rmsnorm_join_kernel(xp_ref, xs_ref, w_ref, o_ref, h_ref, *, n_p):
    x = jnp.where(pl.program_id(0) < n_p, xp_ref[...], xs_ref[...])
    h_ref[...] = x
    ms = jnp.mean(x * x, axis=-1, keepdims=True)
    o_ref[...] = (x * lax.rsqrt(ms + NORM_EPS) * w_ref[...]).astype(o_ref.dtype)


def rmsnorm_join(xp, xs, w, out_dtype):
    (rp, d), rs = xp.shape, xs.shape[0]
    br = rs
    assert rp % br == 0
    n_p = rp // br
    spec = pl.BlockSpec((br, d), lambda i: (i, 0))
    return pl.pallas_call(
        functools.partial(_rmsnorm_join_kernel, n_p=n_p),
        grid=(n_p + 1,),
        in_specs=[pl.BlockSpec((br, d), lambda i: (jnp.minimum(i, n_p - 1), 0)),
                  pl.BlockSpec((br, d), lambda i: (0, 0)), pl.BlockSpec((1, d), lambda i: (0, 0))],
        out_specs=[spec, spec],
        out_shape=[jax.ShapeDtypeStruct((rp + rs, d), out_dtype), jax.ShapeDtypeStruct((rp + rs, d), xp.dtype)],
        compiler_params=_cparams(1),
        name="rmsnorm_join",
    )(xp, xs, w.reshape(1, d))


def _mm_kernel(a_ref, b_ref, o_ref):
    o_ref[...] = _dot(a_ref[...], b_ref[...]).astype(o_ref.dtype)


def matmul(a, w, layer, bm, bn, col0, n_out, out_dtype=F32):
    r, k = a.shape
    assert col0 % bn == 0 and n_out % bn == 0
    return pl.pallas_call(
        _mm_kernel,
        grid=(r // bm, n_out // bn),
        in_specs=[pl.BlockSpec((bm, k), lambda i, j: (i, 0)),
                  pl.BlockSpec((None, k, bn), lambda i, j: (layer, 0, col0 // bn + j))],
        out_specs=pl.BlockSpec((bm, bn), lambda i, j: (i, j)),
        out_shape=jax.ShapeDtypeStruct((r, n_out), out_dtype),
        compiler_params=_cparams(2),
        name="proj_in",
    )(a, w)


def _swiglu_kernel(a_ref, g_ref, u_ref, o_ref):
    a = a_ref[...]
    g = _dot(a, g_ref[...])
    u = _dot(a, u_ref[...])
    o_ref[...] = (g * _sigmoid(g) * u).astype(o_ref.dtype)


def swiglu_matmul(a, wg, wu, layer, bm, bn):
    r, k = a.shape
    n = wg.shape[2]
    wspec = pl.BlockSpec((None, k, bn), lambda i, j: (layer, 0, j))
    return pl.pallas_call(
        _swiglu_kernel,
        grid=(r // bm, pl.cdiv(n, bn)),
        in_specs=[pl.BlockSpec((bm, k), lambda i, j: (i, 0)), wspec, wspec],
        out_specs=pl.BlockSpec((bm, bn), lambda i, j: (i, j)),
        out_shape=jax.ShapeDtypeStruct((r, n), BF16),
        compiler_params=_cparams(2),
        name="ffn_swiglu",
    )(a, wg, wu)


def _mm_res_kernel(*refs, n_a):
    a_refs, b_refs, res_ref, o_ref = refs[:n_a], refs[n_a:2 * n_a], refs[2 * n_a], refs[2 * n_a + 1]
    kk = pl.program_id(2)

    @pl.when(kk == 0)
    def _():
        o_ref[...] = res_ref[...]

    acc = _dot(a_refs[0][...], b_refs[0][...])
    for a_ref, b_ref in zip(a_refs[1:], b_refs[1:]):
        acc = acc + _dot(a_ref[...], b_ref[...])
    o_ref[...] += acc


def matmul_residual(a_list, w, layer, res, bm, bn, bk_list):
    r = res.shape[0]
    n = w.shape[2]
    nk = a_list[0].shape[1] // bk_list[0]
    in_specs, row0 = [], 0
    for a, bk in zip(a_list, bk_list):
        assert a.shape[1] == nk * bk
        in_specs.append(pl.BlockSpec((bm, bk), lambda i, j, k: (i, k)))
    for a, bk in zip(a_list, bk_list):
        assert row0 % bk == 0
        in_specs.append(pl.BlockSpec((None, bk, bn), functools.partial(
            lambda i, j, k, off: (layer, off + k, j), off=row0 // bk)))
        row0 += a.shape[1]
    in_specs.append(pl.BlockSpec((bm, bn), lambda i, j, k: (i, j)))
    return pl.pallas_call(
        functools.partial(_mm_res_kernel, n_a=len(a_list)),
        grid=(r // bm, n // bn, nk),
        in_specs=in_specs,
        out_specs=pl.BlockSpec((bm, bn), lambda i, j, k: (i, j)),
        out_shape=jax.ShapeDtypeStruct((r, n), F32),
        compiler_params=_cparams(3),
        name="proj_residual",
    )(*a_list, *([w] * len(a_list)), res)


def _shifted(prev8, x, k):
    ext = jnp.concatenate([prev8, x], axis=0)
    return pltpu.roll(ext, k, axis=0)[SUBLANES:]


def _seq_specs(width, col_block, tb, n_tb, row0):
    assert row0 % tb == 0 and tb % SUBLANES == 0
    cur = pl.BlockSpec((tb, width), lambda b, t: (row0 // tb + b * n_tb + t, col_block))
    prev = pl.BlockSpec((SUBLANES, width), lambda b, t: (
        jnp.maximum((row0 + (b * n_tb + t) * tb) // SUBLANES - 1, 0), col_block))
    state = pl.BlockSpec((SUBLANES, width), lambda b, t: (b, 0))
    return cur, prev, state


def _row_spec(width, tb, n_tb, row0=0, col_block=0):
    return pl.BlockSpec((tb, width), lambda b, t: (row0 // tb + b * n_tb + t, col_block))


def _const_spec(shape):
    return pl.BlockSpec(shape, lambda b, t: (0,) * len(shape))


def _pad_state_rows(st):
    b, k, c = st.shape
    return jnp.pad(st, ((0, 0), (SUBLANES - k, 0), (0, 0))).reshape(b * SUBLANES, c)


def _rwkv_prep_kernel(cur_ref, prev_ref, st_ref, mix_ref, w0_ref, a0_ref, wwa_ref, g2_ref, kkw_ref, kaw_ref, j_ref,
                      r_o, kk_o, d_o, b_o, k_o, v_o, g_o):
    x = cur_ref[...]
    prev8 = jnp.where(pl.program_id(1) == 0, st_ref[...], prev_ref[...])
    xs = x + (_shifted(prev8, x, 1) - x) * mix_ref[...]
    w3 = RWKV_WIDTH
    r, k, v = xs[:, 0:w3], xs[:, w3:2 * w3], xs[:, 2 * w3:3 * w3]
    wa = xs[:, 3 * w3:3 * w3 + LANES]
    glo = xs[:, 3 * w3 + LANES:3 * w3 + 2 * LANES]
    lane = lax.broadcasted_iota(jnp.int32, wa.shape, 1)
    lora = _dot3(jnp.where(lane < DECAY_LORA, jnp.tanh(wa), wa), wwa_ref[...])
    w = -_softplus(-(w0_ref[...] + lora[:, :w3])) - 0.5
    d = jnp.exp(-jnp.exp(w))
    a = _sigmoid(a0_ref[...] + lora[:, w3:])
    g = _dot3(_sigmoid(glo), g2_ref[...])
    kn = k * kkw_ref[...]
    kk = kn * lax.rsqrt(_segsum(kn * kn, j_ref[...]) + L2_EPS)
    r_o[...] = r
    kk_o[...] = kk
    d_o[...] = d
    b_o[...] = kk * a
    k_o[...] = k * (1.0 + (a - 1.0) * kaw_ref[...])
    v_o[...] = v
    g_o[...] = g


def rwkv_prep(p_rwkv, st8, prm, n_seq, t_len, tb, row0):
    n_tb = t_len // tb
    cur, prev, state = _seq_specs(RWKV_PROJ, 0, tb, n_tb, row0)
    out = jax.ShapeDtypeStruct((n_seq * t_len, RWKV_WIDTH), F32)
    ospec = _row_spec(RWKV_WIDTH, tb, n_tb)
    return pl.pallas_call(
        _rwkv_prep_kernel,
        grid=(n_seq, n_tb),
        in_specs=[cur, prev, state, _const_spec((1, RWKV_PROJ)), _const_spec((1, RWKV_WIDTH)),
                  _const_spec((1, RWKV_WIDTH)), _const_spec((LANES, 2 * RWKV_WIDTH)),
                  _const_spec((GATE_LORA, RWKV_WIDTH)), _const_spec((1, RWKV_WIDTH)), _const_spec((1, RWKV_WIDTH)),
                  _const_spec((LANES, LANES))],
        out_specs=[ospec] * 7,
        out_shape=[out] * 7,
        compiler_params=_cparams(2),
        name="rwkv_prep",
    )(p_rwkv, p_rwkv, st8, prm["mix"], prm["w0"], prm["a0"], prm["wwa"], prm["g2"], prm["k_k"], prm["k_a"],
      prm["j64"])


def _rwkv_post_kernel(y_ref, r_ref, k_ref, v_ref, g_ref, lnw_ref, lnb_ref, rk_ref, j_ref, o_ref):
    j = j_ref[...]
    y = y_ref[...]
    inv_n = 1.0 / RWKV_HEAD_DIM
    yc = y - _segsum(y, j) * inv_n
    var = _segsum(yc * yc, j) * inv_n
    out = yc * lax.rsqrt(var + RWKV_LN_EPS) * lnw_ref[...] + lnb_ref[...]
    out = out + _segsum(r_ref[...] * k_ref[...] * rk_ref[...], j) * v_ref[...]
    o_ref[...] = (out * g_ref[...]).astype(o_ref.dtype)


def rwkv_post(y, r, k, v, g, prm, tb):
    rows = y.shape[0]
    spec = pl.BlockSpec((tb, RWKV_WIDTH), lambda i: (i, 0))
    cspec = pl.BlockSpec((1, RWKV_WIDTH), lambda i: (0, 0))
    return pl.pallas_call(
        _rwkv_post_kernel,
        grid=(rows // tb,),
        in_specs=[spec] * 5 + [cspec] * 3 + [pl.BlockSpec((LANES, LANES), lambda i: (0, 0))],
        out_specs=spec,
        out_shape=jax.ShapeDtypeStruct((rows, RWKV_WIDTH), BF16),
        compiler_params=_cparams(1),
        name="rwkv_post",
    )(y, r, k, v, g, prm["ln_w"], prm["ln_b"], prm["r_k"], prm["j64"])


def _scan_kernel(kk_ref, d_ref, b_ref, k_ref, r_ref, v_ref, s0_ref, y_ref, s_ref, *scratch, tb, dj, di, lane_dup):
    ng = di // SUBLANES
    half = LANES // 2

    @pl.when(pl.program_id(1) == 0)
    def _():
        s_ref[...] = s0_ref[...]

    if lane_dup:
        key_refs, v_s, y_s = scratch[:5], scratch[5], scratch[6]
        n_seq = kk_ref.shape[0]
        heads = half // n_seq
        assert dj == half and 2 * di == half
        zpad = jnp.zeros((LANES, half), F32)

        def head_rows(ref, t):
            return ref[:, pl.ds(pl.multiple_of(t * heads, heads), heads), :].reshape(half, half)

        def fill(t, carry):
            for src, dst in zip((kk_ref, d_ref, b_ref, k_ref, r_ref), key_refs):
                m = head_rows(src, t)
                m = jnp.concatenate([jnp.concatenate([m, m], axis=0), zpad], axis=1)
                dst[pl.ds(pl.multiple_of(t * dj, dj), dj), :] = m.T[:dj]
            mv = head_rows(v_ref, t)
            mv = jnp.concatenate([jnp.concatenate([mv, jnp.zeros_like(mv)], axis=0), zpad], axis=1)
            zv = mv.T
            v_s[pl.ds(pl.multiple_of(t * di, di), di), :] = zv[:di] + pltpu.roll(zv[di:2 * di], half, axis=1)
            return carry

        lax.fori_loop(0, tb, fill, 0, unroll=4)
        kk_s, d_s, b_s, k_s, r_s = key_refs

        def row(ref, t, j):
            return ref[pl.ds(t * dj + j, 1), :]

        def vrows(t, g):
            return pl.ds(pl.multiple_of(t * di + g * SUBLANES, SUBLANES), SUBLANES)

        load_v = lambda t, g: v_s[vrows(t, g), :]

        def store_y(t, g, val):
            y_s[vrows(t, g), :] = val
    else:
        kk_s, d_s, b_s, k_s, r_s = kk_ref, d_ref, b_ref, k_ref, r_ref

        def row(ref, t, j):
            return ref[0, t, pl.ds(j, 1), :]

        load_v = lambda t, g: v_ref[0, t, pl.ds(g * SUBLANES, SUBLANES), :]

        def store_y(t, g, val):
            y_ref[0, t, pl.ds(g * SUBLANES, SUBLANES), :] = val

    def sl(g):
        return pl.ds(g * SUBLANES, SUBLANES)

    zero = tuple(jnp.zeros((SUBLANES, LANES), F32) for _ in range(ng))

    def first_dot(j, acc):
        kkj = row(kk_s, 0, j)
        return tuple(acc[g] + s_ref[0, j, sl(g), :] * kkj for g in range(ng))

    def step(t, s_kk):
        sa = [-a for a in s_kk]
        v = [load_v(t, g) for g in range(ng)]
        t_next = jnp.minimum(t + 1, tb - 1)

        def update(j, carry):
            yacc, nacc = carry
            dj_, bj, kj, rj = row(d_s, t, j), row(b_s, t, j), row(k_s, t, j), row(r_s, t, j)
            kkn = row(kk_s, t_next, j)
            y_out, n_out = [], []
            for g in range(ng):
                s = s_ref[0, j, sl(g), :] * dj_ + sa[g] * bj + v[g] * kj
                s_ref[0, j, sl(g), :] = s
                y_out.append(yacc[g] + s * rj)
                n_out.append(nacc[g] + s * kkn)
            return tuple(y_out), tuple(n_out)

        yacc, nacc = lax.fori_loop(0, dj, update, (zero, zero), unroll=8)
        for g in range(ng):
            store_y(t, g, yacc[g])
        return nacc

    lax.fori_loop(0, tb, step, lax.fori_loop(0, dj, first_dot, zero, unroll=8))
    if lane_dup:
        zrows = jnp.zeros((LANES - 2 * di, LANES), F32)

        def drain(t, carry):
            ys = y_s[pl.ds(pl.multiple_of(t * di, di), di), :]
            ym = jnp.concatenate([ys, pltpu.roll(ys, half, axis=1), zrows], axis=0)
            y_ref[:, pl.ds(pl.multiple_of(t * heads, heads), heads), :] = ym.T[:half, :half].reshape(n_seq, heads, half)
            return carry

        lax.fori_loop(0, tb, drain, 0, unroll=4)


def dplr_scan(kk, d, b, k, r, v, s0, tb):
    lane_dup = kk.ndim == 3
    dj, di = s0.shape[1], s0.shape[2]
    if lane_dup:
        n_seq, rows, width = kk.shape
        heads = LANES // 2 // n_seq
        g_n, t_len = 1, rows // heads
        assert width == LANES // 2 and n_seq * heads * 2 == LANES and v.shape == kk.shape
        jspec = ispec = pl.BlockSpec((n_seq, tb * heads, width), lambda g, t: (0, t, 0))
        scratch = [pltpu.VMEM((tb * dj, LANES), F32)] * 5 + [pltpu.VMEM((tb * di, LANES), F32)] * 2
    else:
        g_n, t_len = kk.shape[:2]
        assert kk.shape[2:] == (dj, LANES) and v.shape[2:] == (di, LANES)
        jspec = pl.BlockSpec((1, tb, dj, LANES), lambda g, t: (g, t, 0, 0))
        ispec = pl.BlockSpec((1, tb, di, LANES), lambda g, t: (g, t, 0, 0))
        scratch = []
    sspec = pl.BlockSpec((1, dj, di, LANES), lambda g, t: (g, 0, 0, 0))
    return pl.pallas_call(
        functools.partial(_scan_kernel, tb=tb, dj=dj, di=di, lane_dup=lane_dup),
        grid=(g_n, t_len // tb),
        in_specs=[jspec] * 5 + [ispec, sspec],
        out_specs=[ispec, sspec],
        out_shape=[jax.ShapeDtypeStruct(v.shape, F32), jax.ShapeDtypeStruct(s0.shape, F32)],
        scratch_shapes=scratch,
        compiler_params=_cparams(2),
        name="dplr_scan",
    )(kk, d, b, k, r, v, s0)


def _to_key_layout(x, n_seq, t_len, heads, dj, dup):
    x = jnp.broadcast_to(x.reshape(1, n_seq, t_len, heads, dj), (dup, n_seq, t_len, heads, dj))
    x = x.transpose(2, 4, 0, 1, 3)
    g_n = dup * n_seq * heads // LANES
    return x.reshape(t_len, dj, g_n, LANES).transpose(2, 0, 1, 3)


def _to_value_layout(x, n_seq, t_len, heads, dv, dup):
    di = dv // dup
    x = x.reshape(n_seq, t_len, heads, dup, di).transpose(1, 4, 3, 0, 2)
    g_n = dup * n_seq * heads // LANES
    return x.reshape(t_len, di, g_n, LANES).transpose(2, 0, 1, 3)


def _from_value_layout(y, n_seq, t_len, heads, dv, dup):
    g_n, _, di, _ = y.shape
    y = y.transpose(1, 2, 0, 3).reshape(t_len, di, dup, n_seq, heads)
    return y.transpose(3, 0, 4, 2, 1).reshape(n_seq * t_len, heads * dv)


def _state_to_layout(s, dup, value_major):
    bsz, heads = s.shape[:2]
    if value_major:
        dv, dj = s.shape[2:]
        s = s.reshape(bsz, heads, dup, dv // dup, dj).transpose(4, 3, 2, 0, 1)
    else:
        dj, dv = s.shape[2:]
        s = s.reshape(bsz, heads, dj, dup, dv // dup).transpose(2, 4, 3, 0, 1)
    g_n = dup * bsz * heads // LANES
    return s.reshape(dj, dv // dup, g_n, LANES).transpose(2, 0, 1, 3)


def _state_from_layout(s, bsz, heads, dup, value_major):
    g_n, dj, di, _ = s.shape
    s = s.transpose(1, 2, 0, 3).reshape(dj, di, dup, bsz, heads)
    if value_major:
        return s.transpose(3, 4, 2, 1, 0).reshape(bsz, heads, dup * di, dj)
    return s.transpose(3, 4, 0, 2, 1).reshape(bsz, heads, dj, dup * di)


def _swa_kernel(sink_ref, q_ref, kc_ref, vc_ref, kp_ref, vp_ref, o_ref, *, tq, has_cache):
    s_len = WINDOW + tq
    kfull = jnp.concatenate([kp_ref[...], kc_ref[...]], axis=0)
    vfull = jnp.concatenate([vp_ref[...], vc_ref[...]], axis=0)
    lane = lax.broadcasted_iota(jnp.int32, kfull.shape, 1)
    low = lane < SWA_HEAD_DIM
    kswap = pltpu.roll(kfull, SWA_HEAD_DIM, axis=1)
    vswap = pltpu.roll(vfull, SWA_HEAD_DIM, axis=1)
    k_lo = [jnp.where(low, kfull, 0.0), jnp.where(low, kswap, 0.0)]
    k_hi = [jnp.where(low, 0.0, kswap), jnp.where(low, 0.0, kfull)]
    v_lo = [jnp.where(low, vfull, 0.0), jnp.where(low, vswap, 0.0)]
    v_hi = [jnp.where(low, 0.0, vswap), jnp.where(low, 0.0, vfull)]
    t_idx = lax.broadcasted_iota(jnp.int32, (tq, s_len), 0)
    s_idx = lax.broadcasted_iota(jnp.int32, (tq, s_len), 1)
    delta = WINDOW + t_idx - s_idx
    valid = (delta >= 0) & (delta <= WINDOW)
    if not has_cache:
        valid = valid & ((s_idx >= WINDOW) | (pl.program_id(1) > 0))
    contract_last = (((1,), (1,)), ((), ()))
    for hp in range(SWA_HEADS // 2):
        qp = q_ref[:, hp * LANES:(hp + 1) * LANES]
        kv = (2 * hp) // SWA_GROUP
        acc = None
        for half, (kmat, vmat) in enumerate(((k_lo[kv], v_lo[kv]), (k_hi[kv], v_hi[kv]))):
            sink = sink_ref[2 * hp + half]
            s = lax.dot_general(qp, kmat, contract_last, preferred_element_type=F32) * SWA_SCALE
            s = jnp.where(valid, s, -jnp.inf)
            m = jnp.maximum(jnp.max(s, axis=-1, keepdims=True), sink)
            e = jnp.exp(s - m)
            den = jnp.sum(e, axis=-1, keepdims=True) + jnp.exp(sink - m)
            o = _dot(e, vmat) / den
            acc = o if acc is None else acc + o
        o_ref[:, hp * LANES:(hp + 1) * LANES] = acc.astype(o_ref.dtype)


def swa_attention(sinks, p_swa, n_seq, t_len, tq, row0, cache_k=None, cache_v=None):
    n_blk = t_len // tq
    has_cache = cache_k is not None
    kcol, vcol = SWA_WIDTH // LANES, SWA_WIDTH // LANES + 1
    q_spec = _row_spec(SWA_WIDTH, tq, n_blk, row0)
    kc_spec = _row_spec(LANES, tq, n_blk, row0, kcol)
    vc_spec = _row_spec(LANES, tq, n_blk, row0, vcol)
    if has_cache:
        kp_spec = vp_spec = pl.BlockSpec((WINDOW, LANES), lambda b, t: (b, 0))
        kp_arr, vp_arr = cache_k, cache_v
    else:
        assert tq == WINDOW and row0 == 0

        def prev_rows(b, t):
            return jnp.maximum(b * n_blk + t - 1, 0)

        kp_spec = pl.BlockSpec((WINDOW, LANES), lambda b, t: (prev_rows(b, t), kcol))
        vp_spec = pl.BlockSpec((WINDOW, LANES), lambda b, t: (prev_rows(b, t), vcol))
        kp_arr = vp_arr = p_swa
    return pl.pallas_call(
        functools.partial(_swa_kernel, tq=tq, has_cache=has_cache),
        grid=(n_seq, n_blk),
        in_specs=[pl.BlockSpec(memory_space=pltpu.SMEM), q_spec, kc_spec, vc_spec, kp_spec, vp_spec],
        out_specs=_row_spec(SWA_WIDTH, tq, n_blk),
        out_shape=jax.ShapeDtypeStruct((n_seq * t_len, SWA_WIDTH), BF16),
        compiler_params=_cparams(2),
        name="swa",
    )(sinks, p_swa, p_swa, p_swa, kp_arr, vp_arr)


def _gdn_prep_kernel(cur_ref, prev_ref, st_ref, ba_ref, cw_ref, alog_ref, dt_ref, eb_ref, eg_ref, j_ref,
                     kn_o, d_o, b_o, v_o, q_o, *, chunked):
    x = cur_ref[...]
    prev8 = jnp.where(pl.program_id(1) == 0, st_ref[...], prev_ref[...])
    y = x * cw_ref[3:4, :]
    for k in range(1, CONV_WIDTH):
        y = y + _shifted(prev8, x, k) * cw_ref[3 - k:4 - k, :]
    act = y * _sigmoid(y)
    w2 = GDN_WIDTH
    q, k, v = act[:, :w2], act[:, w2:2 * w2], act[:, 2 * w2:]
    j = j_ref[...]
    qn = q * lax.rsqrt(_segsum(q * q, j) + L2_EPS) * (GDN_HEAD_DIM ** -0.5)
    kn = k * lax.rsqrt(_segsum(k * k, j) + L2_EPS)
    ba = ba_ref[...]
    ba = jnp.where(lax.broadcasted_iota(jnp.int32, ba.shape, 1) < 2 * GDN_HEADS, ba, 0.0)
    beta = _dot_lhs2(_sigmoid(ba), eb_ref[...])
    gate = _dot_lhs2(-jnp.exp(alog_ref[...]) * _softplus(ba + dt_ref[...]), eg_ref[...])
    kn_o[...] = kn
    q_o[...] = qn
    if chunked:
        d_o[...] = gate
        b_o[...] = beta
        v_o[...] = v
    else:
        eg = jnp.exp(gate)
        d_o[...] = eg
        b_o[...] = eg * beta * kn
        v_o[...] = beta * v


def gdn_prep(p_gdn, p_swa, st8, prm, n_seq, t_len, tb, row0, chunked):
    n_tb = t_len // tb
    cur, prev, state = _seq_specs(GDN_CONV_DIM, 0, tb, n_tb, row0)
    ba_spec = _row_spec(LANES, tb, n_tb, row0)
    out = jax.ShapeDtypeStruct((n_seq * t_len, GDN_WIDTH), F32)
    ospec = _row_spec(GDN_WIDTH, tb, n_tb)
    return pl.pallas_call(
        functools.partial(_gdn_prep_kernel, chunked=chunked),
        grid=(n_seq, n_tb),
        in_specs=[cur, prev, state, ba_spec, _const_spec((SUBLANES, GDN_CONV_DIM)), _const_spec((1, LANES)),
                  _const_spec((1, LANES)), _const_spec((LANES, GDN_WIDTH)),
                  _const_spec((LANES, GDN_WIDTH)), _const_spec((LANES, LANES))],
        out_specs=[ospec] * 5,
        out_shape=[out] * 5,
        compiler_params=_cparams(2),
        name="gdn_prep",
    )(p_gdn, p_gdn, st8, p_swa, prm["conv_w"], prm["a_log"], prm["dt"], prm["e_beta"], prm["e_gate"], prm["j128"])


def _gdn_post_kernel(o_ref_in, z_ref, w_ref, j_ref, o_ref):
    o = o_ref_in[...]
    z = z_ref[...]
    ms = _segsum(o * o, j_ref[...]) * (1.0 / GDN_HEAD_DIM)
    y = o * lax.rsqrt(ms + NORM_EPS) * w_ref[...]
    o_ref[...] = (y * (z * _sigmoid(z))).astype(o_ref.dtype)


def gdn_post(o, p_gdn, prm, tb, row0):
    rows = o.shape[0]
    spec = pl.BlockSpec((tb, GDN_WIDTH), lambda i: (i, 0))
    zspec = pl.BlockSpec((tb, GDN_WIDTH), lambda i: (row0 // tb + i, GDN_CONV_DIM // GDN_WIDTH))
    return pl.pallas_call(
        _gdn_post_kernel,
        grid=(rows // tb,),
        in_specs=[spec, zspec, pl.BlockSpec((1, GDN_WIDTH), lambda i: (0, 0)),
                  pl.BlockSpec((LANES, LANES), lambda i: (0, 0))],
        out_specs=spec,
        out_shape=jax.ShapeDtypeStruct((rows, GDN_WIDTH), BF16),
        compiler_params=_cparams(1),
        name="gdn_post",
    )(o, p_gdn, prm["norm_w"], prm["j128"])


def _split3(x):
    hi = x.astype(BF16)
    r = x - hi.astype(F32)
    mid = r.astype(BF16)
    return hi, mid, (r - mid.astype(F32)).astype(BF16)


def _dotb(a, b):
    return _dot(a.astype(BF16), b.astype(BF16))


def _dot3_packed(x, w, c):
    xh = x.astype(BF16).astype(F32)
    hi_lo = xh + pltpu.roll(x - xh, c, axis=1)
    wh, wl = _split(w)
    lhs = jnp.concatenate([hi_lo, xh], axis=1).astype(BF16)
    rhs = jnp.concatenate([wh, wh, wl, jnp.zeros_like(wh)], axis=0)
    return _dot(lhs, rhs)


def _pad_rows(x, rows):
    return jnp.concatenate([x, jnp.zeros((rows - x.shape[0], x.shape[1]), x.dtype)], axis=0)


def _gdn_chunk_kernel(q_ref, k_ref, v_ref, beta_ref, g_ref, s0_ref, o_ref, s_ref, *, hb, nc):
    c = GDN_CHUNK

    @pl.when(pl.program_id(2) == 0)
    def _():
        s_ref[...] = s0_ref[...]

    row = lax.broadcasted_iota(jnp.int32, (c, LANES), 0)
    col = lax.broadcasted_iota(jnp.int32, (c, LANES), 1)
    incl, strict = row >= col, row > col
    eye = (row == col).astype(F32)
    row3 = lax.broadcasted_iota(jnp.int32, (c, 2 * LANES), 0)
    col3 = lax.broadcasted_iota(jnp.int32, (c, 2 * LANES), 1)
    tri3 = ((row3 >= col3 % c) & (col3 < 3 * c)).astype(BF16)
    lane0_3 = (lax.broadcasted_iota(jnp.int32, (c, 3 * LANES), 1) % LANES == 0).astype(BF16)
    zero_c = jnp.zeros((c, LANES), BF16)
    contract_last = (((1,), (1,)), ((), ()))
    contract_first = (((0,), (0,)), ((), ()))

    units = [(h, ci) for ci in range(nc) for h in range(hb)]
    n_u = len(units)

    def load(ref):
        return [ref[ci * c:(ci + 1) * c, h * LANES:(h + 1) * LANES] for h, ci in units]

    q, k, v, beta, g = load(q_ref), load(k_ref), load(v_ref), load(beta_ref), load(g_ref)
    lanes = lambda x, u: x[:, u * LANES:(u + 1) * LANES]
    g_parts = jnp.concatenate([jnp.concatenate(list(_split3(x)) + [zero_c], axis=0) for x in g], axis=1)
    gc_all = _dot(tri3, g_parts)
    gc = [lanes(gc_all, u) for u in range(n_u)]
    gc_parts = jnp.concatenate([_pad_rows(jnp.concatenate(_split3(x), axis=1), LANES) for x in gc], axis=0)
    gc_row_all = lax.dot_general(lane0_3, gc_parts, contract_last, preferred_element_type=F32)
    dec = [jnp.where(incl, jnp.exp(jnp.where(incl, gc[u] - lanes(gc_row_all, u), 0.0)), 0.0) for u in range(n_u)]
    kb = [k[u] * beta[u] for u in range(n_u)]
    gram = [lax.dot_general(jnp.concatenate([kb[u], q[u]], axis=0).astype(BF16), _pad_rows(k[u].astype(BF16), LANES),
                            contract_last, preferred_element_type=F32) for u in range(n_u)]
    a = [jnp.where(strict, gram[u][:c] * dec[u], 0.0) for u in range(n_u)]
    attn = [jnp.where(incl, gram[u][c:] * dec[u], 0.0)[:, :c].astype(BF16) for u in range(n_u)]
    tinv = [eye - x for x in a]
    p = [_dot3_packed(x, x, c) for x in a]
    n_levels = c.bit_length() - 2
    for lvl in range(n_levels):
        last = lvl + 1 == n_levels
        lhs = tinv if last else [jnp.concatenate([tinv[u], p[u]], axis=0) for u in range(n_u)]
        prod = [_dot3_packed(lhs[u], p[u], c) for u in range(n_u)]
        tinv = [tinv[u] + prod[u][:c] for u in range(n_u)]
        if not last:
            p = [prod[u][c:] for u in range(n_u)]
    eg = [jnp.exp(x) for x in gc]
    uw = [_dotb(tinv[u][:, :c], jnp.concatenate([v[u] * beta[u], kb[u] * eg[u]], axis=1)) for u in range(n_u)]
    w_qe = [jnp.concatenate([uw[u][:, LANES:], q[u] * eg[u]], axis=0).astype(BF16) for u in range(n_u)]
    g_last = [x[c - 1:c, :] for x in gc]
    kd = [(k[u] * jnp.exp(g_last[u] - gc[u])).astype(BF16) for u in range(n_u)]
    eg_last = [jnp.exp(x) for x in g_last]

    s = [s_ref[0, h] for h in range(hb)]
    o_rows = []
    for ci in range(nc):
        us = [ci * hb + h for h in range(hb)]
        wq = [_dot(w_qe[u], s[h].astype(BF16)) for h, u in enumerate(us)]
        v_new = [(uw[u][:, :LANES] - wq[h][:c]).astype(BF16) for h, u in enumerate(us)]
        o_rows.append(jnp.concatenate([wq[h][c:] + _dot(attn[u], v_new[h]) for h, u in enumerate(us)], axis=1))
        s = [s[h] * eg_last[u] + lax.dot_general(kd[u], v_new[h], contract_first, preferred_element_type=F32)
             for h, u in enumerate(us)]
    o_ref[...] = jnp.concatenate(o_rows, axis=0)
    s_ref[0] = jnp.stack(s)


def gdn_chunked(q, k, v, beta, g, s0, n_seq, t_len, hb, nc):
    n_ch = t_len // (GDN_CHUNK * nc)
    spec = pl.BlockSpec((GDN_CHUNK * nc, hb * LANES), lambda b, h, c: (b * n_ch + c, h))
    sspec = pl.BlockSpec((1, hb, GDN_HEAD_DIM, GDN_HEAD_DIM), lambda b, h, c: (b, h, 0, 0))
    return pl.pallas_call(
        functools.partial(_gdn_chunk_kernel, hb=hb, nc=nc),
        grid=(n_seq, GDN_HEADS // hb, n_ch),
        in_specs=[spec] * 5 + [sspec],
        out_specs=[spec, sspec],
        out_shape=[jax.ShapeDtypeStruct(q.shape, F32), jax.ShapeDtypeStruct(s0.shape, F32)],
        compiler_params=_cparams(3),
        name="gdn_chunk",
    )(q, k, v, beta, g, s0)


def rwkv_mix(p_rwkv, shift_state, wkv_state, prm, n_seq, t_len, row0, tb_prep, tb_scan):
    st8 = _pad_state_rows(shift_state[:, None, :])
    r, kk, d, b, k, v, g = rwkv_prep(p_rwkv, st8, prm, n_seq, t_len, tb_prep, row0)
    dup = max(1, LANES // (n_seq * RWKV_HEADS))
    s0 = _state_to_layout(wkv_state, dup, value_major=True)
    if dup == 2:
        tl = lambda x: x.reshape(n_seq, t_len * RWKV_HEADS, RWKV_HEAD_DIM)
        y, s1 = dplr_scan(tl(kk), tl(d), tl(b), tl(k), tl(r), tl(v), s0, tb_scan)
        y = y.reshape(n_seq * t_len, RWKV_WIDTH)
    else:
        kl = functools.partial(_to_key_layout, n_seq=n_seq, t_len=t_len, heads=RWKV_HEADS, dj=RWKV_HEAD_DIM, dup=dup)
        vt = _to_value_layout(v, n_seq, t_len, RWKV_HEADS, RWKV_HEAD_DIM, dup)
        y, s1 = dplr_scan(kl(kk), kl(d), kl(b), kl(k), kl(r), vt, s0, tb_scan)
        y = _from_value_layout(y, n_seq, t_len, RWKV_HEADS, RWKV_HEAD_DIM, dup)
    out = rwkv_post(y, r, k, v, g, prm, tb_prep)
    return out, _state_from_layout(s1, n_seq, RWKV_HEADS, dup, value_major=True)


def gdn_mix(p_gdn, p_swa, conv_state, ssm_state, prm, n_seq, t_len, row0, tb_prep, tb_scan):
    st8 = _pad_state_rows(conv_state)
    if t_len % GDN_CHUNK == 0:
        kn, g, beta, v, q = gdn_prep(p_gdn, p_swa, st8, prm, n_seq, t_len, tb_prep, row0, chunked=True)
        nc = 4 if t_len % (4 * GDN_CHUNK) == 0 else 1
        o, s1 = gdn_chunked(q, kn, v, beta, g, ssm_state, n_seq, t_len, hb=4, nc=nc)
        return gdn_post(o, p_gdn, prm, tb_prep, row0), s1
    kn, d, b, v, q = gdn_prep(p_gdn, p_swa, st8, prm, n_seq, t_len, tb_prep, row0, chunked=False)
    dup = 2 * max(1, LANES // (2 * n_seq * GDN_HEADS))
    kl = functools.partial(_to_key_layout, n_seq=n_seq, t_len=t_len, heads=GDN_HEADS, dj=GDN_HEAD_DIM, dup=dup)
    vt = _to_value_layout(v, n_seq, t_len, GDN_HEADS, GDN_HEAD_DIM, dup)
    s0 = _state_to_layout(ssm_state, dup, value_major=False)
    knl = kl(kn)
    o, s1 = dplr_scan(knl, kl(d), kl(b), knl, kl(q), vt, s0, tb_scan)
    o = _from_value_layout(o, n_seq, t_len, GDN_HEADS, GDN_HEAD_DIM, dup)
    out = gdn_post(o, p_gdn, prm, tb_prep, row0)
    return out, _state_from_layout(s1, n_seq, GDN_HEADS, dup, value_major=False)


def _block_ones(seg):
    i = jnp.arange(LANES)
    return (i[:, None] // seg == i[None, :] // seg).astype(BF16)


def _head_expander(first_row):
    rows = jnp.arange(LANES)[:, None]
    cols = jnp.arange(GDN_WIDTH)[None, :] // GDN_HEAD_DIM
    return (rows == cols + first_row).astype(BF16)


def _layer_params(l, W):
    row = lambda x: x.reshape(1, -1)
    zeros = jnp.zeros((DECAY_LORA, RWKV_WIDTH), F32)
    wwa = jnp.concatenate([jnp.concatenate([W["rwkv_w2"][l], zeros], axis=1),
                           jnp.concatenate([zeros, W["rwkv_a2"][l]], axis=1)], axis=0)
    lane_row = lambda x, off: jnp.pad(x, (off, LANES - off - x.shape[0])).reshape(1, LANES)
    rwkv = dict(mix=row(W["rwkv_shift_mix"][l]), w0=row(W["rwkv_w0"][l]), a0=row(W["rwkv_a0"][l]), wwa=wwa,
                g2=W["rwkv_g2"][l], k_k=row(W["rwkv_k_k"][l]), k_a=row(W["rwkv_k_a"][l]),
                r_k=row(W["rwkv_r_k"][l]), ln_w=row(W["rwkv_ln_w"][l]), ln_b=row(W["rwkv_ln_b"][l]),
                j64=_block_ones(RWKV_HEAD_DIM))
    gdn = dict(conv_w=jnp.pad(W["gdn_conv_w"][l], ((0, SUBLANES - CONV_WIDTH), (0, 0))),
               a_log=lane_row(W["gdn_A_log"][l], GDN_HEADS), dt=lane_row(W["gdn_dt_bias"][l], GDN_HEADS),
               e_beta=_head_expander(0), e_gate=_head_expander(GDN_HEADS),
               norm_w=jnp.tile(W["gdn_norm_w"][l], GDN_HEADS).reshape(1, GDN_WIDTH), j128=_block_ones(LANES))
    return rwkv, gdn


def kernel(x_prompt, x_sample, state_rwkv_shift, state_rwkv_wkv, cache_swa_k, cache_swa_v, state_gdn_conv,
           state_gdn_ssm, norm_mix, w_in, rwkv_shift_mix, rwkv_w0, rwkv_w2, rwkv_a0, rwkv_a2, rwkv_g2, rwkv_k_k,
           rwkv_k_a, rwkv_r_k, rwkv_ln_w, rwkv_ln_b, swa_sinks, gdn_conv_w, gdn_A_log, gdn_dt_bias, gdn_norm_w,
           w_out, norm_ffn, w_gate, w_up, w_down, final_norm):
    W = dict(rwkv_shift_mix=rwkv_shift_mix, rwkv_w0=rwkv_w0, rwkv_w2=rwkv_w2, rwkv_a0=rwkv_a0, rwkv_a2=rwkv_a2,
             rwkv_g2=rwkv_g2, rwkv_k_k=rwkv_k_k, rwkv_k_a=rwkv_k_a, rwkv_r_k=rwkv_r_k, rwkv_ln_w=rwkv_ln_w,
             rwkv_ln_b=rwkv_ln_b, gdn_conv_w=gdn_conv_w, gdn_A_log=gdn_A_log, gdn_dt_bias=gdn_dt_bias,
             gdn_norm_w=gdn_norm_w)
    w_in_b, w_out_b, w_gate_b, w_up_b, w_down_b = (w.astype(BF16) for w in (w_in, w_out, w_gate, w_up, w_down))
    swa_col, gdn_col, ba_col = RWKV_PROJ, RWKV_PROJ + SWA_PROJ, RWKV_PROJ + SWA_PROJ + GDN_MAIN

    u, h = rmsnorm_join(x_prompt.reshape(ROWS_P, D_MODEL), x_sample.reshape(ROWS_D, D_MODEL), norm_mix[0], BF16)
    zero_shift = jnp.zeros((BATCH, RWKV_PROJ), F32)
    zero_wkv = jnp.zeros((BATCH, RWKV_HEADS, RWKV_HEAD_DIM, RWKV_HEAD_DIM), F32)
    zero_conv = jnp.zeros((BATCH, CONV_WIDTH - 1, GDN_CONV_DIM), F32)
    zero_ssm = jnp.zeros((BATCH, GDN_HEADS, GDN_HEAD_DIM, GDN_HEAD_DIM), F32)
    wb = cache_swa_k.shape[2]
    assert wb == WINDOW
    outs = [[] for _ in range(12)]
    bm = 1056
    for l in range(DEPTH):
        rw, gd = _layer_params(l, W)
        if l > 0:
            u = rmsnorm(h, norm_mix[l], BF16)
        p_rwkv = matmul(u, w_in_b, l, bm, 256, 0, RWKV_PROJ)
        p_swa = matmul(u, w_in_b, l, bm, 256, swa_col, SWA_PROJ)
        p_gdn = matmul(u, w_in_b, l, bm, 512, gdn_col, GDN_MAIN)
        p_ba = matmul(u, w_in_b, l, bm, LANES, ba_col, LANES)

        o_rp, wkv_p = rwkv_mix(p_rwkv, zero_shift, zero_wkv, rw, BATCH, SEQ, 0, 128, 32)
        o_rd, wkv_d = rwkv_mix(p_rwkv, state_rwkv_shift[l], state_rwkv_wkv[l], rw, DEC_BATCH, DEC_SEQ, ROWS_P,
                               DEC_SEQ, DEC_SEQ)
        o_sp = swa_attention(swa_sinks[l], p_swa, BATCH, SEQ, WINDOW, 0)
        o_sd = swa_attention(swa_sinks[l], p_swa, DEC_BATCH, DEC_SEQ, DEC_SEQ, ROWS_P,
                             cache_swa_k[l].reshape(DEC_BATCH * wb, SWA_KV_WIDTH),
                             cache_swa_v[l].reshape(DEC_BATCH * wb, SWA_KV_WIDTH))
        o_gp, ssm_p = gdn_mix(p_gdn, p_ba, zero_conv, zero_ssm, gd, BATCH, SEQ, 0, 64, 32)
        o_gd, ssm_d = gdn_mix(p_gdn, p_ba, state_gdn_conv[l], state_gdn_ssm[l], gd, DEC_BATCH, DEC_SEQ, ROWS_P,
                              DEC_SEQ, DEC_SEQ)
        o_r = jnp.concatenate([o_rp, o_rd], axis=0)
        o_s = jnp.concatenate([o_sp, o_sd], axis=0)
        o_g = jnp.concatenate([o_gp, o_gd], axis=0)
        h = matmul_residual([o_r, o_s, o_g], w_out_b, l, h, bm, 512, [RWKV_WIDTH, SWA_WIDTH, GDN_WIDTH])
        u = rmsnorm(h, norm_ffn[l], BF16)
        h1 = swiglu_matmul(u, w_gate_b, w_up_b, l, bm, 512)
        h = matmul_residual([h1], w_down_b, l, h, bm, 512, [D_FF // 2])

        def tail_p(p, n_rows, c0, c1):
            return jnp.stack([lax.slice(p, ((b + 1) * SEQ - n_rows, c0), ((b + 1) * SEQ, c1)) for b in range(BATCH)])

        def tail_d(p, n_rows, c0, c1):
            x = lax.slice(p, (ROWS_P, c0), (ROWS, c1)).reshape(DEC_BATCH, DEC_SEQ, c1 - c0)
            return x[:, DEC_SEQ - n_rows:]

        kv = lambda x: x.reshape(x.shape[0], x.shape[1], SWA_KV_HEADS, SWA_HEAD_DIM)
        k0, k1, v1 = SWA_WIDTH, SWA_WIDTH + SWA_KV_WIDTH, SWA_PROJ
        layer_out = (
            tail_p(p_rwkv, 1, 0, RWKV_PROJ)[:, 0], wkv_p, kv(tail_p(p_swa, wb, k0, k1)), kv(tail_p(p_swa, wb, k1, v1)),
            tail_p(p_gdn, CONV_WIDTH - 1, 0, GDN_CONV_DIM), ssm_p,
            tail_d(p_rwkv, 1, 0, RWKV_PROJ)[:, 0], wkv_d,
            jnp.concatenate([cache_swa_k[l], kv(tail_d(p_swa, DEC_SEQ, k0, k1))], axis=1)[:, -wb:],
            jnp.concatenate([cache_swa_v[l], kv(tail_d(p_swa, DEC_SEQ, k1, v1))], axis=1)[:, -wb:],
            tail_d(p_gdn, CONV_WIDTH - 1, 0, GDN_CONV_DIM), ssm_d)
        for lst, t in zip(outs, layer_out):
            lst.append(t)
    y = rmsnorm(h, final_norm, F32)
    y_prompt = y[:ROWS_P].reshape(BATCH, SEQ, D_MODEL)
    y_sample = y[ROWS_P:].reshape(DEC_BATCH, DEC_SEQ, D_MODEL)
    return (y_prompt, y_sample) + tuple(jnp.stack(lst) for lst in outs)
```

```python
import functools

import jax
import jax.numpy as jnp
from jax import lax
from jax.experimental import pallas as pl
from jax.experimental.pallas import tpu as pltpu

D_MODEL = 4096
BATCH = 4
SEQ = 2048
DEPTH = 4
DEC_BATCH = 32
DEC_SEQ = 8
NORM_EPS = 1e-6
L2_EPS = 1e-6
RWKV_WIDTH = 1024
RWKV_HEAD_DIM = 64
RWKV_HEADS = 16
DECAY_LORA = 64
AAA_LORA = 64
GATE_LORA = 128
RWKV_PROJ = 3 * RWKV_WIDTH + DECAY_LORA + AAA_LORA + GATE_LORA
RWKV_LN_EPS = 64e-5
SWA_WIDTH = 1024
SWA_HEAD_DIM = 64
SWA_HEADS = 16
SWA_KV_HEADS = 2
SWA_GROUP = 8
SWA_KV_WIDTH = 128
SWA_PROJ = SWA_WIDTH + 2 * SWA_KV_WIDTH
WINDOW = 128
SWA_SCALE = SWA_HEAD_DIM ** -0.5
GDN_WIDTH = 2048
GDN_HEAD_DIM = 128
GDN_HEADS = 16
GDN_CONV_DIM = 3 * GDN_WIDTH
CONV_WIDTH = 4
GDN_CHUNK = 64
GDN_MAIN = GDN_CONV_DIM + GDN_WIDTH
D_FF = 11008

ROWS_P = BATCH * SEQ
ROWS_D = DEC_BATCH * DEC_SEQ
ROWS = ROWS_P + ROWS_D

LANES = 128
SUBLANES = 8
VMEM_LIMIT = 56 * 1024 * 1024

F32 = jnp.float32
BF16 = jnp.bfloat16


def _cparams(n_axes):
    return pltpu.CompilerParams(dimension_semantics=("arbitrary",) * n_axes, vmem_limit_bytes=VMEM_LIMIT)


def _split(x):
    hi = x.astype(BF16)
    lo = (x - hi.astype(F32)).astype(BF16)
    return hi, lo


def _dot(a, b):
    return jnp.dot(a, b, preferred_element_type=F32)


def _dot_lhs2(x, m_bf16):
    hi, lo = _split(x)
    return _dot(hi, m_bf16) + _dot(lo, m_bf16)


def _dot3(x, w):
    xh, xl = _split(x)
    wh, wl = _split(w)
    return _dot(xh, wh) + (_dot(xl, wh) + _dot(xh, wl))


def _segsum(x, j_bf16):
    n = x.shape[-1] // LANES
    return jnp.concatenate([_dot_lhs2(x[:, c * LANES:(c + 1) * LANES], j_bf16) for c in range(n)], axis=-1)


def _sigmoid(x):
    return 1.0 / (1.0 + jnp.exp(-x))


def _softplus(x):
    return jnp.maximum(x, 0.0) + jnp.log(1.0 + jnp.exp(-jnp.abs(x)))


def _rmsnorm_kernel(x_ref, w_ref, o_ref):
    x = x_ref[...]
    ms = jnp.mean(x * x, axis=-1, keepdims=True)
    o_ref[...] = (x * lax.rsqrt(ms + NORM_EPS) * w_ref[...]).astype(o_ref.dtype)


def rmsnorm(x, w, out_dtype, br=256):
    r, d = x.shape
    return pl.pallas_call(
        _rmsnorm_kernel,
        grid=(r // br,),
        in_specs=[pl.BlockSpec((br, d), lambda i: (i, 0)), pl.BlockSpec((1, d), lambda i: (0, 0))],
        out_specs=pl.BlockSpec((br, d), lambda i: (i, 0)),
        out_shape=jax.ShapeDtypeStruct((r, d), out_dtype),
        compiler_params=_cparams(1),
        name="rmsnorm",
    )(x, w.reshape(1, d))


def _rmsnorm_join_kernel(xp_ref, xs_ref, w_ref, o_ref, h_ref, *, n_p):
    x = jnp.where(pl.program_id(0) < n_p, xp_ref[...], xs_ref[...])
    h_ref[...] = x
    ms = jnp.mean(x * x, axis=-1, keepdims=True)
    o_ref[...] = (x * lax.rsqrt(ms + NORM_EPS) * w_ref[...]).astype(o_ref.dtype)


def rmsnorm_join(xp, xs, w, out_dtype):
    (rp, d), rs = xp.shape, xs.shape[0]
    br = rs
    assert rp % br == 0
    n_p = rp // br
    spec = pl.BlockSpec((br, d), lambda i: (i, 0))
    return pl.pallas_call(
        functools.partial(_rmsnorm_join_kernel, n_p=n_p),
        grid=(n_p + 1,),
        in_specs=[pl.BlockSpec((br, d), lambda i: (jnp.minimum(i, n_p - 1), 0)),
                  pl.BlockSpec((br, d), lambda i: (0, 0)), pl.BlockSpec((1, d), lambda i: (0, 0))],
        out_specs=[spec, spec],
        out_shape=[jax.ShapeDtypeStruct((rp + rs, d), out_dtype), jax.ShapeDtypeStruct((rp + rs, d), xp.dtype)],
        compiler_params=_cparams(1),
        name="rmsnorm_join",
    )(xp, xs, w.reshape(1, d))


def _mm_kernel(a_ref, b_ref, o_ref):
    o_ref[...] = _dot(a_ref[...], b_ref[...]).astype(o_ref.dtype)


def matmul(a, w, layer, bm, bn, col0, n_out, out_dtype=F32):
    r, k = a.shape
    assert col0 % bn == 0 and n_out % bn == 0
    return pl.pallas_call(
        _mm_kernel,
        grid=(r // bm, n_out // bn),
        in_specs=[pl.BlockSpec((bm, k), lambda i, j: (i, 0)),
                  pl.BlockSpec((None, k, bn), lambda i, j: (layer, 0, col0 // bn + j))],
        out_specs=pl.BlockSpec((bm, bn), lambda i, j: (i, j)),
        out_shape=jax.ShapeDtypeStruct((r, n_out), out_dtype),
        compiler_params=_cparams(2),
        name="proj_in",
    )(a, w)


def _swiglu_kernel(a_ref, g_ref, u_ref, o_ref):
    a = a_ref[...]
    g = _dot(a, g_ref[...])
    u = _dot(a, u_ref[...])
    o_ref[...] = (g * _sigmoid(g) * u).astype(o_ref.dtype)


def swiglu_matmul(a, wg, wu, layer, bm, bn):
    r, k = a.shape
    n = wg.shape[2]
    wspec = pl.BlockSpec((None, k, bn), lambda i, j: (layer, 0, j))
    return pl.pallas_call(
        _swiglu_kernel,
        grid=(r // bm, pl.cdiv(n, bn)),
        in_specs=[pl.BlockSpec((bm, k), lambda i, j: (i, 0)), wspec, wspec],
        out_specs=pl.BlockSpec((bm, bn), lambda i, j: (i, j)),
        out_shape=jax.ShapeDtypeStruct((r, n), BF16),
        compiler_params=_cparams(2),
        name="ffn_swiglu",
    )(a, wg, wu)


def _mm_res_kernel(*refs, n_a):
    a_refs, b_refs, res_ref, o_ref = refs[:n_a], refs[n_a:2 * n_a], refs[2 * n_a], refs[2 * n_a + 1]
    kk = pl.program_id(2)

    @pl.when(kk == 0)
    def _():
        o_ref[...] = res_ref[...]

    acc = _dot(a_refs[0][...], b_refs[0][...])
    for a_ref, b_ref in zip(a_refs[1:], b_refs[1:]):
        acc = acc + _dot(a_ref[...], b_ref[...])
    o_ref[...] += acc


def matmul_residual(a_list, w, layer, res, bm, bn, bk_list):
    r = res.shape[0]
    n = w.shape[2]
    nk = a_list[0].shape[1] // bk_list[0]
    in_specs, row0 = [], 0
    for a, bk in zip(a_list, bk_list):
        assert a.shape[1] == nk * bk
        in_specs.append(pl.BlockSpec((bm, bk), lambda i, j, k: (i, k)))
    for a, bk in zip(a_list, bk_list):
        assert row0 % bk == 0
        in_specs.append(pl.BlockSpec((None, bk, bn), functools.partial(
            lambda i, j, k, off: (layer, off + k, j), off=row0 // bk)))
        row0 += a.shape[1]
    in_specs.append(pl.BlockSpec((bm, bn), lambda i, j, k: (i, j)))
    return pl.pallas_call(
        functools.partial(_mm_res_kernel, n_a=len(a_list)),
        grid=(r // bm, n // bn, nk),
        in_specs=in_specs,
        out_specs=pl.BlockSpec((bm, bn), lambda i, j, k: (i, j)),
        out_shape=jax.ShapeDtypeStruct((r, n), F32),
        compiler_params=_cparams(3),
        name="proj_residual",
    )(*a_list, *([w] * len(a_list)), res)


def _shifted(prev8, x, k):
    ext = jnp.concatenate([prev8, x], axis=0)
    return pltpu.roll(ext, k, axis=0)[SUBLANES:]


def _seq_specs(width, col_block, tb, n_tb, row0):
    assert row0 % tb == 0 and tb % SUBLANES == 0
    cur = pl.BlockSpec((tb, width), lambda b, t: (row0 // tb + b * n_tb + t, col_block))
    prev = pl.BlockSpec((SUBLANES, width), lambda b, t: (
        jnp.maximum((row0 + (b * n_tb + t) * tb) // SUBLANES - 1, 0), col_block))
    state = pl.BlockSpec((SUBLANES, width), lambda b, t: (b, 0))
    return cur, prev, state


def _row_spec(width, tb, n_tb, row0=0, col_block=0):
    return pl.BlockSpec((tb, width), lambda b, t: (row0 // tb + b * n_tb + t, col_block))


def _const_spec(shape):
    return pl.BlockSpec(shape, lambda b, t: (0,) * len(shape))


def _drop_ref(kernel_fn, pos, *refs):
    return kernel_fn(*refs[:pos], *refs[pos + 1:])


def _shared_rows_out(kernel_fn, n_in, rows, width, dtype, dest):
    if dest is None or isinstance(dest, int):
        return kernel_fn, jax.ShapeDtypeStruct((dest or rows, width), dtype), (), [], {}
    assert dest.shape[1] == width and dest.dtype == dtype
    return (functools.partial(_drop_ref, kernel_fn, n_in), jax.ShapeDtypeStruct(dest.shape, dtype), (dest,),
            [pl.BlockSpec(memory_space=pl.ANY)], {n_in: 0})


def _pad_state_rows(st):
    b, k, c = st.shape
    return jnp.pad(st, ((0, 0), (SUBLANES - k, 0), (0, 0))).reshape(b * SUBLANES, c)


def _rwkv_prep_kernel(cur_ref, prev_ref, st_ref, mix_ref, w0_ref, a0_ref, wwa_ref, g2_ref, kkw_ref, kaw_ref, j_ref,
                      r_o, kk_o, d_o, b_o, k_o, v_o, g_o):
    x = cur_ref[...]
    prev8 = jnp.where(pl.program_id(1) == 0, st_ref[...], prev_ref[...])
    xs = x + (_shifted(prev8, x, 1) - x) * mix_ref[...]
    w3 = RWKV_WIDTH
    r, k, v = xs[:, 0:w3], xs[:, w3:2 * w3], xs[:, 2 * w3:3 * w3]
    wa = xs[:, 3 * w3:3 * w3 + LANES]
    glo = xs[:, 3 * w3 + LANES:3 * w3 + 2 * LANES]
    lane = lax.broadcasted_iota(jnp.int32, wa.shape, 1)
    lora = _dot3(jnp.where(lane < DECAY_LORA, jnp.tanh(wa), wa), wwa_ref[...])
    w = -_softplus(-(w0_ref[...] + lora[:, :w3])) - 0.5
    d = jnp.exp(-jnp.exp(w))
    a = _sigmoid(a0_ref[...] + lora[:, w3:])
    g = _dot3(_sigmoid(glo), g2_ref[...])
    kn = k * kkw_ref[...]
    kk = kn * lax.rsqrt(_segsum(kn * kn, j_ref[...]) + L2_EPS)
    r_o[...] = r
    kk_o[...] = kk
    d_o[...] = d
    b_o[...] = kk * a
    k_o[...] = k * (1.0 + (a - 1.0) * kaw_ref[...])
    v_o[...] = v
    g_o[...] = g


def rwkv_prep(p_rwkv, st8, prm, n_seq, t_len, tb, row0):
    n_tb = t_len // tb
    cur, prev, state = _seq_specs(RWKV_PROJ, 0, tb, n_tb, row0)
    out = jax.ShapeDtypeStruct((n_seq * t_len, RWKV_WIDTH), F32)
    ospec = _row_spec(RWKV_WIDTH, tb, n_tb)
    return pl.pallas_call(
        _rwkv_prep_kernel,
        grid=(n_seq, n_tb),
        in_specs=[cur, prev, state, _const_spec((1, RWKV_PROJ)), _const_spec((1, RWKV_WIDTH)),
                  _const_spec((1, RWKV_WIDTH)), _const_spec((LANES, 2 * RWKV_WIDTH)),
                  _const_spec((GATE_LORA, RWKV_WIDTH)), _const_spec((1, RWKV_WIDTH)), _const_spec((1, RWKV_WIDTH)),
                  _const_spec((LANES, LANES))],
        out_specs=[ospec] * 7,
        out_shape=[out] * 7,
        compiler_params=_cparams(2),
        name="rwkv_prep",
    )(p_rwkv, p_rwkv, st8, prm["mix"], prm["w0"], prm["a0"], prm["wwa"], prm["g2"], prm["k_k"], prm["k_a"],
      prm["j64"])


def _rwkv_post_kernel(y_ref, r_ref, k_ref, v_ref, g_ref, lnw_ref, lnb_ref, rk_ref, j_ref, o_ref):
    j = j_ref[...]
    y = y_ref[...]
    inv_n = 1.0 / RWKV_HEAD_DIM
    yc = y - _segsum(y, j) * inv_n
    var = _segsum(yc * yc, j) * inv_n
    out = yc * lax.rsqrt(var + RWKV_LN_EPS) * lnw_ref[...] + lnb_ref[...]
    out = out + _segsum(r_ref[...] * k_ref[...] * rk_ref[...], j) * v_ref[...]
    o_ref[...] = (out * g_ref[...]).astype(o_ref.dtype)


def rwkv_post(y, r, k, v, g, prm, tb, row0, dest):
    rows = y.shape[0]
    spec = pl.BlockSpec((tb, RWKV_WIDTH), lambda i: (i, 0))
    cspec = pl.BlockSpec((1, RWKV_WIDTH), lambda i: (0, 0))
    kern, out_shape, extra, extra_specs, aliases = _shared_rows_out(_rwkv_post_kernel, 9, rows, RWKV_WIDTH, BF16, dest)
    return pl.pallas_call(
        kern,
        grid=(rows // tb,),
        in_specs=[spec] * 5 + [cspec] * 3 + [pl.BlockSpec((LANES, LANES), lambda i: (0, 0))] + extra_specs,
        out_specs=pl.BlockSpec((tb, RWKV_WIDTH), lambda i: ((0 if dest is None else row0 // tb) + i, 0)),
        out_shape=out_shape,
        input_output_aliases=aliases,
        compiler_params=_cparams(1),
        name="rwkv_post",
    )(y, r, k, v, g, prm["ln_w"], prm["ln_b"], prm["r_k"], prm["j64"], *extra)


def _scan_kernel(kk_ref, d_ref, b_ref, k_ref, r_ref, v_ref, s0_ref, y_ref, s_ref, *scratch, tb, dj, di, lane_dup):
    ng = di // SUBLANES
    half = LANES // 2

    @pl.when(pl.program_id(1) == 0)
    def _():
        s_ref[...] = s0_ref[...]

    if lane_dup:
        key_refs, v_s, y_s = scratch[:5], scratch[5], scratch[6]
        n_seq = kk_ref.shape[0]
        heads = half // n_seq
        assert dj == half and 2 * di == half
        zpad = jnp.zeros((LANES, half), F32)

        def head_rows(ref, t):
            return ref[:, pl.ds(pl.multiple_of(t * heads, heads), heads), :].reshape(half, half)

        def fill(t, carry):
            for src, dst in zip((kk_ref, d_ref, b_ref, k_ref, r_ref), key_refs):
                m = head_rows(src, t)
                m = jnp.concatenate([jnp.concatenate([m, m], axis=0), zpad], axis=1)
                dst[pl.ds(pl.multiple_of(t * dj, dj), dj), :] = m.T[:dj]
            mv = head_rows(v_ref, t)
            mv = jnp.concatenate([jnp.concatenate([mv, jnp.zeros_like(mv)], axis=0), zpad], axis=1)
            zv = mv.T
            v_s[pl.ds(pl.multiple_of(t * di, di), di), :] = zv[:di] + pltpu.roll(zv[di:2 * di], half, axis=1)
            return carry

        lax.fori_loop(0, tb, fill, 0, unroll=8)
        kk_s, d_s, b_s, k_s, r_s = key_refs

        def row(ref, t, j):
            return ref[pl.ds(t * dj + j, 1), :]

        def vrows(t, g):
            return pl.ds(pl.multiple_of(t * di + g * SUBLANES, SUBLANES), SUBLANES)

        load_v = lambda t, g: v_s[vrows(t, g), :]

        def store_y(t, g, val):
            y_s[vrows(t, g), :] = val
    else:
        kk_s, d_s, b_s, k_s, r_s = kk_ref, d_ref, b_ref, k_ref, r_ref

        def row(ref, t, j):
            return ref[0, t, pl.ds(j, 1), :]

        load_v = lambda t, g: v_ref[0, t, pl.ds(g * SUBLANES, SUBLANES), :]

        def store_y(t, g, val):
            y_ref[0, t, pl.ds(g * SUBLANES, SUBLANES), :] = val

    def sl(g):
        return pl.ds(g * SUBLANES, SUBLANES)

    zero = tuple(jnp.zeros((SUBLANES, LANES), F32) for _ in range(ng))

    def first_dot(j, acc):
        kkj = row(kk_s, 0, j)
        return tuple(acc[g] + s_ref[0, j, sl(g), :] * kkj for g in range(ng))

    def step(t, s_kk):
        sa = [-a for a in s_kk]
        v = [load_v(t, g) for g in range(ng)]
        t_next = jnp.minimum(t + 1, tb - 1)

        def update(j, carry):
            yacc, nacc = carry
            dj_, bj, kj, rj = row(d_s, t, j), row(b_s, t, j), row(k_s, t, j), row(r_s, t, j)
            kkn = row(kk_s, t_next, j)
            y_out, n_out = [], []
            for g in range(ng):
                s = s_ref[0, j, sl(g), :] * dj_ + sa[g] * bj + v[g] * kj
                s_ref[0, j, sl(g), :] = s
                y_out.append(yacc[g] + s * rj)
                n_out.append(nacc[g] + s * kkn)
            return tuple(y_out), tuple(n_out)

        yacc, nacc = lax.fori_loop(0, dj, update, (zero, zero), unroll=8)
        for g in range(ng):
            store_y(t, g, yacc[g])
        return nacc

    lax.fori_loop(0, tb, step, lax.fori_loop(0, dj, first_dot, zero, unroll=8))
    if lane_dup:
        zrows = jnp.zeros((LANES - 2 * di, LANES), F32)

        def drain(t, carry):
            ys = y_s[pl.ds(pl.multiple_of(t * di, di), di), :]
            ym = jnp.concatenate([ys, pltpu.roll(ys, half, axis=1), zrows], axis=0)
            y_ref[:, pl.ds(pl.multiple_of(t * heads, heads), heads), :] = ym.T[:half, :half].reshape(n_seq, heads, half)
            return carry

        lax.fori_loop(0, tb, drain, 0, unroll=4)


def dplr_scan(kk, d, b, k, r, v, s0, tb):
    lane_dup = kk.ndim == 3
    dj, di = s0.shape[1], s0.shape[2]
    if lane_dup:
        n_seq, rows, width = kk.shape
        heads = LANES // 2 // n_seq
        g_n, t_len = 1, rows // heads
        assert width == LANES // 2 and n_seq * heads * 2 == LANES and v.shape == kk.shape
        jspec = ispec = pl.BlockSpec((n_seq, tb * heads, width), lambda g, t: (0, t, 0))
        scratch = [pltpu.VMEM((tb * dj, LANES), F32)] * 5 + [pltpu.VMEM((tb * di, LANES), F32)] * 2
    else:
        g_n, t_len = kk.shape[:2]
        assert kk.shape[2:] == (dj, LANES) and v.shape[2:] == (di, LANES)
        jspec = pl.BlockSpec((1, tb, dj, LANES), lambda g, t: (g, t, 0, 0))
        ispec = pl.BlockSpec((1, tb, di, LANES), lambda g, t: (g, t, 0, 0))
        scratch = []
    sspec = pl.BlockSpec((1, dj, di, LANES), lambda g, t: (g, 0, 0, 0))
    return pl.pallas_call(
        functools.partial(_scan_kernel, tb=tb, dj=dj, di=di, lane_dup=lane_dup),
        grid=(g_n, t_len // tb),
        in_specs=[jspec] * 5 + [ispec, sspec],
        out_specs=[ispec, sspec],
        out_shape=[jax.ShapeDtypeStruct(v.shape, F32), jax.ShapeDtypeStruct(s0.shape, F32)],
        scratch_shapes=scratch,
        compiler_params=_cparams(2),
        name="dplr_scan",
    )(kk, d, b, k, r, v, s0)


def _to_key_layout(x, n_seq, t_len, heads, dj, dup):
    x = jnp.broadcast_to(x.reshape(1, n_seq, t_len, heads, dj), (dup, n_seq, t_len, heads, dj))
    x = x.transpose(2, 4, 0, 1, 3)
    g_n = dup * n_seq * heads // LANES
    return x.reshape(t_len, dj, g_n, LANES).transpose(2, 0, 1, 3)


def _to_value_layout(x, n_seq, t_len, heads, dv, dup):
    di = dv // dup
    x = x.reshape(n_seq, t_len, heads, dup, di).transpose(1, 4, 3, 0, 2)
    g_n = dup * n_seq * heads // LANES
    return x.reshape(t_len, di, g_n, LANES).transpose(2, 0, 1, 3)


def _from_value_layout(y, n_seq, t_len, heads, dv, dup):
    g_n, _, di, _ = y.shape
    y = y.transpose(1, 2, 0, 3).reshape(t_len, di, dup, n_seq, heads)
    return y.transpose(3, 0, 4, 2, 1).reshape(n_seq * t_len, heads * dv)


def _state_to_layout(s, dup, value_major):
    bsz, heads = s.shape[:2]
    if value_major:
        dv, dj = s.shape[2:]
        s = s.reshape(bsz, heads, dup, dv // dup, dj).transpose(4, 3, 2, 0, 1)
    else:
        dj, dv = s.shape[2:]
        s = s.reshape(bsz, heads, dj, dup, dv // dup).transpose(2, 4, 3, 0, 1)
    g_n = dup * bsz * heads // LANES
    return s.reshape(dj, dv // dup, g_n, LANES).transpose(2, 0, 1, 3)


def _state_from_layout(s, bsz, heads, dup, value_major):
    g_n, dj, di, _ = s.shape
    s = s.transpose(1, 2, 0, 3).reshape(dj, di, dup, bsz, heads)
    if value_major:
        return s.transpose(3, 4, 2, 1, 0).reshape(bsz, heads, dup * di, dj)
    return s.transpose(3, 4, 0, 2, 1).reshape(bsz, heads, dj, dup * di)


def _swa_kernel(sink_ref, q_ref, kc_ref, vc_ref, kp_ref, vp_ref, o_ref, *, tq, has_cache):
    s_len = WINDOW + tq
    kfull = jnp.concatenate([kp_ref[...], kc_ref[...]], axis=0)
    vfull = jnp.concatenate([vp_ref[...], vc_ref[...]], axis=0)
    lane = lax.broadcasted_iota(jnp.int32, kfull.shape, 1)
    low = lane < SWA_HEAD_DIM
    kswap = pltpu.roll(kfull, SWA_HEAD_DIM, axis=1)
    vswap = pltpu.roll(vfull, SWA_HEAD_DIM, axis=1)
    k_lo = [jnp.where(low, kfull, 0.0), jnp.where(low, kswap, 0.0)]
    k_hi = [jnp.where(low, 0.0, kswap), jnp.where(low, 0.0, kfull)]
    v_lo = [jnp.where(low, vfull, 0.0), jnp.where(low, vswap, 0.0)]
    v_hi = [jnp.where(low, 0.0, vswap), jnp.where(low, 0.0, vfull)]
    t_idx = lax.broadcasted_iota(jnp.int32, (tq, s_len), 0)
    s_idx = lax.broadcasted_iota(jnp.int32, (tq, s_len), 1)
    delta = WINDOW + t_idx - s_idx
    valid = (delta >= 0) & (delta <= WINDOW)
    if not has_cache:
        valid = valid & ((s_idx >= WINDOW) | (pl.program_id(1) > 0))
    contract_last = (((1,), (1,)), ((), ()))
    for hp in range(SWA_HEADS // 2):
        qp = q_ref[:, hp * LANES:(hp + 1) * LANES]
        kv = (2 * hp) // SWA_GROUP
        acc = None
        for half, (kmat, vmat) in enumerate(((k_lo[kv], v_lo[kv]), (k_hi[kv], v_hi[kv]))):
            sink = sink_ref[2 * hp + half]
            s = lax.dot_general(qp, kmat, contract_last, preferred_element_type=F32) * SWA_SCALE
            s = jnp.where(valid, s, -jnp.inf)
            m = jnp.maximum(jnp.max(s, axis=-1, keepdims=True), sink)
            e = jnp.exp(s - m)
            den = jnp.sum(e, axis=-1, keepdims=True) + jnp.exp(sink - m)
            o = _dot(e, vmat) / den
            acc = o if acc is None else acc + o
        o_ref[:, hp * LANES:(hp + 1) * LANES] = acc.astype(o_ref.dtype)


def swa_attention(sinks, p_swa, n_seq, t_len, tq, row0, cache_k=None, cache_v=None, dest=None):
    n_blk = t_len // tq
    has_cache = cache_k is not None
    kcol, vcol = SWA_WIDTH // LANES, SWA_WIDTH // LANES + 1
    q_spec = _row_spec(SWA_WIDTH, tq, n_blk, row0)
    kc_spec = _row_spec(LANES, tq, n_blk, row0, kcol)
    vc_spec = _row_spec(LANES, tq, n_blk, row0, vcol)
    if has_cache:
        kp_spec = vp_spec = pl.BlockSpec((WINDOW, LANES), lambda b, t: (b, 0))
        kp_arr, vp_arr = cache_k, cache_v
    else:
        assert tq == WINDOW and row0 == 0

        def prev_rows(b, t):
            return jnp.maximum(b * n_blk + t - 1, 0)

        kp_spec = pl.BlockSpec((WINDOW, LANES), lambda b, t: (prev_rows(b, t), kcol))
        vp_spec = pl.BlockSpec((WINDOW, LANES), lambda b, t: (prev_rows(b, t), vcol))
        kp_arr = vp_arr = p_swa
    kern, out_shape, extra, extra_specs, aliases = _shared_rows_out(
        functools.partial(_swa_kernel, tq=tq, has_cache=has_cache), 6, n_seq * t_len, SWA_WIDTH, BF16, dest)
    return pl.pallas_call(
        kern,
        grid=(n_seq, n_blk),
        in_specs=[pl.BlockSpec(memory_space=pltpu.SMEM), q_spec, kc_spec, vc_spec, kp_spec, vp_spec] + extra_specs,
        out_specs=_row_spec(SWA_WIDTH, tq, n_blk, 0 if dest is None else row0),
        out_shape=out_shape,
        input_output_aliases=aliases,
        compiler_params=_cparams(2),
        name="swa",
    )(sinks, p_swa, p_swa, p_swa, kp_arr, vp_arr, *extra)


def _gdn_prep_kernel(cur_ref, prev_ref, st_ref, ba_ref, cw_ref, alog_ref, dt_ref, eb_ref, eg_ref, j_ref,
                     kn_o, d_o, b_o, v_o, q_o, *, chunked):
    x = cur_ref[...]
    prev8 = jnp.where(pl.program_id(1) == 0, st_ref[...], prev_ref[...])
    y = x * cw_ref[3:4, :]
    for k in range(1, CONV_WIDTH):
        y = y + _shifted(prev8, x, k) * cw_ref[3 - k:4 - k, :]
    act = y * _sigmoid(y)
    w2 = GDN_WIDTH
    q, k, v = act[:, :w2], act[:, w2:2 * w2], act[:, 2 * w2:]
    j = j_ref[...]
    qn = q * lax.rsqrt(_segsum(q * q, j) + L2_EPS) * (GDN_HEAD_DIM ** -0.5)
    kn = k * lax.rsqrt(_segsum(k * k, j) + L2_EPS)
    ba = ba_ref[...]
    ba = jnp.where(lax.broadcasted_iota(jnp.int32, ba.shape, 1) < 2 * GDN_HEADS, ba, 0.0)
    beta = _dot_lhs2(_sigmoid(ba), eb_ref[...])
    gate = _dot_lhs2(-jnp.exp(alog_ref[...]) * _softplus(ba + dt_ref[...]), eg_ref[...])
    kn_o[...] = kn
    q_o[...] = qn
    if chunked:
        d_o[...] = gate
        b_o[...] = beta
        v_o[...] = v
    else:
        eg = jnp.exp(gate)
        d_o[...] = eg
        b_o[...] = eg * beta * kn
        v_o[...] = beta * v


def gdn_prep(p_gdn, p_swa, st8, prm, n_seq, t_len, tb, row0, chunked):
    n_tb = t_len // tb
    cur, prev, state = _seq_specs(GDN_CONV_DIM, 0, tb, n_tb, row0)
    ba_spec = _row_spec(LANES, tb, n_tb, row0)
    out = jax.ShapeDtypeStruct((n_seq * t_len, GDN_WIDTH), F32)
    ospec = _row_spec(GDN_WIDTH, tb, n_tb)
    return pl.pallas_call(
        functools.partial(_gdn_prep_kernel, chunked=chunked),
        grid=(n_seq, n_tb),
        in_specs=[cur, prev, state, ba_spec, _const_spec((SUBLANES, GDN_CONV_DIM)), _const_spec((1, LANES)),
                  _const_spec((1, LANES)), _const_spec((LANES, GDN_WIDTH)),
                  _const_spec((LANES, GDN_WIDTH)), _const_spec((LANES, LANES))],
        out_specs=[ospec] * 5,
        out_shape=[out] * 5,
        compiler_params=_cparams(2),
        name="gdn_prep",
    )(p_gdn, p_gdn, st8, p_swa, prm["conv_w"], prm["a_log"], prm["dt"], prm["e_beta"], prm["e_gate"], prm["j128"])


def _gdn_post_kernel(o_ref_in, z_ref, w_ref, j_ref, o_ref):
    o = o_ref_in[...]
    z = z_ref[...]
    ms = _segsum(o * o, j_ref[...]) * (1.0 / GDN_HEAD_DIM)
    y = o * lax.rsqrt(ms + NORM_EPS) * w_ref[...]
    o_ref[...] = (y * (z * _sigmoid(z))).astype(o_ref.dtype)


def gdn_post(o, p_gdn, prm, tb, row0, dest):
    rows = o.shape[0]
    spec = pl.BlockSpec((tb, GDN_WIDTH), lambda i: (i, 0))
    zspec = pl.BlockSpec((tb, GDN_WIDTH), lambda i: (row0 // tb + i, GDN_CONV_DIM // GDN_WIDTH))
    kern, out_shape, extra, extra_specs, aliases = _shared_rows_out(_gdn_post_kernel, 4, rows, GDN_WIDTH, BF16, dest)
    return pl.pallas_call(
        kern,
        grid=(rows // tb,),
        in_specs=[spec, zspec, pl.BlockSpec((1, GDN_WIDTH), lambda i: (0, 0)),
                  pl.BlockSpec((LANES, LANES), lambda i: (0, 0))] + extra_specs,
        out_specs=pl.BlockSpec((tb, GDN_WIDTH), lambda i: ((0 if dest is None else row0 // tb) + i, 0)),
        out_shape=out_shape,
        input_output_aliases=aliases,
        compiler_params=_cparams(1),
        name="gdn_post",
    )(o, p_gdn, prm["norm_w"], prm["j128"], *extra)


def _split3(x):
    hi = x.astype(BF16)
    r = x - hi.astype(F32)
    mid = r.astype(BF16)
    return hi, mid, (r - mid.astype(F32)).astype(BF16)


def _dotb(a, b):
    return _dot(a.astype(BF16), b.astype(BF16))


def _dot3_packed(x, w, c):
    xh = x.astype(BF16).astype(F32)
    hi_lo = xh + pltpu.roll(x - xh, c, axis=1)
    wh, wl = _split(w)
    lhs = jnp.concatenate([hi_lo, xh], axis=1).astype(BF16)
    rhs = jnp.concatenate([wh, wh, wl, jnp.zeros_like(wh)], axis=0)
    return _dot(lhs, rhs)


def _pad_rows(x, rows):
    return jnp.concatenate([x, jnp.zeros((rows - x.shape[0], x.shape[1]), x.dtype)], axis=0)


def _gdn_chunk_kernel(q_ref, k_ref, v_ref, beta_ref, g_ref, s0_ref, o_ref, s_ref, *, hb, nc):
    c = GDN_CHUNK

    @pl.when(pl.program_id(2) == 0)
    def _():
        s_ref[...] = s0_ref[...]

    row = lax.broadcasted_iota(jnp.int32, (c, LANES), 0)
    col = lax.broadcasted_iota(jnp.int32, (c, LANES), 1)
    incl, strict = row >= col, row > col
    eye = (row == col).astype(F32)
    row3 = lax.broadcasted_iota(jnp.int32, (c, 2 * LANES), 0)
    col3 = lax.broadcasted_iota(jnp.int32, (c, 2 * LANES), 1)
    tri3 = ((row3 >= col3 % c) & (col3 < 3 * c)).astype(BF16)
    lane0_3 = (lax.broadcasted_iota(jnp.int32, (c, 3 * LANES), 1) % LANES == 0).astype(BF16)
    zero_c = jnp.zeros((c, LANES), BF16)
    contract_last = (((1,), (1,)), ((), ()))
    contract_first = (((0,), (0,)), ((), ()))

    units = [(h, ci) for ci in range(nc) for h in range(hb)]
    n_u = len(units)

    def load(ref):
        return [ref[ci * c:(ci + 1) * c, h * LANES:(h + 1) * LANES] for h, ci in units]

    q, k, v, beta, g = load(q_ref), load(k_ref), load(v_ref), load(beta_ref), load(g_ref)
    lanes = lambda x, u: x[:, u * LANES:(u + 1) * LANES]
    g_parts = jnp.concatenate([jnp.concatenate(list(_split3(x)) + [zero_c], axis=0) for x in g], axis=1)
    gc_all = _dot(tri3, g_parts)
    gc = [lanes(gc_all, u) for u in range(n_u)]
    gc_parts = jnp.concatenate([_pad_rows(jnp.concatenate(_split3(x), axis=1), LANES) for x in gc], axis=0)
    gc_row_all = lax.dot_general(lane0_3, gc_parts, contract_last, preferred_element_type=F32)
    dec = [jnp.where(incl, jnp.exp(jnp.where(incl, gc[u] - lanes(gc_row_all, u), 0.0)), 0.0) for u in range(n_u)]
    kb = [k[u] * beta[u] for u in range(n_u)]
    gram = [lax.dot_general(jnp.concatenate([kb[u], q[u]], axis=0).astype(BF16), _pad_rows(k[u].astype(BF16), LANES),
                            contract_last, preferred_element_type=F32) for u in range(n_u)]
    a = [jnp.where(strict, gram[u][:c] * dec[u], 0.0) for u in range(n_u)]
    attn = [jnp.where(incl, gram[u][c:] * dec[u], 0.0)[:, :c].astype(BF16) for u in range(n_u)]
    tinv = [eye - x for x in a]
    p = [_dot3_packed(x, x, c) for x in a]
    n_levels = c.bit_length() - 2
    for lvl in range(n_levels):
        last = lvl + 1 == n_levels
        lhs = tinv if last else [jnp.concatenate([tinv[u], p[u]], axis=0) for u in range(n_u)]
        prod = [_dot3_packed(lhs[u], p[u], c) for u in range(n_u)]
        tinv = [tinv[u] + prod[u][:c] for u in range(n_u)]
        if not last:
            p = [prod[u][c:] for u in range(n_u)]
    eg = [jnp.exp(x) for x in gc]
    uw = [_dotb(tinv[u][:, :c], jnp.concatenate([v[u] * beta[u], kb[u] * eg[u]], axis=1)) for u in range(n_u)]
    w_qe = [jnp.concatenate([uw[u][:, LANES:], q[u] * eg[u]], axis=0).astype(BF16) for u in range(n_u)]
    g_last = [x[c - 1:c, :] for x in gc]
    kd = [(k[u] * jnp.exp(g_last[u] - gc[u])).astype(BF16) for u in range(n_u)]
    eg_last = [jnp.exp(x) for x in g_last]

    s = [s_ref[0, h] for h in range(hb)]
    o_rows = []
    for ci in range(nc):
        us = [ci * hb + h for h in range(hb)]
        wq = [_dot(w_qe[u], s[h].astype(BF16)) for h, u in enumerate(us)]
        v_new = [(uw[u][:, :LANES] - wq[h][:c]).astype(BF16) for h, u in enumerate(us)]
        o_rows.append(jnp.concatenate([wq[h][c:] + _dot(attn[u], v_new[h]) for h, u in enumerate(us)], axis=1))
        s = [s[h] * eg_last[u] + lax.dot_general(kd[u], v_new[h], contract_first, preferred_element_type=F32)
             for h, u in enumerate(us)]
    o_ref[...] = jnp.concatenate(o_rows, axis=0)
    s_ref[0] = jnp.stack(s)


def gdn_chunked(q, k, v, beta, g, s0, n_seq, t_len, hb, nc):
    n_ch = t_len // (GDN_CHUNK * nc)
    spec = pl.BlockSpec((GDN_CHUNK * nc, hb * LANES), lambda b, h, c: (b * n_ch + c, h))
    sspec = pl.BlockSpec((1, hb, GDN_HEAD_DIM, GDN_HEAD_DIM), lambda b, h, c: (b, h, 0, 0))
    return pl.pallas_call(
        functools.partial(_gdn_chunk_kernel, hb=hb, nc=nc),
        grid=(n_seq, GDN_HEADS // hb, n_ch),
        in_specs=[spec] * 5 + [sspec],
        out_specs=[spec, sspec],
        out_shape=[jax.ShapeDtypeStruct(q.shape, F32), jax.ShapeDtypeStruct(s0.shape, F32)],
        compiler_params=_cparams(3),
        name="gdn_chunk",
    )(q, k, v, beta, g, s0)


def rwkv_mix(p_rwkv, shift_state, wkv_state, prm, n_seq, t_len, row0, tb_prep, tb_scan, dest=None):
    st8 = _pad_state_rows(shift_state[:, None, :])
    r, kk, d, b, k, v, g = rwkv_prep(p_rwkv, st8, prm, n_seq, t_len, tb_prep, row0)
    dup = max(1, LANES // (n_seq * RWKV_HEADS))
    s0 = _state_to_layout(wkv_state, dup, value_major=True)
    if dup == 2:
        tl = lambda x: x.reshape(n_seq, t_len * RWKV_HEADS, RWKV_HEAD_DIM)
        y, s1 = dplr_scan(tl(kk), tl(d), tl(b), tl(k), tl(r), tl(v), s0, tb_scan)
        y = y.reshape(n_seq * t_len, RWKV_WIDTH)
    else:
        kl = functools.partial(_to_key_layout, n_seq=n_seq, t_len=t_len, heads=RWKV_HEADS, dj=RWKV_HEAD_DIM, dup=dup)
        vt = _to_value_layout(v, n_seq, t_len, RWKV_HEADS, RWKV_HEAD_DIM, dup)
        y, s1 = dplr_scan(kl(kk), kl(d), kl(b), kl(k), kl(r), vt, s0, tb_scan)
        y = _from_value_layout(y, n_seq, t_len, RWKV_HEADS, RWKV_HEAD_DIM, dup)
    out = rwkv_post(y, r, k, v, g, prm, tb_prep, row0, dest)
    return out, _state_from_layout(s1, n_seq, RWKV_HEADS, dup, value_major=True)


def gdn_mix(p_gdn, p_swa, conv_state, ssm_state, prm, n_seq, t_len, row0, tb_prep, tb_scan, dest=None):
    st8 = _pad_state_rows(conv_state)
    if t_len % GDN_CHUNK == 0:
        kn, g, beta, v, q = gdn_prep(p_gdn, p_swa, st8, prm, n_seq, t_len, tb_prep, row0, chunked=True)
        nc = 4 if t_len % (4 * GDN_CHUNK) == 0 else 1
        o, s1 = gdn_chunked(q, kn, v, beta, g, ssm_state, n_seq, t_len, hb=4, nc=nc)
        return gdn_post(o, p_gdn, prm, tb_prep, row0, dest), s1
    kn, d, b, v, q = gdn_prep(p_gdn, p_swa, st8, prm, n_seq, t_len, tb_prep, row0, chunked=False)
    dup = 2 * max(1, LANES // (2 * n_seq * GDN_HEADS))
    kl = functools.partial(_to_key_layout, n_seq=n_seq, t_len=t_len, heads=GDN_HEADS, dj=GDN_HEAD_DIM, dup=dup)
    vt = _to_value_layout(v, n_seq, t_len, GDN_HEADS, GDN_HEAD_DIM, dup)
    s0 = _state_to_layout(ssm_state, dup, value_major=False)
    knl = kl(kn)
    o, s1 = dplr_scan(knl, kl(d), kl(b), knl, kl(q), vt, s0, tb_scan)
    o = _from_value_layout(o, n_seq, t_len, GDN_HEADS, GDN_HEAD_DIM, dup)
    out = gdn_post(o, p_gdn, prm, tb_prep, row0, dest)
    return out, _state_from_layout(s1, n_seq, GDN_HEADS, dup, value_major=False)


def _block_ones(seg):
    i = jnp.arange(LANES)
    return (i[:, None] // seg == i[None, :] // seg).astype(BF16)


def _head_expander(first_row):
    rows = jnp.arange(LANES)[:, None]
    cols = jnp.arange(GDN_WIDTH)[None, :] // GDN_HEAD_DIM
    return (rows == cols + first_row).astype(BF16)


def _layer_params(l, W):
    row = lambda x: x.reshape(1, -1)
    zeros = jnp.zeros((DECAY_LORA, RWKV_WIDTH), F32)
    wwa = jnp.concatenate([jnp.concatenate([W["rwkv_w2"][l], zeros], axis=1),
                           jnp.concatenate([zeros, W["rwkv_a2"][l]], axis=1)], axis=0)
    lane_row = lambda x, off: jnp.pad(x, (off, LANES - off - x.shape[0])).reshape(1, LANES)
    rwkv = dict(mix=row(W["rwkv_shift_mix"][l]), w0=row(W["rwkv_w0"][l]), a0=row(W["rwkv_a0"][l]), wwa=wwa,
                g2=W["rwkv_g2"][l], k_k=row(W["rwkv_k_k"][l]), k_a=row(W["rwkv_k_a"][l]),
                r_k=row(W["rwkv_r_k"][l]), ln_w=row(W["rwkv_ln_w"][l]), ln_b=row(W["rwkv_ln_b"][l]),
                j64=_block_ones(RWKV_HEAD_DIM))
    gdn = dict(conv_w=jnp.pad(W["gdn_conv_w"][l], ((0, SUBLANES - CONV_WIDTH), (0, 0))),
               a_log=lane_row(W["gdn_A_log"][l], GDN_HEADS), dt=lane_row(W["gdn_dt_bias"][l], GDN_HEADS),
               e_beta=_head_expander(0), e_gate=_head_expander(GDN_HEADS),
               norm_w=jnp.tile(W["gdn_norm_w"][l], GDN_HEADS).reshape(1, GDN_WIDTH), j128=_block_ones(LANES))
    return rwkv, gdn


def kernel(x_prompt, x_sample, state_rwkv_shift, state_rwkv_wkv, cache_swa_k, cache_swa_v, state_gdn_conv,
           state_gdn_ssm, norm_mix, w_in, rwkv_shift_mix, rwkv_w0, rwkv_w2, rwkv_a0, rwkv_a2, rwkv_g2, rwkv_k_k,
           rwkv_k_a, rwkv_r_k, rwkv_ln_w, rwkv_ln_b, swa_sinks, gdn_conv_w, gdn_A_log, gdn_dt_bias, gdn_norm_w,
           w_out, norm_ffn, w_gate, w_up, w_down, final_norm):
    W = dict(rwkv_shift_mix=rwkv_shift_mix, rwkv_w0=rwkv_w0, rwkv_w2=rwkv_w2, rwkv_a0=rwkv_a0, rwkv_a2=rwkv_a2,
             rwkv_g2=rwkv_g2, rwkv_k_k=rwkv_k_k, rwkv_k_a=rwkv_k_a, rwkv_r_k=rwkv_r_k, rwkv_ln_w=rwkv_ln_w,
             rwkv_ln_b=rwkv_ln_b, gdn_conv_w=gdn_conv_w, gdn_A_log=gdn_A_log, gdn_dt_bias=gdn_dt_bias,
             gdn_norm_w=gdn_norm_w)
    w_in_b, w_out_b, w_gate_b, w_up_b, w_down_b = (w.astype(BF16) for w in (w_in, w_out, w_gate, w_up, w_down))
    swa_col, gdn_col, ba_col = RWKV_PROJ, RWKV_PROJ + SWA_PROJ, RWKV_PROJ + SWA_PROJ + GDN_MAIN

    u, h = rmsnorm_join(x_prompt.reshape(ROWS_P, D_MODEL), x_sample.reshape(ROWS_D, D_MODEL), norm_mix[0], BF16)
    zero_shift = jnp.zeros((BATCH, RWKV_PROJ), F32)
    zero_wkv = jnp.zeros((BATCH, RWKV_HEADS, RWKV_HEAD_DIM, RWKV_HEAD_DIM), F32)
    zero_conv = jnp.zeros((BATCH, CONV_WIDTH - 1, GDN_CONV_DIM), F32)
    zero_ssm = jnp.zeros((BATCH, GDN_HEADS, GDN_HEAD_DIM, GDN_HEAD_DIM), F32)
    wb = cache_swa_k.shape[2]
    assert wb == WINDOW
    outs = [[] for _ in range(12)]
    bm = 1056
    for l in range(DEPTH):
        rw, gd = _layer_params(l, W)
        if l > 0:
            u = rmsnorm(h, norm_mix[l], BF16)
        p_rwkv = matmul(u, w_in_b, l, bm, 256, 0, RWKV_PROJ)
        p_swa = matmul(u, w_in_b, l, bm, 256, swa_col, SWA_PROJ)
        p_gdn = matmul(u, w_in_b, l, bm, 512, gdn_col, GDN_MAIN)
        p_ba = matmul(u, w_in_b, l, bm, LANES, ba_col, LANES)

        o_r, wkv_p = rwkv_mix(p_rwkv, zero_shift, zero_wkv, rw, BATCH, SEQ, 0, 128, 32, dest=ROWS)
        o_r, wkv_d = rwkv_mix(p_rwkv, state_rwkv_shift[l], state_rwkv_wkv[l], rw, DEC_BATCH, DEC_SEQ, ROWS_P,
                              DEC_SEQ, DEC_SEQ, dest=o_r)
        o_s = swa_attention(swa_sinks[l], p_swa, BATCH, SEQ, WINDOW, 0, dest=ROWS)
        o_s = swa_attention(swa_sinks[l], p_swa, DEC_BATCH, DEC_SEQ, DEC_SEQ, ROWS_P,
                            cache_swa_k[l].reshape(DEC_BATCH * wb, SWA_KV_WIDTH),
                            cache_swa_v[l].reshape(DEC_BATCH * wb, SWA_KV_WIDTH), dest=o_s)
        o_g, ssm_p = gdn_mix(p_gdn, p_ba, zero_conv, zero_ssm, gd, BATCH, SEQ, 0, 64, 32, dest=ROWS)
        o_g, ssm_d = gdn_mix(p_gdn, p_ba, state_gdn_conv[l], state_gdn_ssm[l], gd, DEC_BATCH, DEC_SEQ, ROWS_P,
                             DEC_SEQ, DEC_SEQ, dest=o_g)
        h = matmul_residual([o_r, o_s, o_g], w_out_b, l, h, bm, 512, [RWKV_WIDTH, SWA_WIDTH, GDN_WIDTH])
        u = rmsnorm(h, norm_ffn[l], BF16)
        h1 = swiglu_matmul(u, w_gate_b, w_up_b, l, bm, 512)
        h = matmul_residual([h1], w_down_b, l, h, bm, 512, [D_FF // 2])

        def tail_p(p, n_rows, c0, c1):
            return jnp.stack([lax.slice(p, ((b + 1) * SEQ - n_rows, c0), ((b + 1) * SEQ, c1)) for b in range(BATCH)])

        def tail_d(p, n_rows, c0, c1):
            x = lax.slice(p, (ROWS_P, c0), (ROWS, c1)).reshape(DEC_BATCH, DEC_SEQ, c1 - c0)
            return x[:, DEC_SEQ - n_rows:]

        kv = lambda x: x.reshape(x.shape[0], x.shape[1], SWA_KV_HEADS, SWA_HEAD_DIM)
        k0, k1, v1 = SWA_WIDTH, SWA_WIDTH + SWA_KV_WIDTH, SWA_PROJ
        layer_out = (
            tail_p(p_rwkv, 1, 0, RWKV_PROJ)[:, 0], wkv_p, kv(tail_p(p_swa, wb, k0, k1)), kv(tail_p(p_swa, wb, k1, v1)),
            tail_p(p_gdn, CONV_WIDTH - 1, 0, GDN_CONV_DIM), ssm_p,
            tail_d(p_rwkv, 1, 0, RWKV_PROJ)[:, 0], wkv_d,
            jnp.concatenate([cache_swa_k[l], kv(tail_d(p_swa, DEC_SEQ, k0, k1))], axis=1)[:, -wb:],
            jnp.concatenate([cache_swa_v[l], kv(tail_d(p_swa, DEC_SEQ, k1, v1))], axis=1)[:, -wb:],
            tail_d(p_gdn, CONV_WIDTH - 1, 0, GDN_CONV_DIM), ssm_d)
        for lst, t in zip(outs, layer_out):
            lst.append(t)
    y = rmsnorm(h, final_norm, F32)
    y_prompt = y[:ROWS_P].reshape(BATCH, SEQ, D_MODEL)
    y_sample = y[ROWS_P:].reshape(DEC_BATCH, DEC_SEQ, D_MODEL)
    return (y_prompt, y_sample) + tuple(jnp.stack(lst) for lst in outs)
```

```python
import functools

import jax
import jax.numpy as jnp
from jax import lax
from jax.experimental import pallas as pl
from jax.experimental.pallas import tpu as pltpu

D_MODEL = 4096
BATCH = 4
SEQ = 2048
DEPTH = 4
DEC_BATCH = 32
DEC_SEQ = 8
NORM_EPS = 1e-6
L2_EPS = 1e-6
RWKV_WIDTH = 1024
RWKV_HEAD_DIM = 64
RWKV_HEADS = 16
DECAY_LORA = 64
AAA_LORA = 64
GATE_LORA = 128
RWKV_PROJ = 3 * RWKV_WIDTH + DECAY_LORA + AAA_LORA + GATE_LORA
RWKV_LN_EPS = 64e-5
SWA_WIDTH = 1024
SWA_HEAD_DIM = 64
SWA_HEADS = 16
SWA_KV_HEADS = 2
SWA_GROUP = 8
SWA_KV_WIDTH = 128
SWA_PROJ = SWA_WIDTH + 2 * SWA_KV_WIDTH
WINDOW = 128
SWA_SCALE = SWA_HEAD_DIM ** -0.5
GDN_WIDTH = 2048
GDN_HEAD_DIM = 128
GDN_HEADS = 16
GDN_CONV_DIM = 3 * GDN_WIDTH
CONV_WIDTH = 4
GDN_CHUNK = 64
GDN_MAIN = GDN_CONV_DIM + GDN_WIDTH
D_FF = 11008

ROWS_P = BATCH * SEQ
ROWS_D = DEC_BATCH * DEC_SEQ
ROWS = ROWS_P + ROWS_D

LANES = 128
SUBLANES = 8
VMEM_LIMIT = 56 * 1024 * 1024

F32 = jnp.float32
BF16 = jnp.bfloat16


def _cparams(n_axes):
    return pltpu.CompilerParams(dimension_semantics=("arbitrary",) * n_axes, vmem_limit_bytes=VMEM_LIMIT)


def _split(x):
    hi = x.astype(BF16)
    lo = (x - hi.astype(F32)).astype(BF16)
    return hi, lo


def _dot(a, b):
    return jnp.dot(a, b, preferred_element_type=F32)


def _dot_lhs2(x, m_bf16):
    hi, lo = _split(x)
    return _dot(hi, m_bf16) + _dot(lo, m_bf16)


def _dot3(x, w):
    xh, xl = _split(x)
    wh, wl = _split(w)
    return _dot(xh, wh) + (_dot(xl, wh) + _dot(xh, wl))


def _segsum(x, j_bf16):
    n = x.shape[-1] // LANES
    return jnp.concatenate([_dot_lhs2(x[:, c * LANES:(c + 1) * LANES], j_bf16) for c in range(n)], axis=-1)


def _sigmoid(x):
    return 1.0 / (1.0 + jnp.exp(-x))


def _softplus(x):
    return jnp.maximum(x, 0.0) + jnp.log(1.0 + jnp.exp(-jnp.abs(x)))


def _rmsnorm_kernel(x_ref, w_ref, o_ref):
    x = x_ref[...]
    ms = jnp.mean(x * x, axis=-1, keepdims=True)
    o_ref[...] = (x * lax.rsqrt(ms + NORM_EPS) * w_ref[...]).astype(o_ref.dtype)


def rmsnorm(x, w, out_dtype, br=256):
    r, d = x.shape
    return pl.pallas_call(
        _rmsnorm_kernel,
        grid=(r // br,),
        in_specs=[pl.BlockSpec((br, d), lambda i: (i, 0)), pl.BlockSpec((1, d), lambda i: (0, 0))],
        out_specs=pl.BlockSpec((br, d), lambda i: (i, 0)),
        out_shape=jax.ShapeDtypeStruct((r, d), out_dtype),
        compiler_params=_cparams(1),
        name="rmsnorm",
    )(x, w.reshape(1, d))


def _rmsnorm_join_kernel(xp_ref, xs_ref, w_ref, o_ref, h_ref, *, n_p):
    x = jnp.where(pl.program_id(0) < n_p, xp_ref[...], xs_ref[...])
    h_ref[...] = x
    ms = jnp.mean(x * x, axis=-1, keepdims=True)
    o_ref[...] = (x * lax.rsqrt(ms + NORM_EPS) * w_ref[...]).astype(o_ref.dtype)


def rmsnorm_join(xp, xs, w, out_dtype):
    (rp, d), rs = xp.shape, xs.shape[0]
    br = rs
    assert rp % br == 0
    n_p = rp // br
    spec = pl.BlockSpec((br, d), lambda i: (i, 0))
    return pl.pallas_call(
        functools.partial(_rmsnorm_join_kernel, n_p=n_p),
        grid=(n_p + 1,),
        in_specs=[pl.BlockSpec((br, d), lambda i: (jnp.minimum(i, n_p - 1), 0)),
                  pl.BlockSpec((br, d), lambda i: (0, 0)), pl.BlockSpec((1, d), lambda i: (0, 0))],
        out_specs=[spec, spec],
        out_shape=[jax.ShapeDtypeStruct((rp + rs, d), out_dtype), jax.ShapeDtypeStruct((rp + rs, d), xp.dtype)],
        compiler_params=_cparams(1),
        name="rmsnorm_join",
    )(xp, xs, w.reshape(1, d))


def _mm_kernel(a_ref, b_ref, o_ref):
    o_ref[...] = _dot(a_ref[...], b_ref[...]).astype(o_ref.dtype)


def matmul(a, w, layer, bm, bn, col0, n_out, out_dtype=F32):
    r, k = a.shape
    assert col0 % bn == 0 and n_out % bn == 0
    return pl.pallas_call(
        _mm_kernel,
        grid=(r // bm, n_out // bn),
        in_specs=[pl.BlockSpec((bm, k), lambda i, j: (i, 0)),
                  pl.BlockSpec((None, k, bn), lambda i, j: (layer, 0, col0 // bn + j))],
        out_specs=pl.BlockSpec((bm, bn), lambda i, j: (i, j)),
        out_shape=jax.ShapeDtypeStruct((r, n_out), out_dtype),
        compiler_params=_cparams(2),
        name="proj_in",
    )(a, w)


def _swiglu_kernel(a_ref, g_ref, u_ref, o_ref):
    a = a_ref[...]
    g = _dot(a, g_ref[...])
    u = _dot(a, u_ref[...])
    o_ref[...] = (g * _sigmoid(g) * u).astype(o_ref.dtype)


def swiglu_matmul(a, wg, wu, layer, bm, bn):
    r, k = a.shape
    n = wg.shape[2]
    wspec = pl.BlockSpec((None, k, bn), lambda i, j: (layer, 0, j))
    return pl.pallas_call(
        _swiglu_kernel,
        grid=(r // bm, pl.cdiv(n, bn)),
        in_specs=[pl.BlockSpec((bm, k), lambda i, j: (i, 0)), wspec, wspec],
        out_specs=pl.BlockSpec((bm, bn), lambda i, j: (i, j)),
        out_shape=jax.ShapeDtypeStruct((r, n), BF16),
        compiler_params=_cparams(2),
        name="ffn_swiglu",
    )(a, wg, wu)


def _mm_res_kernel(*refs, n_a):
    a_refs, b_refs, res_ref, o_ref = refs[:n_a], refs[n_a:2 * n_a], refs[2 * n_a], refs[2 * n_a + 1]
    kk = pl.program_id(2)

    @pl.when(kk == 0)
    def _():
        o_ref[...] = res_ref[...]

    acc = _dot(a_refs[0][...], b_refs[0][...])
    for a_ref, b_ref in zip(a_refs[1:], b_refs[1:]):
        acc = acc + _dot(a_ref[...], b_ref[...])
    o_ref[...] += acc


def matmul_residual(a_list, w, layer, res, bm, bn, bk_list):
    r = res.shape[0]
    n = w.shape[2]
    nk = a_list[0].shape[1] // bk_list[0]
    in_specs, row0 = [], 0
    for a, bk in zip(a_list, bk_list):
        assert a.shape[1] == nk * bk
        in_specs.append(pl.BlockSpec((bm, bk), lambda i, j, k: (i, k)))
    for a, bk in zip(a_list, bk_list):
        assert row0 % bk == 0
        in_specs.append(pl.BlockSpec((None, bk, bn), functools.partial(
            lambda i, j, k, off: (layer, off + k, j), off=row0 // bk)))
        row0 += a.shape[1]
    in_specs.append(pl.BlockSpec((bm, bn), lambda i, j, k: (i, j)))
    return pl.pallas_call(
        functools.partial(_mm_res_kernel, n_a=len(a_list)),
        grid=(r // bm, n // bn, nk),
        in_specs=in_specs,
        out_specs=pl.BlockSpec((bm, bn), lambda i, j, k: (i, j)),
        out_shape=jax.ShapeDtypeStruct((r, n), F32),
        compiler_params=_cparams(3),
        name="proj_residual",
    )(*a_list, *([w] * len(a_list)), res)


def _shifted(prev8, x, k):
    ext = jnp.concatenate([prev8, x], axis=0)
    return pltpu.roll(ext, k, axis=0)[SUBLANES:]


def _seq_specs(width, col_block, tb, n_tb, row0):
    assert row0 % tb == 0 and tb % SUBLANES == 0
    cur = pl.BlockSpec((tb, width), lambda b, t: (row0 // tb + b * n_tb + t, col_block))
    prev = pl.BlockSpec((SUBLANES, width), lambda b, t: (
        jnp.maximum((row0 + (b * n_tb + t) * tb) // SUBLANES - 1, 0), col_block))
    state = pl.BlockSpec((SUBLANES, width), lambda b, t: (b, 0))
    return cur, prev, state


def _row_spec(width, tb, n_tb, row0=0, col_block=0):
    return pl.BlockSpec((tb, width), lambda b, t: (row0 // tb + b * n_tb + t, col_block))


def _const_spec(shape):
    return pl.BlockSpec(shape, lambda b, t: (0,) * len(shape))


def _drop_ref(kernel_fn, pos, *refs):
    return kernel_fn(*refs[:pos], *refs[pos + 1:])


def _shared_rows_out(kernel_fn, n_in, rows, width, dtype, dest):
    if dest is None or isinstance(dest, int):
        return kernel_fn, jax.ShapeDtypeStruct((dest or rows, width), dtype), (), [], {}
    assert dest.shape[1] == width and dest.dtype == dtype
    return (functools.partial(_drop_ref, kernel_fn, n_in), jax.ShapeDtypeStruct(dest.shape, dtype), (dest,),
            [pl.BlockSpec(memory_space=pl.ANY)], {n_in: 0})


def _pad_state_rows(st):
    b, k, c = st.shape
    return jnp.pad(st, ((0, 0), (SUBLANES - k, 0), (0, 0))).reshape(b * SUBLANES, c)


def _rwkv_prep_kernel(cur_ref, prev_ref, st_ref, mix_ref, w0_ref, a0_ref, wwa_ref, g2_ref, kkw_ref, kaw_ref, j_ref,
                      r_o, kk_o, d_o, b_o, k_o, v_o, g_o):
    x = cur_ref[...]
    prev8 = jnp.where(pl.program_id(1) == 0, st_ref[...], prev_ref[...])
    xs = x + (_shifted(prev8, x, 1) - x) * mix_ref[...]
    w3 = RWKV_WIDTH
    r, k, v = xs[:, 0:w3], xs[:, w3:2 * w3], xs[:, 2 * w3:3 * w3]
    wa = xs[:, 3 * w3:3 * w3 + LANES]
    glo = xs[:, 3 * w3 + LANES:3 * w3 + 2 * LANES]
    lane = lax.broadcasted_iota(jnp.int32, wa.shape, 1)
    lora = _dot3(jnp.where(lane < DECAY_LORA, jnp.tanh(wa), wa), wwa_ref[...])
    w = -_softplus(-(w0_ref[...] + lora[:, :w3])) - 0.5
    d = jnp.exp(-jnp.exp(w))
    a = _sigmoid(a0_ref[...] + lora[:, w3:])
    g = _dot3(_sigmoid(glo), g2_ref[...])
    kn = k * kkw_ref[...]
    kk = kn * lax.rsqrt(_segsum(kn * kn, j_ref[...]) + L2_EPS)
    r_o[...] = r
    kk_o[...] = kk
    d_o[...] = d
    b_o[...] = kk * a
    k_o[...] = k * (1.0 + (a - 1.0) * kaw_ref[...])
    v_o[...] = v
    g_o[...] = g


def rwkv_prep(p_rwkv, st8, prm, n_seq, t_len, tb, row0):
    n_tb = t_len // tb
    cur, prev, state = _seq_specs(RWKV_PROJ, 0, tb, n_tb, row0)
    out = jax.ShapeDtypeStruct((n_seq * t_len, RWKV_WIDTH), F32)
    ospec = _row_spec(RWKV_WIDTH, tb, n_tb)
    return pl.pallas_call(
        _rwkv_prep_kernel,
        grid=(n_seq, n_tb),
        in_specs=[cur, prev, state, _const_spec((1, RWKV_PROJ)), _const_spec((1, RWKV_WIDTH)),
                  _const_spec((1, RWKV_WIDTH)), _const_spec((LANES, 2 * RWKV_WIDTH)),
                  _const_spec((GATE_LORA, RWKV_WIDTH)), _const_spec((1, RWKV_WIDTH)), _const_spec((1, RWKV_WIDTH)),
                  _const_spec((LANES, LANES))],
        out_specs=[ospec] * 7,
        out_shape=[out] * 7,
        compiler_params=_cparams(2),
        name="rwkv_prep",
    )(p_rwkv, p_rwkv, st8, prm["mix"], prm["w0"], prm["a0"], prm["wwa"], prm["g2"], prm["k_k"], prm["k_a"],
      prm["j64"])


def _rwkv_post_kernel(y_ref, r_ref, k_ref, v_ref, g_ref, lnw_ref, lnb_ref, rk_ref, j_ref, o_ref):
    j = j_ref[...]
    y = y_ref[...]
    inv_n = 1.0 / RWKV_HEAD_DIM
    yc = y - _segsum(y, j) * inv_n
    var = _segsum(yc * yc, j) * inv_n
    out = yc * lax.rsqrt(var + RWKV_LN_EPS) * lnw_ref[...] + lnb_ref[...]
    out = out + _segsum(r_ref[...] * k_ref[...] * rk_ref[...], j) * v_ref[...]
    o_ref[...] = (out * g_ref[...]).astype(o_ref.dtype)


def rwkv_post(y, r, k, v, g, prm, tb, row0, dest):
    rows = y.shape[0]
    spec = pl.BlockSpec((tb, RWKV_WIDTH), lambda i: (i, 0))
    cspec = pl.BlockSpec((1, RWKV_WIDTH), lambda i: (0, 0))
    kern, out_shape, extra, extra_specs, aliases = _shared_rows_out(_rwkv_post_kernel, 9, rows, RWKV_WIDTH, BF16, dest)
    return pl.pallas_call(
        kern,
        grid=(rows // tb,),
        in_specs=[spec] * 5 + [cspec] * 3 + [pl.BlockSpec((LANES, LANES), lambda i: (0, 0))] + extra_specs,
        out_specs=pl.BlockSpec((tb, RWKV_WIDTH), lambda i: ((0 if dest is None else row0 // tb) + i, 0)),
        out_shape=out_shape,
        input_output_aliases=aliases,
        compiler_params=_cparams(1),
        name="rwkv_post",
    )(y, r, k, v, g, prm["ln_w"], prm["ln_b"], prm["r_k"], prm["j64"], *extra)


def _scan_kernel(kk_ref, d_ref, b_ref, k_ref, r_ref, v_ref, s0_ref, y_ref, s_ref, *scratch, tb, dj, di, lane_dup):
    ng = di // SUBLANES
    half = LANES // 2

    @pl.when(pl.program_id(1) == 0)
    def _():
        s_ref[...] = s0_ref[...]

    if lane_dup:
        key_refs, v_s, y_s = scratch[:5], scratch[5], scratch[6]
        n_seq = kk_ref.shape[0]
        heads = half // n_seq
        assert dj == half and 2 * di == half
        zpad = jnp.zeros((LANES, half), F32)

        def head_rows(ref, t):
            return ref[:, pl.ds(pl.multiple_of(t * heads, heads), heads), :].reshape(half, half)

        def fill(t, carry):
            for src, dst in zip((kk_ref, d_ref, b_ref, k_ref, r_ref), key_refs):
                m = head_rows(src, t)
                m = jnp.concatenate([jnp.concatenate([m, m], axis=0), zpad], axis=1)
                dst[pl.ds(pl.multiple_of(t * dj, dj), dj), :] = m.T[:dj]
            mv = head_rows(v_ref, t)
            mv = jnp.concatenate([jnp.concatenate([mv, jnp.zeros_like(mv)], axis=0), zpad], axis=1)
            zv = mv.T
            v_s[pl.ds(pl.multiple_of(t * di, di), di), :] = zv[:di] + pltpu.roll(zv[di:2 * di], half, axis=1)
            return carry

        lax.fori_loop(0, tb, fill, 0, unroll=8)
        kk_s, d_s, b_s, k_s, r_s = key_refs

        def row(ref, t, j):
            return ref[pl.ds(t * dj + j, 1), :]

        def vrows(t, g):
            return pl.ds(pl.multiple_of(t * di + g * SUBLANES, SUBLANES), SUBLANES)

        load_v = lambda t, g: v_s[vrows(t, g), :]

        def store_y(t, g, val):
            y_s[vrows(t, g), :] = val
    else:
        kk_s, d_s, b_s, k_s, r_s = kk_ref, d_ref, b_ref, k_ref, r_ref

        def row(ref, t, j):
            return ref[0, t, pl.ds(j, 1), :]

        load_v = lambda t, g: v_ref[0, t, pl.ds(g * SUBLANES, SUBLANES), :]

        def store_y(t, g, val):
            y_ref[0, t, pl.ds(g * SUBLANES, SUBLANES), :] = val

    def sl(g):
        return pl.ds(g * SUBLANES, SUBLANES)

    zero = tuple(jnp.zeros((SUBLANES, LANES), F32) for _ in range(ng))

    def first_dot(j, acc):
        kkj = row(kk_s, 0, j)
        return tuple(acc[g] + s_ref[0, j, sl(g), :] * kkj for g in range(ng))

    def step(t, s_kk):
        sa = [-a for a in s_kk]
        v = [load_v(t, g) for g in range(ng)]
        t_next = jnp.minimum(t + 1, tb - 1)

        def update(j, carry):
            yacc, nacc = carry
            dj_, bj, kj, rj = row(d_s, t, j), row(b_s, t, j), row(k_s, t, j), row(r_s, t, j)
            kkn = row(kk_s, t_next, j)
            y_out, n_out = [], []
            for g in range(ng):
                s = s_ref[0, j, sl(g), :] * dj_ + sa[g] * bj + v[g] * kj
                s_ref[0, j, sl(g), :] = s
                y_out.append(yacc[g] + s * rj)
                n_out.append(nacc[g] + s * kkn)
            return tuple(y_out), tuple(n_out)

        yacc, nacc = lax.fori_loop(0, dj, update, (zero, zero), unroll=8)
        for g in range(ng):
            store_y(t, g, yacc[g])
        return nacc

    lax.fori_loop(0, tb, step, lax.fori_loop(0, dj, first_dot, zero, unroll=8))
    if lane_dup:
        zrows = jnp.zeros((LANES - 2 * di, LANES), F32)

        def drain(t, carry):
            ys = y_s[pl.ds(pl.multiple_of(t * di, di), di), :]
            ym = jnp.concatenate([ys, pltpu.roll(ys, half, axis=1), zrows], axis=0)
            y_ref[:, pl.ds(pl.multiple_of(t * heads, heads), heads), :] = ym.T[:half, :half].reshape(n_seq, heads, half)
            return carry

        lax.fori_loop(0, tb, drain, 0, unroll=8)


def dplr_scan(kk, d, b, k, r, v, s0, tb):
    lane_dup = kk.ndim == 3
    dj, di = s0.shape[1], s0.shape[2]
    if lane_dup:
        n_seq, rows, width = kk.shape
        heads = LANES // 2 // n_seq
        g_n, t_len = 1, rows // heads
        assert width == LANES // 2 and n_seq * heads * 2 == LANES and v.shape == kk.shape
        jspec = ispec = pl.BlockSpec((n_seq, tb * heads, width), lambda g, t: (0, t, 0))
        scratch = [pltpu.VMEM((tb * dj, LANES), F32)] * 5 + [pltpu.VMEM((tb * di, LANES), F32)] * 2
    else:
        g_n, t_len = kk.shape[:2]
        assert kk.shape[2:] == (dj, LANES) and v.shape[2:] == (di, LANES)
        jspec = pl.BlockSpec((1, tb, dj, LANES), lambda g, t: (g, t, 0, 0))
        ispec = pl.BlockSpec((1, tb, di, LANES), lambda g, t: (g, t, 0, 0))
        scratch = []
    sspec = pl.BlockSpec((1, dj, di, LANES), lambda g, t: (g, 0, 0, 0))
    return pl.pallas_call(
        functools.partial(_scan_kernel, tb=tb, dj=dj, di=di, lane_dup=lane_dup),
        grid=(g_n, t_len // tb),
        in_specs=[jspec] * 5 + [ispec, sspec],
        out_specs=[ispec, sspec],
        out_shape=[jax.ShapeDtypeStruct(v.shape, F32), jax.ShapeDtypeStruct(s0.shape, F32)],
        scratch_shapes=scratch,
        compiler_params=_cparams(2),
        name="dplr_scan",
    )(kk, d, b, k, r, v, s0)


def _to_key_layout(x, n_seq, t_len, heads, dj, dup):
    x = jnp.broadcast_to(x.reshape(1, n_seq, t_len, heads, dj), (dup, n_seq, t_len, heads, dj))
    x = x.transpose(2, 4, 0, 1, 3)
    g_n = dup * n_seq * heads // LANES
    return x.reshape(t_len, dj, g_n, LANES).transpose(2, 0, 1, 3)


def _to_value_layout(x, n_seq, t_len, heads, dv, dup):
    di = dv // dup
    x = x.reshape(n_seq, t_len, heads, dup, di).transpose(1, 4, 3, 0, 2)
    g_n = dup * n_seq * heads // LANES
    return x.reshape(t_len, di, g_n, LANES).transpose(2, 0, 1, 3)


def _from_value_layout(y, n_seq, t_len, heads, dv, dup):
    g_n, _, di, _ = y.shape
    y = y.transpose(1, 2, 0, 3).reshape(t_len, di, dup, n_seq, heads)
    return y.transpose(3, 0, 4, 2, 1).reshape(n_seq * t_len, heads * dv)


def _state_to_layout(s, dup, value_major):
    bsz, heads = s.shape[:2]
    if value_major:
        dv, dj = s.shape[2:]
        s = s.reshape(bsz, heads, dup, dv // dup, dj).transpose(4, 3, 2, 0, 1)
    else:
        dj, dv = s.shape[2:]
        s = s.reshape(bsz, heads, dj, dup, dv // dup).transpose(2, 4, 3, 0, 1)
    g_n = dup * bsz * heads // LANES
    return s.reshape(dj, dv // dup, g_n, LANES).transpose(2, 0, 1, 3)


def _state_from_layout(s, bsz, heads, dup, value_major):
    g_n, dj, di, _ = s.shape
    s = s.transpose(1, 2, 0, 3).reshape(dj, di, dup, bsz, heads)
    if value_major:
        return s.transpose(3, 4, 2, 1, 0).reshape(bsz, heads, dup * di, dj)
    return s.transpose(3, 4, 0, 2, 1).reshape(bsz, heads, dj, dup * di)


def _swa_kernel(sink_ref, q_ref, kc_ref, vc_ref, kp_ref, vp_ref, o_ref, *, tq, has_cache):
    s_len = WINDOW + tq
    kfull = jnp.concatenate([kp_ref[...], kc_ref[...]], axis=0)
    vfull = jnp.concatenate([vp_ref[...], vc_ref[...]], axis=0)
    lane = lax.broadcasted_iota(jnp.int32, kfull.shape, 1)
    low = lane < SWA_HEAD_DIM
    kswap = pltpu.roll(kfull, SWA_HEAD_DIM, axis=1)
    vswap = pltpu.roll(vfull, SWA_HEAD_DIM, axis=1)
    k_lo = [jnp.where(low, kfull, 0.0), jnp.where(low, kswap, 0.0)]
    k_hi = [jnp.where(low, 0.0, kswap), jnp.where(low, 0.0, kfull)]
    v_lo = [jnp.where(low, vfull, 0.0), jnp.where(low, vswap, 0.0)]
    v_hi = [jnp.where(low, 0.0, vswap), jnp.where(low, 0.0, vfull)]
    t_idx = lax.broadcasted_iota(jnp.int32, (tq, s_len), 0)
    s_idx = lax.broadcasted_iota(jnp.int32, (tq, s_len), 1)
    delta = WINDOW + t_idx - s_idx
    valid = (delta >= 0) & (delta <= WINDOW)
    if not has_cache:
        valid = valid & ((s_idx >= WINDOW) | (pl.program_id(1) > 0))
    contract_last = (((1,), (1,)), ((), ()))
    for hp in range(SWA_HEADS // 2):
        qp = q_ref[:, hp * LANES:(hp + 1) * LANES]
        kv = (2 * hp) // SWA_GROUP
        acc = None
        for half, (kmat, vmat) in enumerate(((k_lo[kv], v_lo[kv]), (k_hi[kv], v_hi[kv]))):
            sink = sink_ref[2 * hp + half]
            s = lax.dot_general(qp, kmat, contract_last, preferred_element_type=F32) * SWA_SCALE
            s = jnp.where(valid, s, -jnp.inf)
            m = jnp.maximum(jnp.max(s, axis=-1, keepdims=True), sink)
            e = jnp.exp(s - m)
            den = jnp.sum(e, axis=-1, keepdims=True) + jnp.exp(sink - m)
            o = _dot(e, vmat) / den
            acc = o if acc is None else acc + o
        o_ref[:, hp * LANES:(hp + 1) * LANES] = acc.astype(o_ref.dtype)


def swa_attention(sinks, p_swa, n_seq, t_len, tq, row0, cache_k=None, cache_v=None, dest=None):
    n_blk = t_len // tq
    has_cache = cache_k is not None
    kcol, vcol = SWA_WIDTH // LANES, SWA_WIDTH // LANES + 1
    q_spec = _row_spec(SWA_WIDTH, tq, n_blk, row0)
    kc_spec = _row_spec(LANES, tq, n_blk, row0, kcol)
    vc_spec = _row_spec(LANES, tq, n_blk, row0, vcol)
    if has_cache:
        kp_spec = vp_spec = pl.BlockSpec((WINDOW, LANES), lambda b, t: (b, 0))
        kp_arr, vp_arr = cache_k, cache_v
    else:
        assert tq == WINDOW and row0 == 0

        def prev_rows(b, t):
            return jnp.maximum(b * n_blk + t - 1, 0)

        kp_spec = pl.BlockSpec((WINDOW, LANES), lambda b, t: (prev_rows(b, t), kcol))
        vp_spec = pl.BlockSpec((WINDOW, LANES), lambda b, t: (prev_rows(b, t), vcol))
        kp_arr = vp_arr = p_swa
    kern, out_shape, extra, extra_specs, aliases = _shared_rows_out(
        functools.partial(_swa_kernel, tq=tq, has_cache=has_cache), 6, n_seq * t_len, SWA_WIDTH, BF16, dest)
    return pl.pallas_call(
        kern,
        grid=(n_seq, n_blk),
        in_specs=[pl.BlockSpec(memory_space=pltpu.SMEM), q_spec, kc_spec, vc_spec, kp_spec, vp_spec] + extra_specs,
        out_specs=_row_spec(SWA_WIDTH, tq, n_blk, 0 if dest is None else row0),
        out_shape=out_shape,
        input_output_aliases=aliases,
        compiler_params=_cparams(2),
        name="swa",
    )(sinks, p_swa, p_swa, p_swa, kp_arr, vp_arr, *extra)


def _gdn_prep_kernel(cur_ref, prev_ref, st_ref, ba_ref, cw_ref, alog_ref, dt_ref, eb_ref, eg_ref, j_ref,
                     kn_o, d_o, b_o, v_o, q_o, *, chunked):
    x = cur_ref[...]
    prev8 = jnp.where(pl.program_id(1) == 0, st_ref[...], prev_ref[...])
    y = x * cw_ref[3:4, :]
    for k in range(1, CONV_WIDTH):
        y = y + _shifted(prev8, x, k) * cw_ref[3 - k:4 - k, :]
    act = y * _sigmoid(y)
    w2 = GDN_WIDTH
    q, k, v = act[:, :w2], act[:, w2:2 * w2], act[:, 2 * w2:]
    j = j_ref[...]
    qn = q * lax.rsqrt(_segsum(q * q, j) + L2_EPS) * (GDN_HEAD_DIM ** -0.5)
    kn = k * lax.rsqrt(_segsum(k * k, j) + L2_EPS)
    ba = ba_ref[...]
    ba = jnp.where(lax.broadcasted_iota(jnp.int32, ba.shape, 1) < 2 * GDN_HEADS, ba, 0.0)
    beta = _dot_lhs2(_sigmoid(ba), eb_ref[...])
    gate = _dot_lhs2(-jnp.exp(alog_ref[...]) * _softplus(ba + dt_ref[...]), eg_ref[...])
    kn_o[...] = kn
    q_o[...] = qn
    if chunked:
        d_o[...] = gate
        b_o[...] = beta
        v_o[...] = v
    else:
        eg = jnp.exp(gate)
        d_o[...] = eg
        b_o[...] = eg * beta * kn
        v_o[...] = beta * v


def gdn_prep(p_gdn, p_swa, st8, prm, n_seq, t_len, tb, row0, chunked):
    n_tb = t_len // tb
    cur, prev, state = _seq_specs(GDN_CONV_DIM, 0, tb, n_tb, row0)
    ba_spec = _row_spec(LANES, tb, n_tb, row0)
    out = jax.ShapeDtypeStruct((n_seq * t_len, GDN_WIDTH), F32)
    ospec = _row_spec(GDN_WIDTH, tb, n_tb)
    return pl.pallas_call(
        functools.partial(_gdn_prep_kernel, chunked=chunked),
        grid=(n_seq, n_tb),
        in_specs=[cur, prev, state, ba_spec, _const_spec((SUBLANES, GDN_CONV_DIM)), _const_spec((1, LANES)),
                  _const_spec((1, LANES)), _const_spec((LANES, GDN_WIDTH)),
                  _const_spec((LANES, GDN_WIDTH)), _const_spec((LANES, LANES))],
        out_specs=[ospec] * 5,
        out_shape=[out] * 5,
        compiler_params=_cparams(2),
        name="gdn_prep",
    )(p_gdn, p_gdn, st8, p_swa, prm["conv_w"], prm["a_log"], prm["dt"], prm["e_beta"], prm["e_gate"], prm["j128"])


def _gdn_post_kernel(o_ref_in, z_ref, w_ref, j_ref, o_ref):
    o = o_ref_in[...]
    z = z_ref[...]
    ms = _segsum(o * o, j_ref[...]) * (1.0 / GDN_HEAD_DIM)
    y = o * lax.rsqrt(ms + NORM_EPS) * w_ref[...]
    o_ref[...] = (y * (z * _sigmoid(z))).astype(o_ref.dtype)


def gdn_post(o, p_gdn, prm, tb, row0, dest):
    rows = o.shape[0]
    spec = pl.BlockSpec((tb, GDN_WIDTH), lambda i: (i, 0))
    zspec = pl.BlockSpec((tb, GDN_WIDTH), lambda i: (row0 // tb + i, GDN_CONV_DIM // GDN_WIDTH))
    kern, out_shape, extra, extra_specs, aliases = _shared_rows_out(_gdn_post_kernel, 4, rows, GDN_WIDTH, BF16, dest)
    return pl.pallas_call(
        kern,
        grid=(rows // tb,),
        in_specs=[spec, zspec, pl.BlockSpec((1, GDN_WIDTH), lambda i: (0, 0)),
                  pl.BlockSpec((LANES, LANES), lambda i: (0, 0))] + extra_specs,
        out_specs=pl.BlockSpec((tb, GDN_WIDTH), lambda i: ((0 if dest is None else row0 // tb) + i, 0)),
        out_shape=out_shape,
        input_output_aliases=aliases,
        compiler_params=_cparams(1),
        name="gdn_post",
    )(o, p_gdn, prm["norm_w"], prm["j128"], *extra)


def _split3(x):
    hi = x.astype(BF16)
    r = x - hi.astype(F32)
    mid = r.astype(BF16)
    return hi, mid, (r - mid.astype(F32)).astype(BF16)


def _dotb(a, b):
    return _dot(a.astype(BF16), b.astype(BF16))


def _dot3_packed(x, w, c):
    xh = x.astype(BF16).astype(F32)
    hi_lo = xh + pltpu.roll(x - xh, c, axis=1)
    wh, wl = _split(w)
    lhs = jnp.concatenate([hi_lo, xh], axis=1).astype(BF16)
    rhs = jnp.concatenate([wh, wh, wl, jnp.zeros_like(wh)], axis=0)
    return _dot(lhs, rhs)


def _pad_rows(x, rows):
    return jnp.concatenate([x, jnp.zeros((rows - x.shape[0], x.shape[1]), x.dtype)], axis=0)


def _gdn_chunk_kernel(q_ref, k_ref, v_ref, beta_ref, g_ref, s0_ref, o_ref, s_ref, *, hb, nc):
    c = GDN_CHUNK

    @pl.when(pl.program_id(2) == 0)
    def _():
        s_ref[...] = s0_ref[...]

    row = lax.broadcasted_iota(jnp.int32, (c, LANES), 0)
    col = lax.broadcasted_iota(jnp.int32, (c, LANES), 1)
    incl, strict = row >= col, row > col
    eye = (row == col).astype(F32)
    row3 = lax.broadcasted_iota(jnp.int32, (c, 2 * LANES), 0)
    col3 = lax.broadcasted_iota(jnp.int32, (c, 2 * LANES), 1)
    tri3 = ((row3 >= col3 % c) & (col3 < 3 * c)).astype(BF16)
    lane0_3 = (lax.broadcasted_iota(jnp.int32, (c, 3 * LANES), 1) % LANES == 0).astype(BF16)
    zero_c = jnp.zeros((c, LANES), BF16)
    contract_last = (((1,), (1,)), ((), ()))
    contract_first = (((0,), (0,)), ((), ()))

    units = [(h, ci) for ci in range(nc) for h in range(hb)]
    n_u = len(units)

    def load(ref):
        return [ref[ci * c:(ci + 1) * c, h * LANES:(h + 1) * LANES] for h, ci in units]

    q, k, v, beta, g = load(q_ref), load(k_ref), load(v_ref), load(beta_ref), load(g_ref)
    lanes = lambda x, u: x[:, u * LANES:(u + 1) * LANES]
    g_parts = jnp.concatenate([jnp.concatenate(list(_split3(x)) + [zero_c], axis=0) for x in g], axis=1)
    gc_all = _dot(tri3, g_parts)
    gc = [lanes(gc_all, u) for u in range(n_u)]
    gc_parts = jnp.concatenate([_pad_rows(jnp.concatenate(_split3(x), axis=1), LANES) for x in gc], axis=0)
    gc_row_all = lax.dot_general(lane0_3, gc_parts, contract_last, preferred_element_type=F32)
    dec = [jnp.where(incl, jnp.exp(jnp.where(incl, gc[u] - lanes(gc_row_all, u), 0.0)), 0.0) for u in range(n_u)]
    kb = [k[u] * beta[u] for u in range(n_u)]
    gram = [lax.dot_general(jnp.concatenate([kb[u], q[u]], axis=0).astype(BF16), _pad_rows(k[u].astype(BF16), LANES),
                            contract_last, preferred_element_type=F32) for u in range(n_u)]
    a = [jnp.where(strict, gram[u][:c] * dec[u], 0.0) for u in range(n_u)]
    attn = [jnp.where(incl, gram[u][c:] * dec[u], 0.0)[:, :c].astype(BF16) for u in range(n_u)]
    tinv = [eye - x for x in a]
    p = [_dot3_packed(x, x, c) for x in a]
    n_levels = c.bit_length() - 2
    for lvl in range(n_levels):
        last = lvl + 1 == n_levels
        lhs = tinv if last else [jnp.concatenate([tinv[u], p[u]], axis=0) for u in range(n_u)]
        prod = [_dot3_packed(lhs[u], p[u], c) for u in range(n_u)]
        tinv = [tinv[u] + prod[u][:c] for u in range(n_u)]
        if not last:
            p = [prod[u][c:] for u in range(n_u)]
    eg = [jnp.exp(x) for x in gc]
    uw = [_dotb(tinv[u][:, :c], jnp.concatenate([v[u] * beta[u], kb[u] * eg[u]], axis=1)) for u in range(n_u)]
    w_qe = [jnp.concatenate([uw[u][:, LANES:], q[u] * eg[u]], axis=0).astype(BF16) for u in range(n_u)]
    g_last = [x[c - 1:c, :] for x in gc]
    kd = [(k[u] * jnp.exp(g_last[u] - gc[u])).astype(BF16) for u in range(n_u)]
    eg_last = [jnp.exp(x) for x in g_last]

    s = [s_ref[0, h] for h in range(hb)]
    o_rows = []
    for ci in range(nc):
        us = [ci * hb + h for h in range(hb)]
        wq = [_dot(w_qe[u], s[h].astype(BF16)) for h, u in enumerate(us)]
        v_new = [(uw[u][:, :LANES] - wq[h][:c]).astype(BF16) for h, u in enumerate(us)]
        o_rows.append(jnp.concatenate([wq[h][c:] + _dot(attn[u], v_new[h]) for h, u in enumerate(us)], axis=1))
        s = [s[h] * eg_last[u] + lax.dot_general(kd[u], v_new[h], contract_first, preferred_element_type=F32)
             for h, u in enumerate(us)]
    o_ref[...] = jnp.concatenate(o_rows, axis=0)
    s_ref[0] = jnp.stack(s)


def gdn_chunked(q, k, v, beta, g, s0, n_seq, t_len, hb, nc):
    n_ch = t_len // (GDN_CHUNK * nc)
    spec = pl.BlockSpec((GDN_CHUNK * nc, hb * LANES), lambda b, h, c: (b * n_ch + c, h))
    sspec = pl.BlockSpec((1, hb, GDN_HEAD_DIM, GDN_HEAD_DIM), lambda b, h, c: (b, h, 0, 0))
    return pl.pallas_call(
        functools.partial(_gdn_chunk_kernel, hb=hb, nc=nc),
        grid=(n_seq, GDN_HEADS // hb, n_ch),
        in_specs=[spec] * 5 + [sspec],
        out_specs=[spec, sspec],
        out_shape=[jax.ShapeDtypeStruct(q.shape, F32), jax.ShapeDtypeStruct(s0.shape, F32)],
        compiler_params=_cparams(3),
        name="gdn_chunk",
    )(q, k, v, beta, g, s0)


def rwkv_mix(p_rwkv, shift_state, wkv_state, prm, n_seq, t_len, row0, tb_prep, tb_scan, dest=None):
    st8 = _pad_state_rows(shift_state[:, None, :])
    r, kk, d, b, k, v, g = rwkv_prep(p_rwkv, st8, prm, n_seq, t_len, tb_prep, row0)
    dup = max(1, LANES // (n_seq * RWKV_HEADS))
    s0 = _state_to_layout(wkv_state, dup, value_major=True)
    if dup == 2:
        tl = lambda x: x.reshape(n_seq, t_len * RWKV_HEADS, RWKV_HEAD_DIM)
        y, s1 = dplr_scan(tl(kk), tl(d), tl(b), tl(k), tl(r), tl(v), s0, tb_scan)
        y = y.reshape(n_seq * t_len, RWKV_WIDTH)
    else:
        kl = functools.partial(_to_key_layout, n_seq=n_seq, t_len=t_len, heads=RWKV_HEADS, dj=RWKV_HEAD_DIM, dup=dup)
        vt = _to_value_layout(v, n_seq, t_len, RWKV_HEADS, RWKV_HEAD_DIM, dup)
        y, s1 = dplr_scan(kl(kk), kl(d), kl(b), kl(k), kl(r), vt, s0, tb_scan)
        y = _from_value_layout(y, n_seq, t_len, RWKV_HEADS, RWKV_HEAD_DIM, dup)
    out = rwkv_post(y, r, k, v, g, prm, tb_prep, row0, dest)
    return out, _state_from_layout(s1, n_seq, RWKV_HEADS, dup, value_major=True)


def gdn_mix(p_gdn, p_swa, conv_state, ssm_state, prm, n_seq, t_len, row0, tb_prep, tb_scan, dest=None):
    st8 = _pad_state_rows(conv_state)
    if t_len % GDN_CHUNK == 0:
        kn, g, beta, v, q = gdn_prep(p_gdn, p_swa, st8, prm, n_seq, t_len, tb_prep, row0, chunked=True)
        nc = 4 if t_len % (4 * GDN_CHUNK) == 0 else 1
        o, s1 = gdn_chunked(q, kn, v, beta, g, ssm_state, n_seq, t_len, hb=8, nc=nc)
        return gdn_post(o, p_gdn, prm, tb_prep, row0, dest), s1
    kn, d, b, v, q = gdn_prep(p_gdn, p_swa, st8, prm, n_seq, t_len, tb_prep, row0, chunked=False)
    dup = 2 * max(1, LANES // (2 * n_seq * GDN_HEADS))
    kl = functools.partial(_to_key_layout, n_seq=n_seq, t_len=t_len, heads=GDN_HEADS, dj=GDN_HEAD_DIM, dup=dup)
    vt = _to_value_layout(v, n_seq, t_len, GDN_HEADS, GDN_HEAD_DIM, dup)
    s0 = _state_to_layout(ssm_state, dup, value_major=False)
    knl = kl(kn)
    o, s1 = dplr_scan(knl, kl(d), kl(b), knl, kl(q), vt, s0, tb_scan)
    o = _from_value_layout(o, n_seq, t_len, GDN_HEADS, GDN_HEAD_DIM, dup)
    out = gdn_post(o, p_gdn, prm, tb_prep, row0, dest)
    return out, _state_from_layout(s1, n_seq, GDN_HEADS, dup, value_major=False)


def _block_ones(seg):
    i = jnp.arange(LANES)
    return (i[:, None] // seg == i[None, :] // seg).astype(BF16)


def _head_expander(first_row):
    rows = jnp.arange(LANES)[:, None]
    cols = jnp.arange(GDN_WIDTH)[None, :] // GDN_HEAD_DIM
    return (rows == cols + first_row).astype(BF16)


def _layer_params(l, W):
    row = lambda x: x.reshape(1, -1)
    zeros = jnp.zeros((DECAY_LORA, RWKV_WIDTH), F32)
    wwa = jnp.concatenate([jnp.concatenate([W["rwkv_w2"][l], zeros], axis=1),
                           jnp.concatenate([zeros, W["rwkv_a2"][l]], axis=1)], axis=0)
    lane_row = lambda x, off: jnp.pad(x, (off, LANES - off - x.shape[0])).reshape(1, LANES)
    rwkv = dict(mix=row(W["rwkv_shift_mix"][l]), w0=row(W["rwkv_w0"][l]), a0=row(W["rwkv_a0"][l]), wwa=wwa,
                g2=W["rwkv_g2"][l], k_k=row(W["rwkv_k_k"][l]), k_a=row(W["rwkv_k_a"][l]),
                r_k=row(W["rwkv_r_k"][l]), ln_w=row(W["rwkv_ln_w"][l]), ln_b=row(W["rwkv_ln_b"][l]),
                j64=_block_ones(RWKV_HEAD_DIM))
    gdn = dict(conv_w=jnp.pad(W["gdn_conv_w"][l], ((0, SUBLANES - CONV_WIDTH), (0, 0))),
               a_log=lane_row(W["gdn_A_log"][l], GDN_HEADS), dt=lane_row(W["gdn_dt_bias"][l], GDN_HEADS),
               e_beta=_head_expander(0), e_gate=_head_expander(GDN_HEADS),
               norm_w=jnp.tile(W["gdn_norm_w"][l], GDN_HEADS).reshape(1, GDN_WIDTH), j128=_block_ones(LANES))
    return rwkv, gdn


def kernel(x_prompt, x_sample, state_rwkv_shift, state_rwkv_wkv, cache_swa_k, cache_swa_v, state_gdn_conv,
           state_gdn_ssm, norm_mix, w_in, rwkv_shift_mix, rwkv_w0, rwkv_w2, rwkv_a0, rwkv_a2, rwkv_g2, rwkv_k_k,
           rwkv_k_a, rwkv_r_k, rwkv_ln_w, rwkv_ln_b, swa_sinks, gdn_conv_w, gdn_A_log, gdn_dt_bias, gdn_norm_w,
           w_out, norm_ffn, w_gate, w_up, w_down, final_norm):
    W = dict(rwkv_shift_mix=rwkv_shift_mix, rwkv_w0=rwkv_w0, rwkv_w2=rwkv_w2, rwkv_a0=rwkv_a0, rwkv_a2=rwkv_a2,
             rwkv_g2=rwkv_g2, rwkv_k_k=rwkv_k_k, rwkv_k_a=rwkv_k_a, rwkv_r_k=rwkv_r_k, rwkv_ln_w=rwkv_ln_w,
             rwkv_ln_b=rwkv_ln_b, gdn_conv_w=gdn_conv_w, gdn_A_log=gdn_A_log, gdn_dt_bias=gdn_dt_bias,
             gdn_norm_w=gdn_norm_w)
    w_in_b, w_out_b, w_gate_b, w_up_b, w_down_b = (w.astype(BF16) for w in (w_in, w_out, w_gate, w_up, w_down))
    swa_col, gdn_col, ba_col = RWKV_PROJ, RWKV_PROJ + SWA_PROJ, RWKV_PROJ + SWA_PROJ + GDN_MAIN

    u, h = rmsnorm_join(x_prompt.reshape(ROWS_P, D_MODEL), x_sample.reshape(ROWS_D, D_MODEL), norm_mix[0], BF16)
    zero_shift = jnp.zeros((BATCH, RWKV_PROJ), F32)
    zero_wkv = jnp.zeros((BATCH, RWKV_HEADS, RWKV_HEAD_DIM, RWKV_HEAD_DIM), F32)
    zero_conv = jnp.zeros((BATCH, CONV_WIDTH - 1, GDN_CONV_DIM), F32)
    zero_ssm = jnp.zeros((BATCH, GDN_HEADS, GDN_HEAD_DIM, GDN_HEAD_DIM), F32)
    wb = cache_swa_k.shape[2]
    assert wb == WINDOW
    outs = [[] for _ in range(12)]
    bm = 1056
    for l in range(DEPTH):
        rw, gd = _layer_params(l, W)
        if l > 0:
            u = rmsnorm(h, norm_mix[l], BF16)
        p_rwkv = matmul(u, w_in_b, l, 2 * bm, 256, 0, RWKV_PROJ)
        p_swa = matmul(u, w_in_b, l, 2 * bm, 256, swa_col, SWA_PROJ)
        p_gdn = matmul(u, w_in_b, l, 2 * bm, 512, gdn_col, GDN_MAIN)
        p_ba = matmul(u, w_in_b, l, bm, LANES, ba_col, LANES)

        o_r, wkv_p = rwkv_mix(p_rwkv, zero_shift, zero_wkv, rw, BATCH, SEQ, 0, 128, 32, dest=ROWS)
        o_r, wkv_d = rwkv_mix(p_rwkv, state_rwkv_shift[l], state_rwkv_wkv[l], rw, DEC_BATCH, DEC_SEQ, ROWS_P,
                              DEC_SEQ, DEC_SEQ, dest=o_r)
        o_s = swa_attention(swa_sinks[l], p_swa, BATCH, SEQ, WINDOW, 0, dest=ROWS)
        o_s = swa_attention(swa_sinks[l], p_swa, DEC_BATCH, DEC_SEQ, DEC_SEQ, ROWS_P,
                            cache_swa_k[l].reshape(DEC_BATCH * wb, SWA_KV_WIDTH),
                            cache_swa_v[l].reshape(DEC_BATCH * wb, SWA_KV_WIDTH), dest=o_s)
        o_g, ssm_p = gdn_mix(p_gdn, p_ba, zero_conv, zero_ssm, gd, BATCH, SEQ, 0, 64, 32, dest=ROWS)
        o_g, ssm_d = gdn_mix(p_gdn, p_ba, state_gdn_conv[l], state_gdn_ssm[l], gd, DEC_BATCH, DEC_SEQ, ROWS_P,
                             DEC_SEQ, DEC_SEQ, dest=o_g)
        h = matmul_residual([o_r, o_s, o_g], w_out_b, l, h, bm, 512, [RWKV_WIDTH, SWA_WIDTH, GDN_WIDTH])
        u = rmsnorm(h, norm_ffn[l], BF16)
        h1 = swiglu_matmul(u, w_gate_b, w_up_b, l, bm, 512)
        h = matmul_residual([h1], w_down_b, l, h, bm, 512, [D_FF // 2])

        def tail_p(p, n_rows, c0, c1):
            return jnp.stack([lax.slice(p, ((b + 1) * SEQ - n_rows, c0), ((b + 1) * SEQ, c1)) for b in range(BATCH)])

        def tail_d(p, n_rows, c0, c1):
            x = lax.slice(p, (ROWS_P, c0), (ROWS, c1)).reshape(DEC_BATCH, DEC_SEQ, c1 - c0)
            return x[:, DEC_SEQ - n_rows:]

        kv = lambda x: x.reshape(x.shape[0], x.shape[1], SWA_KV_HEADS, SWA_HEAD_DIM)
        k0, k1, v1 = SWA_WIDTH, SWA_WIDTH + SWA_KV_WIDTH, SWA_PROJ
        layer_out = (
            tail_p(p_rwkv, 1, 0, RWKV_PROJ)[:, 0], wkv_p, kv(tail_p(p_swa, wb, k0, k1)), kv(tail_p(p_swa, wb, k1, v1)),
            tail_p(p_gdn, CONV_WIDTH - 1, 0, GDN_CONV_DIM), ssm_p,
            tail_d(p_rwkv, 1, 0, RWKV_PROJ)[:, 0], wkv_d,
            jnp.concatenate([cache_swa_k[l], kv(tail_d(p_swa, DEC_SEQ, k0, k1))], axis=1)[:, -wb:],
            jnp.concatenate([cache_swa_v[l], kv(tail_d(p_swa, DEC_SEQ, k1, v1))], axis=1)[:, -wb:],
            tail_d(p_gdn, CONV_WIDTH - 1, 0, GDN_CONV_DIM), ssm_d)
        for lst, t in zip(outs, layer_out):
            lst.append(t)
    y = rmsnorm(h, final_norm, F32)
    y_prompt = y[:ROWS_P].reshape(BATCH, SEQ, D_MODEL)
    y_sample = y[ROWS_P:].reshape(DEC_BATCH, DEC_SEQ, D_MODEL)
    return (y_prompt, y_sample) + tuple(jnp.stack(lst) for lst in outs)
```

```python
import functools

import jax
import jax.numpy as jnp
from jax import lax
from jax.experimental import pallas as pl
from jax.experimental.pallas import tpu as pltpu

D_MODEL = 4096
BATCH = 4
SEQ = 2048
DEPTH = 4
DEC_BATCH = 32
DEC_SEQ = 8
NORM_EPS = 1e-6
L2_EPS = 1e-6
RWKV_WIDTH = 1024
RWKV_HEAD_DIM = 64
RWKV_HEADS = 16
DECAY_LORA = 64
AAA_LORA = 64
GATE_LORA = 128
RWKV_PROJ = 3 * RWKV_WIDTH + DECAY_LORA + AAA_LORA + GATE_LORA
RWKV_LN_EPS = 64e-5
SWA_WIDTH = 1024
SWA_HEAD_DIM = 64
SWA_HEADS = 16
SWA_KV_HEADS = 2
SWA_GROUP = 8
SWA_KV_WIDTH = 128
SWA_PROJ = SWA_WIDTH + 2 * SWA_KV_WIDTH
WINDOW = 128
SWA_SCALE = SWA_HEAD_DIM ** -0.5
GDN_WIDTH = 2048
GDN_HEAD_DIM = 128
GDN_HEADS = 16
GDN_CONV_DIM = 3 * GDN_WIDTH
CONV_WIDTH = 4
GDN_CHUNK = 64
GDN_MAIN = GDN_CONV_DIM + GDN_WIDTH
D_FF = 11008

ROWS_P = BATCH * SEQ
ROWS_D = DEC_BATCH * DEC_SEQ
ROWS = ROWS_P + ROWS_D

LANES = 128
SUBLANES = 8
VMEM_LIMIT = 56 * 1024 * 1024

F32 = jnp.float32
BF16 = jnp.bfloat16


def _cparams(n_axes):
    return pltpu.CompilerParams(dimension_semantics=("arbitrary",) * n_axes, vmem_limit_bytes=VMEM_LIMIT)


def _split(x):
    hi = x.astype(BF16)
    lo = (x - hi.astype(F32)).astype(BF16)
    return hi, lo


def _dot(a, b):
    return jnp.dot(a, b, preferred_element_type=F32)


def _dot_lhs2(x, m_bf16):
    hi, lo = _split(x)
    return _dot(hi, m_bf16) + _dot(lo, m_bf16)


def _dot3(x, w):
    xh, xl = _split(x)
    wh, wl = _split(w)
    return _dot(xh, wh) + (_dot(xl, wh) + _dot(xh, wl))


def _segsum(x, j_bf16):
    n = x.shape[-1] // LANES
    return jnp.concatenate([_dot_lhs2(x[:, c * LANES:(c + 1) * LANES], j_bf16) for c in range(n)], axis=-1)


def _sigmoid(x):
    return 1.0 / (1.0 + jnp.exp(-x))


def _softplus(x):
    return jnp.maximum(x, 0.0) + jnp.log(1.0 + jnp.exp(-jnp.abs(x)))


def _rmsnorm_kernel(x_ref, w_ref, o_ref):
    x = x_ref[...]
    ms = jnp.mean(x * x, axis=-1, keepdims=True)
    o_ref[...] = (x * lax.rsqrt(ms + NORM_EPS) * w_ref[...]).astype(o_ref.dtype)


def rmsnorm(x, w, out_dtype, br=256):
    r, d = x.shape
    return pl.pallas_call(
        _rmsnorm_kernel,
        grid=(r // br,),
        in_specs=[pl.BlockSpec((br, d), lambda i: (i, 0)), pl.BlockSpec((1, d), lambda i: (0, 0))],
        out_specs=pl.BlockSpec((br, d), lambda i: (i, 0)),
        out_shape=jax.ShapeDtypeStruct((r, d), out_dtype),
        compiler_params=_cparams(1),
        name="rmsnorm",
    )(x, w.reshape(1, d))


def _rmsnorm_join_kernel(xp_ref, xs_ref, w_ref, o_ref, h_ref, *, n_p):
    x = jnp.where(pl.program_id(0) < n_p, xp_ref[...], xs_ref[...])
    h_ref[...] = x
    ms = jnp.mean(x * x, axis=-1, keepdims=True)
    o_ref[...] = (x * lax.rsqrt(ms + NORM_EPS) * w_ref[...]).astype(o_ref.dtype)


def rmsnorm_join(xp, xs, w, out_dtype):
    (rp, d), rs = xp.shape, xs.shape[0]
    br = rs
    assert rp % br == 0
    n_p = rp // br
    spec = pl.BlockSpec((br, d), lambda i: (i, 0))
    return pl.pallas_call(
        functools.partial(_rmsnorm_join_kernel, n_p=n_p),
        grid=(n_p + 1,),
        in_specs=[pl.BlockSpec((br, d), lambda i: (jnp.minimum(i, n_p - 1), 0)),
                  pl.BlockSpec((br, d), lambda i: (0, 0)), pl.BlockSpec((1, d), lambda i: (0, 0))],
        out_specs=[spec, spec],
        out_shape=[jax.ShapeDtypeStruct((rp + rs, d), out_dtype), jax.ShapeDtypeStruct((rp + rs, d), xp.dtype)],
        compiler_params=_cparams(1),
        name="rmsnorm_join",
    )(xp, xs, w.reshape(1, d))


def _mm_kernel(a_ref, b_ref, o_ref):
    o_ref[...] = _dot(a_ref[...], b_ref[...]).astype(o_ref.dtype)


def matmul(a, w, layer, bm, bn, col0, n_out, out_dtype=F32):
    r, k = a.shape
    assert col0 % bn == 0 and n_out % bn == 0
    return pl.pallas_call(
        _mm_kernel,
        grid=(r // bm, n_out // bn),
        in_specs=[pl.BlockSpec((bm, k), lambda i, j: (i, 0)),
                  pl.BlockSpec((None, k, bn), lambda i, j: (layer, 0, col0 // bn + j))],
        out_specs=pl.BlockSpec((bm, bn), lambda i, j: (i, j)),
        out_shape=jax.ShapeDtypeStruct((r, n_out), out_dtype),
        compiler_params=_cparams(2),
        name="proj_in",
    )(a, w)


def _swiglu_kernel(a_ref, g_ref, u_ref, o_ref):
    a = a_ref[...]
    g = _dot(a, g_ref[...])
    u = _dot(a, u_ref[...])
    o_ref[...] = (g * _sigmoid(g) * u).astype(o_ref.dtype)


def swiglu_matmul(a, wg, wu, layer, bm, bn):
    r, k = a.shape
    n = wg.shape[2]
    wspec = pl.BlockSpec((None, k, bn), lambda i, j: (layer, 0, j))
    return pl.pallas_call(
        _swiglu_kernel,
        grid=(r // bm, pl.cdiv(n, bn)),
        in_specs=[pl.BlockSpec((bm, k), lambda i, j: (i, 0)), wspec, wspec],
        out_specs=pl.BlockSpec((bm, bn), lambda i, j: (i, j)),
        out_shape=jax.ShapeDtypeStruct((r, n), BF16),
        compiler_params=_cparams(2),
        name="ffn_swiglu",
    )(a, wg, wu)


def _mm_res_kernel(*refs, n_a):
    a_refs, b_refs, res_ref, o_ref = refs[:n_a], refs[n_a:2 * n_a], refs[2 * n_a], refs[2 * n_a + 1]
    kk = pl.program_id(2)

    @pl.when(kk == 0)
    def _():
        o_ref[...] = res_ref[...]

    acc = _dot(a_refs[0][...], b_refs[0][...])
    for a_ref, b_ref in zip(a_refs[1:], b_refs[1:]):
        acc = acc + _dot(a_ref[...], b_ref[...])
    o_ref[...] += acc


def matmul_residual(a_list, w, layer, res, bm, bn, bk_list):
    r = res.shape[0]
    n = w.shape[2]
    nk = a_list[0].shape[1] // bk_list[0]
    in_specs, row0 = [], 0
    for a, bk in zip(a_list, bk_list):
        assert a.shape[1] == nk * bk
        in_specs.append(pl.BlockSpec((bm, bk), lambda i, j, k: (i, k)))
    for a, bk in zip(a_list, bk_list):
        assert row0 % bk == 0
        in_specs.append(pl.BlockSpec((None, bk, bn), functools.partial(
            lambda i, j, k, off: (layer, off + k, j), off=row0 // bk)))
        row0 += a.shape[1]
    in_specs.append(pl.BlockSpec((bm, bn), lambda i, j, k: (i, j)))
    return pl.pallas_call(
        functools.partial(_mm_res_kernel, n_a=len(a_list)),
        grid=(r // bm, n // bn, nk),
        in_specs=in_specs,
        out_specs=pl.BlockSpec((bm, bn), lambda i, j, k: (i, j)),
        out_shape=jax.ShapeDtypeStruct((r, n), F32),
        compiler_params=_cparams(3),
        name="proj_residual",
    )(*a_list, *([w] * len(a_list)), res)


def _shifted(prev8, x, k):
    ext = jnp.concatenate([prev8, x], axis=0)
    return pltpu.roll(ext, k, axis=0)[SUBLANES:]


def _seq_specs(width, col_block, tb, n_tb, row0):
    assert row0 % tb == 0 and tb % SUBLANES == 0
    cur = pl.BlockSpec((tb, width), lambda b, t: (row0 // tb + b * n_tb + t, col_block))
    prev = pl.BlockSpec((SUBLANES, width), lambda b, t: (
        jnp.maximum((row0 + (b * n_tb + t) * tb) // SUBLANES - 1, 0), col_block))
    state = pl.BlockSpec((SUBLANES, width), lambda b, t: (b, 0))
    return cur, prev, state


def _row_spec(width, tb, n_tb, row0=0, col_block=0):
    return pl.BlockSpec((tb, width), lambda b, t: (row0 // tb + b * n_tb + t, col_block))


def _const_spec(shape):
    return pl.BlockSpec(shape, lambda b, t: (0,) * len(shape))


def _drop_ref(kernel_fn, pos, *refs):
    return kernel_fn(*refs[:pos], *refs[pos + 1:])


def _shared_rows_out(kernel_fn, n_in, rows, width, dtype, dest):
    if dest is None or isinstance(dest, int):
        return kernel_fn, jax.ShapeDtypeStruct((dest or rows, width), dtype), (), [], {}
    assert dest.shape[1] == width and dest.dtype == dtype
    return (functools.partial(_drop_ref, kernel_fn, n_in), jax.ShapeDtypeStruct(dest.shape, dtype), (dest,),
            [pl.BlockSpec(memory_space=pl.ANY)], {n_in: 0})


def _pad_state_rows(st):
    b, k, c = st.shape
    return jnp.pad(st, ((0, 0), (SUBLANES - k, 0), (0, 0))).reshape(b * SUBLANES, c)


def _rwkv_prep_kernel(cur_ref, prev_ref, st_ref, mix_ref, w0_ref, a0_ref, wwa_ref, g2_ref, kkw_ref, kaw_ref, j_ref,
                      r_o, kk_o, d_o, b_o, k_o, v_o, g_o):
    x = cur_ref[...]
    prev8 = jnp.where(pl.program_id(1) == 0, st_ref[...], prev_ref[...])
    xs = x + (_shifted(prev8, x, 1) - x) * mix_ref[...]
    w3 = RWKV_WIDTH
    r, k, v = xs[:, 0:w3], xs[:, w3:2 * w3], xs[:, 2 * w3:3 * w3]
    wa = xs[:, 3 * w3:3 * w3 + LANES]
    glo = xs[:, 3 * w3 + LANES:3 * w3 + 2 * LANES]
    lane = lax.broadcasted_iota(jnp.int32, wa.shape, 1)
    lora = _dot3(jnp.where(lane < DECAY_LORA, jnp.tanh(wa), wa), wwa_ref[...])
    w = -_softplus(-(w0_ref[...] + lora[:, :w3])) - 0.5
    d = jnp.exp(-jnp.exp(w))
    a = _sigmoid(a0_ref[...] + lora[:, w3:])
    g = _dot3(_sigmoid(glo), g2_ref[...])
    kn = k * kkw_ref[...]
    kk = kn * lax.rsqrt(_segsum(kn * kn, j_ref[...]) + L2_EPS)
    r_o[...] = r
    kk_o[...] = kk
    d_o[...] = d
    b_o[...] = kk * a
    k_o[...] = k * (1.0 + (a - 1.0) * kaw_ref[...])
    v_o[...] = v
    g_o[...] = g


def rwkv_prep(p_rwkv, st8, prm, n_seq, t_len, tb, row0):
    n_tb = t_len // tb
    cur, prev, state = _seq_specs(RWKV_PROJ, 0, tb, n_tb, row0)
    out = jax.ShapeDtypeStruct((n_seq * t_len, RWKV_WIDTH), F32)
    ospec = _row_spec(RWKV_WIDTH, tb, n_tb)
    return pl.pallas_call(
        _rwkv_prep_kernel,
        grid=(n_seq, n_tb),
        in_specs=[cur, prev, state, _const_spec((1, RWKV_PROJ)), _const_spec((1, RWKV_WIDTH)),
                  _const_spec((1, RWKV_WIDTH)), _const_spec((LANES, 2 * RWKV_WIDTH)),
                  _const_spec((GATE_LORA, RWKV_WIDTH)), _const_spec((1, RWKV_WIDTH)), _const_spec((1, RWKV_WIDTH)),
                  _const_spec((LANES, LANES))],
        out_specs=[ospec] * 7,
        out_shape=[out] * 7,
        compiler_params=_cparams(2),
        name="rwkv_prep",
    )(p_rwkv, p_rwkv, st8, prm["mix"], prm["w0"], prm["a0"], prm["wwa"], prm["g2"], prm["k_k"], prm["k_a"],
      prm["j64"])


def _rwkv_post_kernel(y_ref, r_ref, k_ref, v_ref, g_ref, lnw_ref, lnb_ref, rk_ref, j_ref, o_ref):
    j = j_ref[...]
    y = y_ref[...]
    inv_n = 1.0 / RWKV_HEAD_DIM
    yc = y - _segsum(y, j) * inv_n
    var = _segsum(yc * yc, j) * inv_n
    out = yc * lax.rsqrt(var + RWKV_LN_EPS) * lnw_ref[...] + lnb_ref[...]
    out = out + _segsum(r_ref[...] * k_ref[...] * rk_ref[...], j) * v_ref[...]
    o_ref[...] = (out * g_ref[...]).astype(o_ref.dtype)


def rwkv_post(y, r, k, v, g, prm, tb, row0, dest):
    rows = y.shape[0]
    spec = pl.BlockSpec((tb, RWKV_WIDTH), lambda i: (i, 0))
    cspec = pl.BlockSpec((1, RWKV_WIDTH), lambda i: (0, 0))
    kern, out_shape, extra, extra_specs, aliases = _shared_rows_out(_rwkv_post_kernel, 9, rows, RWKV_WIDTH, BF16, dest)
    return pl.pallas_call(
        kern,
        grid=(rows // tb,),
        in_specs=[spec] * 5 + [cspec] * 3 + [pl.BlockSpec((LANES, LANES), lambda i: (0, 0))] + extra_specs,
        out_specs=pl.BlockSpec((tb, RWKV_WIDTH), lambda i: ((0 if dest is None else row0 // tb) + i, 0)),
        out_shape=out_shape,
        input_output_aliases=aliases,
        compiler_params=_cparams(1),
        name="rwkv_post",
    )(y, r, k, v, g, prm["ln_w"], prm["ln_b"], prm["r_k"], prm["j64"], *extra)


def _scan_kernel(kk_ref, d_ref, b_ref, k_ref, r_ref, v_ref, s0_ref, y_ref, s_ref, *scratch, tb, dj, di, lane_dup):
    ng = di // SUBLANES
    half = LANES // 2

    @pl.when(pl.program_id(1) == 0)
    def _():
        s_ref[...] = s0_ref[...]

    if lane_dup:
        key_refs, v_s, y_s = scratch[:5], scratch[5], scratch[6]
        n_seq = kk_ref.shape[0]
        heads = half // n_seq
        assert dj == half and 2 * di == half
        zpad = jnp.zeros((LANES, half), F32)

        def head_rows(ref, t):
            return ref[:, pl.ds(pl.multiple_of(t * heads, heads), heads), :].reshape(half, half)

        low = lax.broadcasted_iota(jnp.int32, (half, LANES), 1) < half
        low_v = lax.broadcasted_iota(jnp.int32, (di, LANES), 1) < half

        def fill(t, carry):
            rows = pl.ds(pl.multiple_of(t * dj, dj), dj)

            def pair(ref_a, ref_b):
                w = jnp.concatenate([head_rows(ref_a, t), head_rows(ref_b, t)], axis=0)
                z = jnp.concatenate([w, zpad], axis=1).T[:half]
                return z, pltpu.roll(z, half, axis=1)

            for (ref_a, dst_a), (ref_b, dst_b) in (((kk_ref, key_refs[0]), (d_ref, key_refs[1])),
                                                   ((b_ref, key_refs[2]), (k_ref, key_refs[3]))):
                z, zr = pair(ref_a, ref_b)
                dst_a[rows, :] = jnp.where(low, z, zr)
                dst_b[rows, :] = jnp.where(low, zr, z)
            z, zr = pair(r_ref, v_ref)
            key_refs[4][rows, :] = jnp.where(low, z, zr)
            v_s[pl.ds(pl.multiple_of(t * di, di), di), :] = jnp.where(low_v, zr[:di], z[di:2 * di])
            return carry

        lax.fori_loop(0, tb, fill, 0, unroll=8)
        kk_s, d_s, b_s, k_s, r_s = key_refs

        def row(ref, t, j):
            return ref[pl.ds(t * dj + j, 1), :]

        def vrows(t, g):
            return pl.ds(pl.multiple_of(t * di + g * SUBLANES, SUBLANES), SUBLANES)

        load_v = lambda t, g: v_s[vrows(t, g), :]

        def store_y(t, g, val):
            y_s[vrows(t, g), :] = val
    else:
        kk_s, d_s, b_s, k_s, r_s = kk_ref, d_ref, b_ref, k_ref, r_ref

        def row(ref, t, j):
            return ref[0, t, pl.ds(j, 1), :]

        load_v = lambda t, g: v_ref[0, t, pl.ds(g * SUBLANES, SUBLANES), :]

        def store_y(t, g, val):
            y_ref[0, t, pl.ds(g * SUBLANES, SUBLANES), :] = val

    def sl(g):
        return pl.ds(g * SUBLANES, SUBLANES)

    zero = tuple(jnp.zeros((SUBLANES, LANES), F32) for _ in range(ng))

    def first_dot(j, acc):
        kkj = row(kk_s, 0, j)
        return tuple(acc[g] + s_ref[0, j, sl(g), :] * kkj for g in range(ng))

    def step(t, s_kk):
        sa = [-a for a in s_kk]
        v = [load_v(t, g) for g in range(ng)]
        t_next = jnp.minimum(t + 1, tb - 1)

        def update(j, carry):
            yacc, nacc = carry
            dj_, bj, kj, rj = row(d_s, t, j), row(b_s, t, j), row(k_s, t, j), row(r_s, t, j)
            kkn = row(kk_s, t_next, j)
            y_out, n_out = [], []
            for g in range(ng):
                s = s_ref[0, j, sl(g), :] * dj_ + sa[g] * bj + v[g] * kj
                s_ref[0, j, sl(g), :] = s
                y_out.append(yacc[g] + s * rj)
                n_out.append(nacc[g] + s * kkn)
            return tuple(y_out), tuple(n_out)

        yacc, nacc = lax.fori_loop(0, dj, update, (zero, zero), unroll=8)
        for g in range(ng):
            store_y(t, g, yacc[g])
        return nacc

    lax.fori_loop(0, tb, step, lax.fori_loop(0, dj, first_dot, zero, unroll=8))
    if lane_dup:
        zrows = jnp.zeros((LANES - 2 * di, LANES), F32)

        def drain(t, carry):
            ys = y_s[pl.ds(pl.multiple_of(t * di, di), di), :]
            ym = jnp.concatenate([ys, pltpu.roll(ys, half, axis=1), zrows], axis=0)
            y_ref[:, pl.ds(pl.multiple_of(t * heads, heads), heads), :] = ym.T[:half, :half].reshape(n_seq, heads, half)
            return carry

        lax.fori_loop(0, tb, drain, 0, unroll=8)


def dplr_scan(kk, d, b, k, r, v, s0, tb):
    lane_dup = kk.ndim == 3
    dj, di = s0.shape[1], s0.shape[2]
    if lane_dup:
        n_seq, rows, width = kk.shape
        heads = LANES // 2 // n_seq
        g_n, t_len = 1, rows // heads
        assert width == LANES // 2 and n_seq * heads * 2 == LANES and v.shape == kk.shape
        jspec = ispec = pl.BlockSpec((n_seq, tb * heads, width), lambda g, t: (0, t, 0))
        scratch = [pltpu.VMEM((tb * dj, LANES), F32)] * 5 + [pltpu.VMEM((tb * di, LANES), F32)] * 2
    else:
        g_n, t_len = kk.shape[:2]
        assert kk.shape[2:] == (dj, LANES) and v.shape[2:] == (di, LANES)
        jspec = pl.BlockSpec((1, tb, dj, LANES), lambda g, t: (g, t, 0, 0))
        ispec = pl.BlockSpec((1, tb, di, LANES), lambda g, t: (g, t, 0, 0))
        scratch = []
    sspec = pl.BlockSpec((1, dj, di, LANES), lambda g, t: (g, 0, 0, 0))
    return pl.pallas_call(
        functools.partial(_scan_kernel, tb=tb, dj=dj, di=di, lane_dup=lane_dup),
        grid=(g_n, t_len // tb),
        in_specs=[jspec] * 5 + [ispec, sspec],
        out_specs=[ispec, sspec],
        out_shape=[jax.ShapeDtypeStruct(v.shape, F32), jax.ShapeDtypeStruct(s0.shape, F32)],
        scratch_shapes=scratch,
        compiler_params=_cparams(2),
        name="dplr_scan",
    )(kk, d, b, k, r, v, s0)


def _to_key_layout(x, n_seq, t_len, heads, dj, dup):
    x = jnp.broadcast_to(x.reshape(1, n_seq, t_len, heads, dj), (dup, n_seq, t_len, heads, dj))
    x = x.transpose(2, 4, 0, 1, 3)
    g_n = dup * n_seq * heads // LANES
    return x.reshape(t_len, dj, g_n, LANES).transpose(2, 0, 1, 3)


def _to_value_layout(x, n_seq, t_len, heads, dv, dup):
    di = dv // dup
    x = x.reshape(n_seq, t_len, heads, dup, di).transpose(1, 4, 3, 0, 2)
    g_n = dup * n_seq * heads // LANES
    return x.reshape(t_len, di, g_n, LANES).transpose(2, 0, 1, 3)


def _from_value_layout(y, n_seq, t_len, heads, dv, dup):
    g_n, _, di, _ = y.shape
    y = y.transpose(1, 2, 0, 3).reshape(t_len, di, dup, n_seq, heads)
    return y.transpose(3, 0, 4, 2, 1).reshape(n_seq * t_len, heads * dv)


def _state_to_layout(s, dup, value_major):
    bsz, heads = s.shape[:2]
    if value_major:
        dv, dj = s.shape[2:]
        s = s.reshape(bsz, heads, dup, dv // dup, dj).transpose(4, 3, 2, 0, 1)
    else:
        dj, dv = s.shape[2:]
        s = s.reshape(bsz, heads, dj, dup, dv // dup).transpose(2, 4, 3, 0, 1)
    g_n = dup * bsz * heads // LANES
    return s.reshape(dj, dv // dup, g_n, LANES).transpose(2, 0, 1, 3)


def _state_from_layout(s, bsz, heads, dup, value_major):
    g_n, dj, di, _ = s.shape
    s = s.transpose(1, 2, 0, 3).reshape(dj, di, dup, bsz, heads)
    if value_major:
        return s.transpose(3, 4, 2, 1, 0).reshape(bsz, heads, dup * di, dj)
    return s.transpose(3, 4, 0, 2, 1).reshape(bsz, heads, dj, dup * di)


def _swa_kernel(sink_ref, q_ref, kc_ref, vc_ref, kp_ref, vp_ref, o_ref, *, tq, has_cache):
    s_len = WINDOW + tq
    kfull = jnp.concatenate([kp_ref[...], kc_ref[...]], axis=0)
    vfull = jnp.concatenate([vp_ref[...], vc_ref[...]], axis=0)
    lane = lax.broadcasted_iota(jnp.int32, kfull.shape, 1)
    low = lane < SWA_HEAD_DIM
    kswap = pltpu.roll(kfull, SWA_HEAD_DIM, axis=1)
    vswap = pltpu.roll(vfull, SWA_HEAD_DIM, axis=1)
    k_lo = [jnp.where(low, kfull, 0.0), jnp.where(low, kswap, 0.0)]
    k_hi = [jnp.where(low, 0.0, kswap), jnp.where(low, 0.0, kfull)]
    v_lo = [jnp.where(low, vfull, 0.0), jnp.where(low, vswap, 0.0)]
    v_hi = [jnp.where(low, 0.0, vswap), jnp.where(low, 0.0, vfull)]
    t_idx = lax.broadcasted_iota(jnp.int32, (tq, s_len), 0)
    s_idx = lax.broadcasted_iota(jnp.int32, (tq, s_len), 1)
    delta = WINDOW + t_idx - s_idx
    valid = (delta >= 0) & (delta <= WINDOW)
    if not has_cache:
        valid = valid & ((s_idx >= WINDOW) | (pl.program_id(1) > 0))
    contract_last = (((1,), (1,)), ((), ()))
    for hp in range(SWA_HEADS // 2):
        qp = q_ref[:, hp * LANES:(hp + 1) * LANES]
        kv = (2 * hp) // SWA_GROUP
        acc = None
        for half, (kmat, vmat) in enumerate(((k_lo[kv], v_lo[kv]), (k_hi[kv], v_hi[kv]))):
            sink = sink_ref[2 * hp + half]
            s = lax.dot_general(qp, kmat, contract_last, preferred_element_type=F32) * SWA_SCALE
            s = jnp.where(valid, s, -jnp.inf)
            m = jnp.maximum(jnp.max(s, axis=-1, keepdims=True), sink)
            e = jnp.exp(s - m)
            den = jnp.sum(e, axis=-1, keepdims=True) + jnp.exp(sink - m)
            o = _dot(e, vmat) / den
            acc = o if acc is None else acc + o
        o_ref[:, hp * LANES:(hp + 1) * LANES] = acc.astype(o_ref.dtype)


def swa_attention(sinks, p_swa, n_seq, t_len, tq, row0, cache_k=None, cache_v=None, dest=None):
    n_blk = t_len // tq
    has_cache = cache_k is not None
    kcol, vcol = SWA_WIDTH // LANES, SWA_WIDTH // LANES + 1
    q_spec = _row_spec(SWA_WIDTH, tq, n_blk, row0)
    kc_spec = _row_spec(LANES, tq, n_blk, row0, kcol)
    vc_spec = _row_spec(LANES, tq, n_blk, row0, vcol)
    if has_cache:
        kp_spec = vp_spec = pl.BlockSpec((WINDOW, LANES), lambda b, t: (b, 0))
        kp_arr, vp_arr = cache_k, cache_v
    else:
        assert tq == WINDOW and row0 == 0

        def prev_rows(b, t):
            return jnp.maximum(b * n_blk + t - 1, 0)

        kp_spec = pl.BlockSpec((WINDOW, LANES), lambda b, t: (prev_rows(b, t), kcol))
        vp_spec = pl.BlockSpec((WINDOW, LANES), lambda b, t: (prev_rows(b, t), vcol))
        kp_arr = vp_arr = p_swa
    kern, out_shape, extra, extra_specs, aliases = _shared_rows_out(
        functools.partial(_swa_kernel, tq=tq, has_cache=has_cache), 6, n_seq * t_len, SWA_WIDTH, BF16, dest)
    return pl.pallas_call(
        kern,
        grid=(n_seq, n_blk),
        in_specs=[pl.BlockSpec(memory_space=pltpu.SMEM), q_spec, kc_spec, vc_spec, kp_spec, vp_spec] + extra_specs,
        out_specs=_row_spec(SWA_WIDTH, tq, n_blk, 0 if dest is None else row0),
        out_shape=out_shape,
        input_output_aliases=aliases,
        compiler_params=_cparams(2),
        name="swa",
    )(sinks, p_swa, p_swa, p_swa, kp_arr, vp_arr, *extra)


def _gdn_prep_kernel(cur_ref, prev_ref, st_ref, ba_ref, cw_ref, alog_ref, dt_ref, eb_ref, eg_ref, j_ref,
                     kn_o, d_o, b_o, v_o, q_o, *, chunked):
    x = cur_ref[...]
    prev8 = jnp.where(pl.program_id(1) == 0, st_ref[...], prev_ref[...])
    y = x * cw_ref[3:4, :]
    for k in range(1, CONV_WIDTH):
        y = y + _shifted(prev8, x, k) * cw_ref[3 - k:4 - k, :]
    act = y * _sigmoid(y)
    w2 = GDN_WIDTH
    q, k, v = act[:, :w2], act[:, w2:2 * w2], act[:, 2 * w2:]
    j = j_ref[...]
    qn = q * lax.rsqrt(_segsum(q * q, j) + L2_EPS) * (GDN_HEAD_DIM ** -0.5)
    kn = k * lax.rsqrt(_segsum(k * k, j) + L2_EPS)
    ba = ba_ref[...]
    ba = jnp.where(lax.broadcasted_iota(jnp.int32, ba.shape, 1) < 2 * GDN_HEADS, ba, 0.0)
    beta = _dot_lhs2(_sigmoid(ba), eb_ref[...])
    gate = _dot_lhs2(-jnp.exp(alog_ref[...]) * _softplus(ba + dt_ref[...]), eg_ref[...])
    kn_o[...] = kn
    q_o[...] = qn
    if chunked:
        d_o[...] = gate
        b_o[...] = beta
        v_o[...] = v
    else:
        eg = jnp.exp(gate)
        d_o[...] = eg
        b_o[...] = eg * beta * kn
        v_o[...] = beta * v


def gdn_prep(p_gdn, p_swa, st8, prm, n_seq, t_len, tb, row0, chunked):
    n_tb = t_len // tb
    cur, prev, state = _seq_specs(GDN_CONV_DIM, 0, tb, n_tb, row0)
    ba_spec = _row_spec(LANES, tb, n_tb, row0)
    out = jax.ShapeDtypeStruct((n_seq * t_len, GDN_WIDTH), F32)
    ospec = _row_spec(GDN_WIDTH, tb, n_tb)
    return pl.pallas_call(
        functools.partial(_gdn_prep_kernel, chunked=chunked),
        grid=(n_seq, n_tb),
        in_specs=[cur, prev, state, ba_spec, _const_spec((SUBLANES, GDN_CONV_DIM)), _const_spec((1, LANES)),
                  _const_spec((1, LANES)), _const_spec((LANES, GDN_WIDTH)),
                  _const_spec((LANES, GDN_WIDTH)), _const_spec((LANES, LANES))],
        out_specs=[ospec] * 5,
        out_shape=[out] * 5,
        compiler_params=_cparams(2),
        name="gdn_prep",
    )(p_gdn, p_gdn, st8, p_swa, prm["conv_w"], prm["a_log"], prm["dt"], prm["e_beta"], prm["e_gate"], prm["j128"])


def _gdn_post_kernel(o_ref_in, z_ref, w_ref, j_ref, o_ref):
    o = o_ref_in[...]
    z = z_ref[...]
    ms = _segsum(o * o, j_ref[...]) * (1.0 / GDN_HEAD_DIM)
    y = o * lax.rsqrt(ms + NORM_EPS) * w_ref[...]
    o_ref[...] = (y * (z * _sigmoid(z))).astype(o_ref.dtype)


def gdn_post(o, p_gdn, prm, tb, row0, dest):
    rows = o.shape[0]
    spec = pl.BlockSpec((tb, GDN_WIDTH), lambda i: (i, 0))
    zspec = pl.BlockSpec((tb, GDN_WIDTH), lambda i: (row0 // tb + i, GDN_CONV_DIM // GDN_WIDTH))
    kern, out_shape, extra, extra_specs, aliases = _shared_rows_out(_gdn_post_kernel, 4, rows, GDN_WIDTH, BF16, dest)
    return pl.pallas_call(
        kern,
        grid=(rows // tb,),
        in_specs=[spec, zspec, pl.BlockSpec((1, GDN_WIDTH), lambda i: (0, 0)),
                  pl.BlockSpec((LANES, LANES), lambda i: (0, 0))] + extra_specs,
        out_specs=pl.BlockSpec((tb, GDN_WIDTH), lambda i: ((0 if dest is None else row0 // tb) + i, 0)),
        out_shape=out_shape,
        input_output_aliases=aliases,
        compiler_params=_cparams(1),
        name="gdn_post",
    )(o, p_gdn, prm["norm_w"], prm["j128"], *extra)


def _split3(x):
    hi = x.astype(BF16)
    r = x - hi.astype(F32)
    mid = r.astype(BF16)
    return hi, mid, (r - mid.astype(F32)).astype(BF16)


def _dotb(a, b):
    return _dot(a.astype(BF16), b.astype(BF16))


def _dot3_packed(x, w, c):
    xh = x.astype(BF16).astype(F32)
    hi_lo = xh + pltpu.roll(x - xh, c, axis=1)
    wh, wl = _split(w)
    lhs = jnp.concatenate([hi_lo, xh], axis=1).astype(BF16)
    rhs = jnp.concatenate([wh, wh, wl, jnp.zeros_like(wh)], axis=0)
    return _dot(lhs, rhs)


def _pad_rows(x, rows):
    return jnp.concatenate([x, jnp.zeros((rows - x.shape[0], x.shape[1]), x.dtype)], axis=0)


def _gdn_chunk_kernel(q_ref, k_ref, v_ref, beta_ref, g_ref, s0_ref, o_ref, s_ref, *, hb, nc):
    c = GDN_CHUNK

    @pl.when(pl.program_id(2) == 0)
    def _():
        s_ref[...] = s0_ref[...]

    row = lax.broadcasted_iota(jnp.int32, (c, LANES), 0)
    col = lax.broadcasted_iota(jnp.int32, (c, LANES), 1)
    incl, strict = row >= col, row > col
    eye = (row == col).astype(F32)
    row3 = lax.broadcasted_iota(jnp.int32, (c, 2 * LANES), 0)
    col3 = lax.broadcasted_iota(jnp.int32, (c, 2 * LANES), 1)
    tri3 = ((row3 >= col3 % c) & (col3 < 3 * c)).astype(BF16)
    lane0_3 = (lax.broadcasted_iota(jnp.int32, (c, 3 * LANES), 1) % LANES == 0).astype(BF16)
    zero_c = jnp.zeros((c, LANES), BF16)
    contract_last = (((1,), (1,)), ((), ()))
    contract_first = (((0,), (0,)), ((), ()))

    units = [(h, ci) for ci in range(nc) for h in range(hb)]
    n_u = len(units)

    def load(ref):
        return [ref[ci * c:(ci + 1) * c, h * LANES:(h + 1) * LANES] for h, ci in units]

    q, k, v, beta, g = load(q_ref), load(k_ref), load(v_ref), load(beta_ref), load(g_ref)
    lanes = lambda x, u: x[:, u * LANES:(u + 1) * LANES]
    g_parts = jnp.concatenate([jnp.concatenate(list(_split3(x)) + [zero_c], axis=0) for x in g], axis=1)
    gc_all = _dot(tri3, g_parts)
    gc = [lanes(gc_all, u) for u in range(n_u)]
    gc_parts = jnp.concatenate([_pad_rows(jnp.concatenate(_split3(x), axis=1), LANES) for x in gc], axis=0)
    gc_row_all = lax.dot_general(lane0_3, gc_parts, contract_last, preferred_element_type=F32)
    dec = [jnp.where(incl, jnp.exp(jnp.where(incl, gc[u] - lanes(gc_row_all, u), 0.0)), 0.0) for u in range(n_u)]
    kb = [k[u] * beta[u] for u in range(n_u)]
    gram = [lax.dot_general(jnp.concatenate([kb[u], q[u]], axis=0).astype(BF16), _pad_rows(k[u].astype(BF16), LANES),
                            contract_last, preferred_element_type=F32) for u in range(n_u)]
    a = [jnp.where(strict, gram[u][:c] * dec[u], 0.0) for u in range(n_u)]
    attn = [jnp.where(incl, gram[u][c:] * dec[u], 0.0)[:, :c].astype(BF16) for u in range(n_u)]
    tinv = [eye - x for x in a]
    p = [_dot3_packed(x, x, c) for x in a]
    n_levels = c.bit_length() - 2
    for lvl in range(n_levels):
        last = lvl + 1 == n_levels
        lhs = tinv if last else [jnp.concatenate([tinv[u], p[u]], axis=0) for u in range(n_u)]
        prod = [_dot3_packed(lhs[u], p[u], c) for u in range(n_u)]
        tinv = [tinv[u] + prod[u][:c] for u in range(n_u)]
        if not last:
            p = [prod[u][c:] for u in range(n_u)]
    eg = [jnp.exp(x) for x in gc]
    uw = [_dotb(tinv[u][:, :c], jnp.concatenate([v[u] * beta[u], kb[u] * eg[u]], axis=1)) for u in range(n_u)]
    w_qe = [jnp.concatenate([uw[u][:, LANES:], q[u] * eg[u]], axis=0).astype(BF16) for u in range(n_u)]
    g_last = [x[c - 1:c, :] for x in gc]
    kd = [(k[u] * jnp.exp(g_last[u] - gc[u])).astype(BF16) for u in range(n_u)]
    eg_last = [jnp.exp(x) for x in g_last]

    s = [s_ref[0, h] for h in range(hb)]
    o_rows = []
    for ci in range(nc):
        us = [ci * hb + h for h in range(hb)]
        wq = [_dot(w_qe[u], s[h].astype(BF16)) for h, u in enumerate(us)]
        v_new = [(uw[u][:, :LANES] - wq[h][:c]).astype(BF16) for h, u in enumerate(us)]
        o_rows.append(jnp.concatenate([wq[h][c:] + _dot(attn[u], v_new[h]) for h, u in enumerate(us)], axis=1))
        s = [s[h] * eg_last[u] + lax.dot_general(kd[u], v_new[h], contract_first, preferred_element_type=F32)
             for h, u in enumerate(us)]
    o_ref[...] = jnp.concatenate(o_rows, axis=0)
    s_ref[0] = jnp.stack(s)


def gdn_chunked(q, k, v, beta, g, s0, n_seq, t_len, hb, nc):
    n_ch = t_len // (GDN_CHUNK * nc)
    spec = pl.BlockSpec((GDN_CHUNK * nc, hb * LANES), lambda b, h, c: (b * n_ch + c, h))
    sspec = pl.BlockSpec((1, hb, GDN_HEAD_DIM, GDN_HEAD_DIM), lambda b, h, c: (b, h, 0, 0))
    return pl.pallas_call(
        functools.partial(_gdn_chunk_kernel, hb=hb, nc=nc),
        grid=(n_seq, GDN_HEADS // hb, n_ch),
        in_specs=[spec] * 5 + [sspec],
        out_specs=[spec, sspec],
        out_shape=[jax.ShapeDtypeStruct(q.shape, F32), jax.ShapeDtypeStruct(s0.shape, F32)],
        compiler_params=_cparams(3),
        name="gdn_chunk",
    )(q, k, v, beta, g, s0)


def rwkv_mix(p_rwkv, shift_state, wkv_state, prm, n_seq, t_len, row0, tb_prep, tb_scan, dest=None):
    st8 = _pad_state_rows(shift_state[:, None, :])
    r, kk, d, b, k, v, g = rwkv_prep(p_rwkv, st8, prm, n_seq, t_len, tb_prep, row0)
    dup = max(1, LANES // (n_seq * RWKV_HEADS))
    s0 = _state_to_layout(wkv_state, dup, value_major=True)
    if dup == 2:
        tl = lambda x: x.reshape(n_seq, t_len * RWKV_HEADS, RWKV_HEAD_DIM)
        y, s1 = dplr_scan(tl(kk), tl(d), tl(b), tl(k), tl(r), tl(v), s0, tb_scan)
        y = y.reshape(n_seq * t_len, RWKV_WIDTH)
    else:
        kl = functools.partial(_to_key_layout, n_seq=n_seq, t_len=t_len, heads=RWKV_HEADS, dj=RWKV_HEAD_DIM, dup=dup)
        vt = _to_value_layout(v, n_seq, t_len, RWKV_HEADS, RWKV_HEAD_DIM, dup)
        y, s1 = dplr_scan(kl(kk), kl(d), kl(b), kl(k), kl(r), vt, s0, tb_scan)
        y = _from_value_layout(y, n_seq, t_len, RWKV_HEADS, RWKV_HEAD_DIM, dup)
    out = rwkv_post(y, r, k, v, g, prm, tb_prep, row0, dest)
    return out, _state_from_layout(s1, n_seq, RWKV_HEADS, dup, value_major=True)


def gdn_mix(p_gdn, p_swa, conv_state, ssm_state, prm, n_seq, t_len, row0, tb_prep, tb_scan, dest=None):
    st8 = _pad_state_rows(conv_state)
    if t_len % GDN_CHUNK == 0:
        kn, g, beta, v, q = gdn_prep(p_gdn, p_swa, st8, prm, n_seq, t_len, tb_prep, row0, chunked=True)
        nc = 4 if t_len % (4 * GDN_CHUNK) == 0 else 1
        o, s1 = gdn_chunked(q, kn, v, beta, g, ssm_state, n_seq, t_len, hb=8, nc=nc)
        return gdn_post(o, p_gdn, prm, tb_prep, row0, dest), s1
    kn, d, b, v, q = gdn_prep(p_gdn, p_swa, st8, prm, n_seq, t_len, tb_prep, row0, chunked=False)
    dup = 2 * max(1, LANES // (2 * n_seq * GDN_HEADS))
    kl = functools.partial(_to_key_layout, n_seq=n_seq, t_len=t_len, heads=GDN_HEADS, dj=GDN_HEAD_DIM, dup=dup)
    vt = _to_value_layout(v, n_seq, t_len, GDN_HEADS, GDN_HEAD_DIM, dup)
    s0 = _state_to_layout(ssm_state, dup, value_major=False)
    knl = kl(kn)
    o, s1 = dplr_scan(knl, kl(d), kl(b), knl, kl(q), vt, s0, tb_scan)
    o = _from_value_layout(o, n_seq, t_len, GDN_HEADS, GDN_HEAD_DIM, dup)
    out = gdn_post(o, p_gdn, prm, tb_prep, row0, dest)
    return out, _state_from_layout(s1, n_seq, GDN_HEADS, dup, value_major=False)


def _block_ones(seg):
    i = jnp.arange(LANES)
    return (i[:, None] // seg == i[None, :] // seg).astype(BF16)


def _head_expander(first_row):
    rows = jnp.arange(LANES)[:, None]
    cols = jnp.arange(GDN_WIDTH)[None, :] // GDN_HEAD_DIM
    return (rows == cols + first_row).astype(BF16)


def _layer_params(l, W):
    row = lambda x: x.reshape(1, -1)
    zeros = jnp.zeros((DECAY_LORA, RWKV_WIDTH), F32)
    wwa = jnp.concatenate([jnp.concatenate([W["rwkv_w2"][l], zeros], axis=1),
                           jnp.concatenate([zeros, W["rwkv_a2"][l]], axis=1)], axis=0)
    lane_row = lambda x, off: jnp.pad(x, (off, LANES - off - x.shape[0])).reshape(1, LANES)
    rwkv = dict(mix=row(W["rwkv_shift_mix"][l]), w0=row(W["rwkv_w0"][l]), a0=row(W["rwkv_a0"][l]), wwa=wwa,
                g2=W["rwkv_g2"][l], k_k=row(W["rwkv_k_k"][l]), k_a=row(W["rwkv_k_a"][l]),
                r_k=row(W["rwkv_r_k"][l]), ln_w=row(W["rwkv_ln_w"][l]), ln_b=row(W["rwkv_ln_b"][l]),
                j64=_block_ones(RWKV_HEAD_DIM))
    gdn = dict(conv_w=jnp.pad(W["gdn_conv_w"][l], ((0, SUBLANES - CONV_WIDTH), (0, 0))),
               a_log=lane_row(W["gdn_A_log"][l], GDN_HEADS), dt=lane_row(W["gdn_dt_bias"][l], GDN_HEADS),
               e_beta=_head_expander(0), e_gate=_head_expander(GDN_HEADS),
               norm_w=jnp.tile(W["gdn_norm_w"][l], GDN_HEADS).reshape(1, GDN_WIDTH), j128=_block_ones(LANES))
    return rwkv, gdn


def kernel(x_prompt, x_sample, state_rwkv_shift, state_rwkv_wkv, cache_swa_k, cache_swa_v, state_gdn_conv,
           state_gdn_ssm, norm_mix, w_in, rwkv_shift_mix, rwkv_w0, rwkv_w2, rwkv_a0, rwkv_a2, rwkv_g2, rwkv_k_k,
           rwkv_k_a, rwkv_r_k, rwkv_ln_w, rwkv_ln_b, swa_sinks, gdn_conv_w, gdn_A_log, gdn_dt_bias, gdn_norm_w,
           w_out, norm_ffn, w_gate, w_up, w_down, final_norm):
    W = dict(rwkv_shift_mix=rwkv_shift_mix, rwkv_w0=rwkv_w0, rwkv_w2=rwkv_w2, rwkv_a0=rwkv_a0, rwkv_a2=rwkv_a2,
             rwkv_g2=rwkv_g2, rwkv_k_k=rwkv_k_k, rwkv_k_a=rwkv_k_a, rwkv_r_k=rwkv_r_k, rwkv_ln_w=rwkv_ln_w,
             rwkv_ln_b=rwkv_ln_b, gdn_conv_w=gdn_conv_w, gdn_A_log=gdn_A_log, gdn_dt_bias=gdn_dt_bias,
             gdn_norm_w=gdn_norm_w)
    w_in_b, w_out_b, w_gate_b, w_up_b, w_down_b = (w.astype(BF16) for w in (w_in, w_out, w_gate, w_up, w_down))
    swa_col, gdn_col, ba_col = RWKV_PROJ, RWKV_PROJ + SWA_PROJ, RWKV_PROJ + SWA_PROJ + GDN_MAIN

    u, h = rmsnorm_join(x_prompt.reshape(ROWS_P, D_MODEL), x_sample.reshape(ROWS_D, D_MODEL), norm_mix[0], BF16)
    zero_shift = jnp.zeros((BATCH, RWKV_PROJ), F32)
    zero_wkv = jnp.zeros((BATCH, RWKV_HEADS, RWKV_HEAD_DIM, RWKV_HEAD_DIM), F32)
    zero_conv = jnp.zeros((BATCH, CONV_WIDTH - 1, GDN_CONV_DIM), F32)
    zero_ssm = jnp.zeros((BATCH, GDN_HEADS, GDN_HEAD_DIM, GDN_HEAD_DIM), F32)
    wb = cache_swa_k.shape[2]
    assert wb == WINDOW
    outs = [[] for _ in range(12)]
    bm = 1056
    for l in range(DEPTH):
        rw, gd = _layer_params(l, W)
        if l > 0:
            u = rmsnorm(h, norm_mix[l], BF16)
        p_rwkv = matmul(u, w_in_b, l, 2 * bm, 256, 0, RWKV_PROJ)
        p_swa = matmul(u, w_in_b, l, 2 * bm, 256, swa_col, SWA_PROJ)
        p_gdn = matmul(u, w_in_b, l, 2 * bm, 512, gdn_col, GDN_MAIN)
        p_ba = matmul(u, w_in_b, l, bm, LANES, ba_col, LANES)

        o_r, wkv_p = rwkv_mix(p_rwkv, zero_shift, zero_wkv, rw, BATCH, SEQ, 0, 128, 32, dest=ROWS)
        o_r, wkv_d = rwkv_mix(p_rwkv, state_rwkv_shift[l], state_rwkv_wkv[l], rw, DEC_BATCH, DEC_SEQ, ROWS_P,
                              DEC_SEQ, DEC_SEQ, dest=o_r)
        o_s = swa_attention(swa_sinks[l], p_swa, BATCH, SEQ, WINDOW, 0, dest=ROWS)
        o_s = swa_attention(swa_sinks[l], p_swa, DEC_BATCH, DEC_SEQ, DEC_SEQ, ROWS_P,
                            cache_swa_k[l].reshape(DEC_BATCH * wb, SWA_KV_WIDTH),
                            cache_swa_v[l].reshape(DEC_BATCH * wb, SWA_KV_WIDTH), dest=o_s)
        o_g, ssm_p = gdn_mix(p_gdn, p_ba, zero_conv, zero_ssm, gd, BATCH, SEQ, 0, 64, 32, dest=ROWS)
        o_g, ssm_d = gdn_mix(p_gdn, p_ba, state_gdn_conv[l], state_gdn_ssm[l], gd, DEC_BATCH, DEC_SEQ, ROWS_P,
                             DEC_SEQ, DEC_SEQ, dest=o_g)
        h = matmul_residual([o_r, o_s, o_g], w_out_b, l, h, bm, 512, [RWKV_WIDTH, SWA_WIDTH, GDN_WIDTH])
        u = rmsnorm(h, norm_ffn[l], BF16)
        h1 = swiglu_matmul(u, w_gate_b, w_up_b, l, bm, 512)
        h = matmul_residual([h1], w_down_b, l, h, bm, 512, [D_FF // 2])

        def tail_p(p, n_rows, c0, c1):
            return jnp.stack([lax.slice(p, ((b + 1) * SEQ - n_rows, c0), ((b + 1) * SEQ, c1)) for b in range(BATCH)])

        def tail_d(p, n_rows, c0, c1):
            x = lax.slice(p, (ROWS_P, c0), (ROWS, c1)).reshape(DEC_BATCH, DEC_SEQ, c1 - c0)
            return x[:, DEC_SEQ - n_rows:]

        kv = lambda x: x.reshape(x.shape[0], x.shape[1], SWA_KV_HEADS, SWA_HEAD_DIM)
        k0, k1, v1 = SWA_WIDTH, SWA_WIDTH + SWA_KV_WIDTH, SWA_PROJ
        layer_out = (
            tail_p(p_rwkv, 1, 0, RWKV_PROJ)[:, 0], wkv_p, kv(tail_p(p_swa, wb, k0, k1)), kv(tail_p(p_swa, wb, k1, v1)),
            tail_p(p_gdn, CONV_WIDTH - 1, 0, GDN_CONV_DIM), ssm_p,
            tail_d(p_rwkv, 1, 0, RWKV_PROJ)[:, 0], wkv_d,
            jnp.concatenate([cache_swa_k[l], kv(tail_d(p_swa, DEC_SEQ, k0, k1))], axis=1)[:, -wb:],
            jnp.concatenate([cache_swa_v[l], kv(tail_d(p_swa, DEC_SEQ, k1, v1))], axis=1)[:, -wb:],
            tail_d(p_gdn, CONV_WIDTH - 1, 0, GDN_CONV_DIM), ssm_d)
        for lst, t in zip(outs, layer_out):
            lst.append(t)
    y = rmsnorm(h, final_norm, F32)
    y_prompt = y[:ROWS_P].reshape(BATCH, SEQ, D_MODEL)
    y_sample = y[ROWS_P:].reshape(DEC_BATCH, DEC_SEQ, D_MODEL)
    return (y_prompt, y_sample) + tuple(jnp.stack(lst) for lst in outs)
```

```python
import functools

import jax
import jax.numpy as jnp
from jax import lax
from jax.experimental import pallas as pl
from jax.experimental.pallas import tpu as pltpu

D_MODEL = 4096
BATCH = 4
SEQ = 2048
DEPTH = 4
DEC_BATCH = 32
DEC_SEQ = 8
NORM_EPS = 1e-6
L2_EPS = 1e-6
RWKV_WIDTH = 1024
RWKV_HEAD_DIM = 64
RWKV_HEADS = 16
DECAY_LORA = 64
AAA_LORA = 64
GATE_LORA = 128
RWKV_PROJ = 3 * RWKV_WIDTH + DECAY_LORA + AAA_LORA + GATE_LORA
RWKV_LN_EPS = 64e-5
SWA_WIDTH = 1024
SWA_HEAD_DIM = 64
SWA_HEADS = 16
SWA_KV_HEADS = 2
SWA_GROUP = 8
SWA_KV_WIDTH = 128
SWA_PROJ = SWA_WIDTH + 2 * SWA_KV_WIDTH
WINDOW = 128
SWA_SCALE = SWA_HEAD_DIM ** -0.5
GDN_WIDTH = 2048
GDN_HEAD_DIM = 128
GDN_HEADS = 16
GDN_CONV_DIM = 3 * GDN_WIDTH
CONV_WIDTH = 4
GDN_CHUNK = 64
GDN_MAIN = GDN_CONV_DIM + GDN_WIDTH
D_FF = 11008

ROWS_P = BATCH * SEQ
ROWS_D = DEC_BATCH * DEC_SEQ
ROWS = ROWS_P + ROWS_D

LANES = 128
SUBLANES = 8
VMEM_LIMIT = 56 * 1024 * 1024

F32 = jnp.float32
BF16 = jnp.bfloat16


def _cparams(n_axes):
    return pltpu.CompilerParams(dimension_semantics=("arbitrary",) * n_axes, vmem_limit_bytes=VMEM_LIMIT)


def _split(x):
    hi = x.astype(BF16)
    lo = (x - hi.astype(F32)).astype(BF16)
    return hi, lo


def _dot(a, b):
    return jnp.dot(a, b, preferred_element_type=F32)


def _dot_lhs2(x, m_bf16):
    hi, lo = _split(x)
    return _dot(hi, m_bf16) + _dot(lo, m_bf16)


def _dot3(x, w):
    xh, xl = _split(x)
    wh, wl = _split(w)
    return _dot(xh, wh) + (_dot(xl, wh) + _dot(xh, wl))


def _segsum(x, j_bf16):
    n = x.shape[-1] // LANES
    return jnp.concatenate([_dot_lhs2(x[:, c * LANES:(c + 1) * LANES], j_bf16) for c in range(n)], axis=-1)


def _sigmoid(x):
    return 1.0 / (1.0 + jnp.exp(-x))


def _softplus(x):
    return jnp.maximum(x, 0.0) + jnp.log(1.0 + jnp.exp(-jnp.abs(x)))


def _rmsnorm_kernel(x_ref, w_ref, o_ref):
    x = x_ref[...]
    ms = jnp.mean(x * x, axis=-1, keepdims=True)
    o_ref[...] = (x * lax.rsqrt(ms + NORM_EPS) * w_ref[...]).astype(o_ref.dtype)


def rmsnorm(x, w, out_dtype, br=256):
    r, d = x.shape
    return pl.pallas_call(
        _rmsnorm_kernel,
        grid=(r // br,),
        in_specs=[pl.BlockSpec((br, d), lambda i: (i, 0)), pl.BlockSpec((1, d), lambda i: (0, 0))],
        out_specs=pl.BlockSpec((br, d), lambda i: (i, 0)),
        out_shape=jax.ShapeDtypeStruct((r, d), out_dtype),
        compiler_params=_cparams(1),
        name="rmsnorm",
    )(x, w.reshape(1, d))


def _rmsnorm_join_kernel(xp_ref, xs_ref, w_ref, o_ref, h_ref, *, n_p):
    x = jnp.where(pl.program_id(0) < n_p, xp_ref[...], xs_ref[...])
    h_ref[...] = x
    ms = jnp.mean(x * x, axis=-1, keepdims=True)
    o_ref[...] = (x * lax.rsqrt(ms + NORM_EPS) * w_ref[...]).astype(o_ref.dtype)


def rmsnorm_join(xp, xs, w, out_dtype):
    (rp, d), rs = xp.shape, xs.shape[0]
    br = rs
    assert rp % br == 0
    n_p = rp // br
    spec = pl.BlockSpec((br, d), lambda i: (i, 0))
    return pl.pallas_call(
        functools.partial(_rmsnorm_join_kernel, n_p=n_p),
        grid=(n_p + 1,),
        in_specs=[pl.BlockSpec((br, d), lambda i: (jnp.minimum(i, n_p - 1), 0)),
                  pl.BlockSpec((br, d), lambda i: (0, 0)), pl.BlockSpec((1, d), lambda i: (0, 0))],
        out_specs=[spec, spec],
        out_shape=[jax.ShapeDtypeStruct((rp + rs, d), out_dtype), jax.ShapeDtypeStruct((rp + rs, d), xp.dtype)],
        compiler_params=_cparams(1),
        name="rmsnorm_join",
    )(xp, xs, w.reshape(1, d))


def _mm_kernel(a_ref, b_ref, o_ref):
    o_ref[...] = _dot(a_ref[...], b_ref[...]).astype(o_ref.dtype)


def matmul(a, w, layer, bm, bn, col0, n_out, out_dtype=F32):
    r, k = a.shape
    assert col0 % bn == 0 and n_out % bn == 0
    return pl.pallas_call(
        _mm_kernel,
        grid=(r // bm, n_out // bn),
        in_specs=[pl.BlockSpec((bm, k), lambda i, j: (i, 0)),
                  pl.BlockSpec((None, k, bn), lambda i, j: (layer, 0, col0 // bn + j))],
        out_specs=pl.BlockSpec((bm, bn), lambda i, j: (i, j)),
        out_shape=jax.ShapeDtypeStruct((r, n_out), out_dtype),
        compiler_params=_cparams(2),
        name="proj_in",
    )(a, w)


def _swiglu_kernel(a_ref, g_ref, u_ref, o_ref):
    a = a_ref[...]
    g = _dot(a, g_ref[...])
    u = _dot(a, u_ref[...])
    o_ref[...] = (g * _sigmoid(g) * u).astype(o_ref.dtype)


def swiglu_matmul(a, wg, wu, layer, bm, bn):
    r, k = a.shape
    n = wg.shape[2]
    wspec = pl.BlockSpec((None, k, bn), lambda i, j: (layer, 0, j))
    return pl.pallas_call(
        _swiglu_kernel,
        grid=(r // bm, pl.cdiv(n, bn)),
        in_specs=[pl.BlockSpec((bm, k), lambda i, j: (i, 0)), wspec, wspec],
        out_specs=pl.BlockSpec((bm, bn), lambda i, j: (i, j)),
        out_shape=jax.ShapeDtypeStruct((r, n), BF16),
        compiler_params=_cparams(2),
        name="ffn_swiglu",
    )(a, wg, wu)


def _mm_res_kernel(*refs, n_a):
    a_refs, b_refs, res_ref, o_ref = refs[:n_a], refs[n_a:2 * n_a], refs[2 * n_a], refs[2 * n_a + 1]
    kk = pl.program_id(2)

    @pl.when(kk == 0)
    def _():
        o_ref[...] = res_ref[...]

    acc = _dot(a_refs[0][...], b_refs[0][...])
    for a_ref, b_ref in zip(a_refs[1:], b_refs[1:]):
        acc = acc + _dot(a_ref[...], b_ref[...])
    o_ref[...] += acc


def matmul_residual(a_list, w, layer, res, bm, bn, bk_list):
    r = res.shape[0]
    n = w.shape[2]
    nk = a_list[0].shape[1] // bk_list[0]
    in_specs, row0 = [], 0
    for a, bk in zip(a_list, bk_list):
        assert a.shape[1] == nk * bk
        in_specs.append(pl.BlockSpec((bm, bk), lambda i, j, k: (i, k)))
    for a, bk in zip(a_list, bk_list):
        assert row0 % bk == 0
        in_specs.append(pl.BlockSpec((None, bk, bn), functools.partial(
            lambda i, j, k, off: (layer, off + k, j), off=row0 // bk)))
        row0 += a.shape[1]
    in_specs.append(pl.BlockSpec((bm, bn), lambda i, j, k: (i, j)))
    return pl.pallas_call(
        functools.partial(_mm_res_kernel, n_a=len(a_list)),
        grid=(r // bm, n // bn, nk),
        in_specs=in_specs,
        out_specs=pl.BlockSpec((bm, bn), lambda i, j, k: (i, j)),
        out_shape=jax.ShapeDtypeStruct((r, n), F32),
        compiler_params=_cparams(3),
        name="proj_residual",
    )(*a_list, *([w] * len(a_list)), res)


def _shifted(prev8, x, k):
    ext = jnp.concatenate([prev8, x], axis=0)
    return pltpu.roll(ext, k, axis=0)[SUBLANES:]


def _seq_specs(width, col_block, tb, n_tb, row0):
    assert row0 % tb == 0 and tb % SUBLANES == 0
    cur = pl.BlockSpec((tb, width), lambda b, t: (row0 // tb + b * n_tb + t, col_block))
    prev = pl.BlockSpec((SUBLANES, width), lambda b, t: (
        jnp.maximum((row0 + (b * n_tb + t) * tb) // SUBLANES - 1, 0), col_block))
    state = pl.BlockSpec((SUBLANES, width), lambda b, t: (b, 0))
    return cur, prev, state


def _row_spec(width, tb, n_tb, row0=0, col_block=0):
    return pl.BlockSpec((tb, width), lambda b, t: (row0 // tb + b * n_tb + t, col_block))


def _const_spec(shape):
    return pl.BlockSpec(shape, lambda b, t: (0,) * len(shape))


def _drop_ref(kernel_fn, pos, *refs):
    return kernel_fn(*refs[:pos], *refs[pos + 1:])


def _shared_rows_out(kernel_fn, n_in, rows, width, dtype, dest):
    if dest is None or isinstance(dest, int):
        return kernel_fn, jax.ShapeDtypeStruct((dest or rows, width), dtype), (), [], {}
    assert dest.shape[1] == width and dest.dtype == dtype
    return (functools.partial(_drop_ref, kernel_fn, n_in), jax.ShapeDtypeStruct(dest.shape, dtype), (dest,),
            [pl.BlockSpec(memory_space=pl.ANY)], {n_in: 0})


def _pad_state_rows(st):
    b, k, c = st.shape
    return jnp.pad(st, ((0, 0), (SUBLANES - k, 0), (0, 0))).reshape(b * SUBLANES, c)


def _rwkv_prep_kernel(cur_ref, prev_ref, st_ref, mix_ref, w0_ref, a0_ref, wwa_ref, g2_ref, kkw_ref, kaw_ref, j_ref,
                      r_o, kk_o, d_o, b_o, k_o, v_o, g_o):
    x = cur_ref[...]
    prev8 = jnp.where(pl.program_id(1) == 0, st_ref[...], prev_ref[...])
    xs = x + (_shifted(prev8, x, 1) - x) * mix_ref[...]
    w3 = RWKV_WIDTH
    r, k, v = xs[:, 0:w3], xs[:, w3:2 * w3], xs[:, 2 * w3:3 * w3]
    wa = xs[:, 3 * w3:3 * w3 + LANES]
    glo = xs[:, 3 * w3 + LANES:3 * w3 + 2 * LANES]
    lane = lax.broadcasted_iota(jnp.int32, wa.shape, 1)
    lora = _dot3(jnp.where(lane < DECAY_LORA, jnp.tanh(wa), wa), wwa_ref[...])
    w = -_softplus(-(w0_ref[...] + lora[:, :w3])) - 0.5
    d = jnp.exp(-jnp.exp(w))
    a = _sigmoid(a0_ref[...] + lora[:, w3:])
    g = _dot3(_sigmoid(glo), g2_ref[...])
    kn = k * kkw_ref[...]
    kk = kn * lax.rsqrt(_segsum(kn * kn, j_ref[...]) + L2_EPS)
    r_o[...] = r
    kk_o[...] = kk
    d_o[...] = d
    b_o[...] = kk * a
    k_o[...] = k * (1.0 + (a - 1.0) * kaw_ref[...])
    v_o[...] = v
    g_o[...] = g


def rwkv_prep(p_rwkv, st8, prm, n_seq, t_len, tb, row0):
    n_tb = t_len // tb
    cur, prev, state = _seq_specs(RWKV_PROJ, 0, tb, n_tb, row0)
    out = jax.ShapeDtypeStruct((n_seq * t_len, RWKV_WIDTH), F32)
    ospec = _row_spec(RWKV_WIDTH, tb, n_tb)
    return pl.pallas_call(
        _rwkv_prep_kernel,
        grid=(n_seq, n_tb),
        in_specs=[cur, prev, state, _const_spec((1, RWKV_PROJ)), _const_spec((1, RWKV_WIDTH)),
                  _const_spec((1, RWKV_WIDTH)), _const_spec((LANES, 2 * RWKV_WIDTH)),
                  _const_spec((GATE_LORA, RWKV_WIDTH)), _const_spec((1, RWKV_WIDTH)), _const_spec((1, RWKV_WIDTH)),
                  _const_spec((LANES, LANES))],
        out_specs=[ospec] * 7,
        out_shape=[out] * 7,
        compiler_params=_cparams(2),
        name="rwkv_prep",
    )(p_rwkv, p_rwkv, st8, prm["mix"], prm["w0"], prm["a0"], prm["wwa"], prm["g2"], prm["k_k"], prm["k_a"],
      prm["j64"])


def _rwkv_post_kernel(y_ref, r_ref, k_ref, v_ref, g_ref, lnw_ref, lnb_ref, rk_ref, j_ref, o_ref):
    j = j_ref[...]
    y = y_ref[...]
    inv_n = 1.0 / RWKV_HEAD_DIM
    yc = y - _segsum(y, j) * inv_n
    var = _segsum(yc * yc, j) * inv_n
    out = yc * lax.rsqrt(var + RWKV_LN_EPS) * lnw_ref[...] + lnb_ref[...]
    out = out + _segsum(r_ref[...] * k_ref[...] * rk_ref[...], j) * v_ref[...]
    o_ref[...] = (out * g_ref[...]).astype(o_ref.dtype)


def rwkv_post(y, r, k, v, g, prm, tb, row0, dest):
    rows = y.shape[0]
    spec = pl.BlockSpec((tb, RWKV_WIDTH), lambda i: (i, 0))
    cspec = pl.BlockSpec((1, RWKV_WIDTH), lambda i: (0, 0))
    kern, out_shape, extra, extra_specs, aliases = _shared_rows_out(_rwkv_post_kernel, 9, rows, RWKV_WIDTH, BF16, dest)
    return pl.pallas_call(
        kern,
        grid=(rows // tb,),
        in_specs=[spec] * 5 + [cspec] * 3 + [pl.BlockSpec((LANES, LANES), lambda i: (0, 0))] + extra_specs,
        out_specs=pl.BlockSpec((tb, RWKV_WIDTH), lambda i: ((0 if dest is None else row0 // tb) + i, 0)),
        out_shape=out_shape,
        input_output_aliases=aliases,
        compiler_params=_cparams(1),
        name="rwkv_post",
    )(y, r, k, v, g, prm["ln_w"], prm["ln_b"], prm["r_k"], prm["j64"], *extra)


def _scan_kernel(kk_ref, d_ref, b_ref, k_ref, r_ref, v_ref, s0_ref, y_ref, s_ref, *scratch, tb, dj, di, lane_dup):
    ng = di // SUBLANES
    half = LANES // 2

    @pl.when(pl.program_id(1) == 0)
    def _():
        s_ref[...] = s0_ref[...]

    if lane_dup:
        key_refs, v_s, y_s = scratch[:5], scratch[5], scratch[6]
        n_seq = kk_ref.shape[0]
        heads = half // n_seq
        assert dj == half and 2 * di == half
        zpad = jnp.zeros((LANES, half), F32)

        def head_rows(ref, t):
            return ref[:, pl.ds(pl.multiple_of(t * heads, heads), heads), :].reshape(half, half)

        low = lax.broadcasted_iota(jnp.int32, (half, LANES), 1) < half
        low_v = lax.broadcasted_iota(jnp.int32, (di, LANES), 1) < half

        def fill(t, carry):
            rows = pl.ds(pl.multiple_of(t * dj, dj), dj)

            def pair(ref_a, ref_b):
                w = jnp.concatenate([head_rows(ref_a, t), head_rows(ref_b, t)], axis=0)
                z = jnp.concatenate([w, zpad], axis=1).T[:half]
                return z, pltpu.roll(z, half, axis=1)

            for (ref_a, dst_a), (ref_b, dst_b) in (((kk_ref, key_refs[0]), (d_ref, key_refs[1])),
                                                   ((b_ref, key_refs[2]), (k_ref, key_refs[3]))):
                z, zr = pair(ref_a, ref_b)
                dst_a[rows, :] = jnp.where(low, z, zr)
                dst_b[rows, :] = jnp.where(low, zr, z)
            z, zr = pair(r_ref, v_ref)
            key_refs[4][rows, :] = jnp.where(low, z, zr)
            v_s[pl.ds(pl.multiple_of(t * di, di), di), :] = jnp.where(low_v, zr[:di], z[di:2 * di])
            return carry

        lax.fori_loop(0, tb, fill, 0, unroll=8)
        kk_s, d_s, b_s, k_s, r_s = key_refs

        def row(ref, t, j):
            return ref[pl.ds(t * dj + j, 1), :]

        def vrows(t, g):
            return pl.ds(pl.multiple_of(t * di + g * SUBLANES, SUBLANES), SUBLANES)

        load_v = lambda t, g: v_s[vrows(t, g), :]

        def store_y(t, g, val):
            y_s[vrows(t, g), :] = val
    else:
        kk_s, d_s, b_s, k_s, r_s = kk_ref, d_ref, b_ref, k_ref, r_ref

        def row(ref, t, j):
            return ref[0, t, pl.ds(j, 1), :]

        load_v = lambda t, g: v_ref[0, t, pl.ds(g * SUBLANES, SUBLANES), :]

        def store_y(t, g, val):
            y_ref[0, t, pl.ds(g * SUBLANES, SUBLANES), :] = val

    def sl(g):
        return pl.ds(g * SUBLANES, SUBLANES)

    zero = tuple(jnp.zeros((SUBLANES, LANES), F32) for _ in range(ng))

    def first_dot(j, acc):
        kkj = row(kk_s, 0, j)
        return tuple(acc[g] + s_ref[0, j, sl(g), :] * kkj for g in range(ng))

    def step(t, s_kk):
        sa = [-a for a in s_kk]
        v = [load_v(t, g) for g in range(ng)]
        t_next = jnp.minimum(t + 1, tb - 1)

        def update(j, carry):
            yacc, nacc = carry
            dj_, bj, kj, rj = row(d_s, t, j), row(b_s, t, j), row(k_s, t, j), row(r_s, t, j)
            kkn = row(kk_s, t_next, j)
            y_out, n_out = [], []
            for g in range(ng):
                s = s_ref[0, j, sl(g), :] * dj_ + sa[g] * bj + v[g] * kj
                s_ref[0, j, sl(g), :] = s
                y_out.append(yacc[g] + s * rj)
                n_out.append(nacc[g] + s * kkn)
            return tuple(y_out), tuple(n_out)

        yacc, nacc = lax.fori_loop(0, dj, update, (zero, zero), unroll=8)
        for g in range(ng):
            store_y(t, g, yacc[g])
        return nacc

    lax.fori_loop(0, tb, step, lax.fori_loop(0, dj, first_dot, zero, unroll=8))
    if lane_dup:
        zrows = jnp.zeros((LANES - 2 * di, LANES), F32)

        def drain(t, carry):
            ys = y_s[pl.ds(pl.multiple_of(t * di, di), di), :]
            ym = jnp.concatenate([ys, pltpu.roll(ys, half, axis=1), zrows], axis=0)
            y_ref[:, pl.ds(pl.multiple_of(t * heads, heads), heads), :] = ym.T[:half, :half].reshape(n_seq, heads, half)
            return carry

        lax.fori_loop(0, tb, drain, 0, unroll=8)


def dplr_scan(kk, d, b, k, r, v, s0, tb):
    lane_dup = kk.ndim == 3
    dj, di = s0.shape[1], s0.shape[2]
    if lane_dup:
        n_seq, rows, width = kk.shape
        heads = LANES // 2 // n_seq
        g_n, t_len = 1, rows // heads
        assert width == LANES // 2 and n_seq * heads * 2 == LANES and v.shape == kk.shape
        jspec = ispec = pl.BlockSpec((n_seq, tb * heads, width), lambda g, t: (0, t, 0))
        scratch = [pltpu.VMEM((tb * dj, LANES), F32)] * 5 + [pltpu.VMEM((tb * di, LANES), F32)] * 2
    else:
        g_n, t_len = kk.shape[:2]
        assert kk.shape[2:] == (dj, LANES) and v.shape[2:] == (di, LANES)
        jspec = pl.BlockSpec((1, tb, dj, LANES), lambda g, t: (g, t, 0, 0))
        ispec = pl.BlockSpec((1, tb, di, LANES), lambda g, t: (g, t, 0, 0))
        scratch = []
    sspec = pl.BlockSpec((1, dj, di, LANES), lambda g, t: (g, 0, 0, 0))
    return pl.pallas_call(
        functools.partial(_scan_kernel, tb=tb, dj=dj, di=di, lane_dup=lane_dup),
        grid=(g_n, t_len // tb),
        in_specs=[jspec] * 5 + [ispec, sspec],
        out_specs=[ispec, sspec],
        out_shape=[jax.ShapeDtypeStruct(v.shape, F32), jax.ShapeDtypeStruct(s0.shape, F32)],
        scratch_shapes=scratch,
        compiler_params=_cparams(2),
        name="dplr_scan",
    )(kk, d, b, k, r, v, s0)


def _to_key_layout(x, n_seq, t_len, heads, dj, dup):
    x = jnp.broadcast_to(x.reshape(1, n_seq, t_len, heads, dj), (dup, n_seq, t_len, heads, dj))
    x = x.transpose(2, 4, 0, 1, 3)
    g_n = dup * n_seq * heads // LANES
    return x.reshape(t_len, dj, g_n, LANES).transpose(2, 0, 1, 3)


def _to_value_layout(x, n_seq, t_len, heads, dv, dup):
    di = dv // dup
    x = x.reshape(n_seq, t_len, heads, dup, di).transpose(1, 4, 3, 0, 2)
    g_n = dup * n_seq * heads // LANES
    return x.reshape(t_len, di, g_n, LANES).transpose(2, 0, 1, 3)


def _from_value_layout(y, n_seq, t_len, heads, dv, dup):
    g_n, _, di, _ = y.shape
    y = y.transpose(1, 2, 0, 3).reshape(t_len, di, dup, n_seq, heads)
    return y.transpose(3, 0, 4, 2, 1).reshape(n_seq * t_len, heads * dv)


def _state_to_layout(s, dup, value_major):
    bsz, heads = s.shape[:2]
    if value_major:
        dv, dj = s.shape[2:]
        s = s.reshape(bsz, heads, dup, dv // dup, dj).transpose(4, 3, 2, 0, 1)
    else:
        dj, dv = s.shape[2:]
        s = s.reshape(bsz, heads, dj, dup, dv // dup).transpose(2, 4, 3, 0, 1)
    g_n = dup * bsz * heads // LANES
    return s.reshape(dj, dv // dup, g_n, LANES).transpose(2, 0, 1, 3)


def _state_from_layout(s, bsz, heads, dup, value_major):
    g_n, dj, di, _ = s.shape
    s = s.transpose(1, 2, 0, 3).reshape(dj, di, dup, bsz, heads)
    if value_major:
        return s.transpose(3, 4, 2, 1, 0).reshape(bsz, heads, dup * di, dj)
    return s.transpose(3, 4, 0, 2, 1).reshape(bsz, heads, dj, dup * di)


def _swa_kernel(sink_ref, q_ref, kc_ref, vc_ref, kp_ref, vp_ref, o_ref, *, tq, has_cache):
    s_len = WINDOW + tq
    kfull = jnp.concatenate([kp_ref[...], kc_ref[...]], axis=0)
    vfull = jnp.concatenate([vp_ref[...], vc_ref[...]], axis=0)
    lane = lax.broadcasted_iota(jnp.int32, kfull.shape, 1)
    low = lane < SWA_HEAD_DIM
    kswap = pltpu.roll(kfull, SWA_HEAD_DIM, axis=1)
    vswap = pltpu.roll(vfull, SWA_HEAD_DIM, axis=1)
    k_lo = [jnp.where(low, kfull, 0.0), jnp.where(low, kswap, 0.0)]
    k_hi = [jnp.where(low, 0.0, kswap), jnp.where(low, 0.0, kfull)]
    v_lo = [jnp.where(low, vfull, 0.0), jnp.where(low, vswap, 0.0)]
    v_hi = [jnp.where(low, 0.0, vswap), jnp.where(low, 0.0, vfull)]
    t_idx = lax.broadcasted_iota(jnp.int32, (tq, s_len), 0)
    s_idx = lax.broadcasted_iota(jnp.int32, (tq, s_len), 1)
    delta = WINDOW + t_idx - s_idx
    valid = (delta >= 0) & (delta <= WINDOW)
    if not has_cache:
        valid = valid & ((s_idx >= WINDOW) | (pl.program_id(1) > 0))
    contract_last = (((1,), (1,)), ((), ()))
    all_heads = [(hp, half) for hp in range(SWA_HEADS // 2) for half in range(2)]
    group = len(all_heads) if tq <= 2 * SUBLANES else 1
    o_done = {}
    for h0 in range(0, len(all_heads), group):
        heads = all_heads[h0:h0 + group]
        mats = {(hp, half): ((k_lo, v_lo), (k_hi, v_hi))[half] for hp, half in heads}
        s_all = [lax.dot_general(q_ref[:, hp * LANES:(hp + 1) * LANES], mats[hp, half][0][(2 * hp) // SWA_GROUP],
                                 contract_last, preferred_element_type=F32) * SWA_SCALE for hp, half in heads]
        e_all, den_all = [], []
        for (hp, half), s in zip(heads, s_all):
            sink = sink_ref[2 * hp + half]
            s = jnp.where(valid, s, -jnp.inf)
            m = jnp.maximum(jnp.max(s, axis=-1, keepdims=True), sink)
            e = jnp.exp(s - m)
            e_all.append(e)
            den_all.append(jnp.sum(e, axis=-1, keepdims=True) + jnp.exp(sink - m))
        for (hp, half), e, den in zip(heads, e_all, den_all):
            o_done[hp, half] = _dot(e, mats[hp, half][1][(2 * hp) // SWA_GROUP]) / den
        for hp in sorted({hp for hp, half in heads if half == 1}):
            o_ref[:, hp * LANES:(hp + 1) * LANES] = (o_done.pop((hp, 0)) + o_done.pop((hp, 1))).astype(o_ref.dtype)


def swa_attention(sinks, p_swa, n_seq, t_len, tq, row0, cache_k=None, cache_v=None, dest=None):
    n_blk = t_len // tq
    has_cache = cache_k is not None
    kcol, vcol = SWA_WIDTH // LANES, SWA_WIDTH // LANES + 1
    q_spec = _row_spec(SWA_WIDTH, tq, n_blk, row0)
    kc_spec = _row_spec(LANES, tq, n_blk, row0, kcol)
    vc_spec = _row_spec(LANES, tq, n_blk, row0, vcol)
    if has_cache:
        kp_spec = vp_spec = pl.BlockSpec((WINDOW, LANES), lambda b, t: (b, 0))
        kp_arr, vp_arr = cache_k, cache_v
    else:
        assert tq == WINDOW and row0 == 0

        def prev_rows(b, t):
            return jnp.maximum(b * n_blk + t - 1, 0)

        kp_spec = pl.BlockSpec((WINDOW, LANES), lambda b, t: (prev_rows(b, t), kcol))
        vp_spec = pl.BlockSpec((WINDOW, LANES), lambda b, t: (prev_rows(b, t), vcol))
        kp_arr = vp_arr = p_swa
    kern, out_shape, extra, extra_specs, aliases = _shared_rows_out(
        functools.partial(_swa_kernel, tq=tq, has_cache=has_cache), 6, n_seq * t_len, SWA_WIDTH, BF16, dest)
    return pl.pallas_call(
        kern,
        grid=(n_seq, n_blk),
        in_specs=[pl.BlockSpec(memory_space=pltpu.SMEM), q_spec, kc_spec, vc_spec, kp_spec, vp_spec] + extra_specs,
        out_specs=_row_spec(SWA_WIDTH, tq, n_blk, 0 if dest is None else row0),
        out_shape=out_shape,
        input_output_aliases=aliases,
        compiler_params=_cparams(2),
        name="swa",
    )(sinks, p_swa, p_swa, p_swa, kp_arr, vp_arr, *extra)


def _gdn_prep_kernel(cur_ref, prev_ref, st_ref, ba_ref, cw_ref, alog_ref, dt_ref, eb_ref, eg_ref, j_ref,
                     kn_o, d_o, b_o, v_o, q_o, *, chunked):
    x = cur_ref[...]
    prev8 = jnp.where(pl.program_id(1) == 0, st_ref[...], prev_ref[...])
    y = x * cw_ref[3:4, :]
    for k in range(1, CONV_WIDTH):
        y = y + _shifted(prev8, x, k) * cw_ref[3 - k:4 - k, :]
    act = y * _sigmoid(y)
    w2 = GDN_WIDTH
    q, k, v = act[:, :w2], act[:, w2:2 * w2], act[:, 2 * w2:]
    j = j_ref[...]
    qn = q * lax.rsqrt(_segsum(q * q, j) + L2_EPS) * (GDN_HEAD_DIM ** -0.5)
    kn = k * lax.rsqrt(_segsum(k * k, j) + L2_EPS)
    ba = ba_ref[...]
    ba = jnp.where(lax.broadcasted_iota(jnp.int32, ba.shape, 1) < 2 * GDN_HEADS, ba, 0.0)
    beta = _dot_lhs2(_sigmoid(ba), eb_ref[...])
    gate = _dot_lhs2(-jnp.exp(alog_ref[...]) * _softplus(ba + dt_ref[...]), eg_ref[...])
    kn_o[...] = kn
    q_o[...] = qn
    if chunked:
        d_o[...] = gate
        b_o[...] = beta
        v_o[...] = v
    else:
        eg = jnp.exp(gate)
        d_o[...] = eg
        b_o[...] = eg * beta * kn
        v_o[...] = beta * v


def gdn_prep(p_gdn, p_swa, st8, prm, n_seq, t_len, tb, row0, chunked):
    n_tb = t_len // tb
    cur, prev, state = _seq_specs(GDN_CONV_DIM, 0, tb, n_tb, row0)
    ba_spec = _row_spec(LANES, tb, n_tb, row0)
    out = jax.ShapeDtypeStruct((n_seq * t_len, GDN_WIDTH), F32)
    ospec = _row_spec(GDN_WIDTH, tb, n_tb)
    return pl.pallas_call(
        functools.partial(_gdn_prep_kernel, chunked=chunked),
        grid=(n_seq, n_tb),
        in_specs=[cur, prev, state, ba_spec, _const_spec((SUBLANES, GDN_CONV_DIM)), _const_spec((1, LANES)),
                  _const_spec((1, LANES)), _const_spec((LANES, GDN_WIDTH)),
                  _const_spec((LANES, GDN_WIDTH)), _const_spec((LANES, LANES))],
        out_specs=[ospec] * 5,
        out_shape=[out] * 5,
        compiler_params=_cparams(2),
        name="gdn_prep",
    )(p_gdn, p_gdn, st8, p_swa, prm["conv_w"], prm["a_log"], prm["dt"], prm["e_beta"], prm["e_gate"], prm["j128"])


def _gdn_post_kernel(o_ref_in, z_ref, w_ref, j_ref, o_ref):
    o = o_ref_in[...]
    z = z_ref[...]
    ms = _segsum(o * o, j_ref[...]) * (1.0 / GDN_HEAD_DIM)
    y = o * lax.rsqrt(ms + NORM_EPS) * w_ref[...]
    o_ref[...] = (y * (z * _sigmoid(z))).astype(o_ref.dtype)


def gdn_post(o, p_gdn, prm, tb, row0, dest):
    rows = o.shape[0]
    spec = pl.BlockSpec((tb, GDN_WIDTH), lambda i: (i, 0))
    zspec = pl.BlockSpec((tb, GDN_WIDTH), lambda i: (row0 // tb + i, GDN_CONV_DIM // GDN_WIDTH))
    kern, out_shape, extra, extra_specs, aliases = _shared_rows_out(_gdn_post_kernel, 4, rows, GDN_WIDTH, BF16, dest)
    return pl.pallas_call(
        kern,
        grid=(rows // tb,),
        in_specs=[spec, zspec, pl.BlockSpec((1, GDN_WIDTH), lambda i: (0, 0)),
                  pl.BlockSpec((LANES, LANES), lambda i: (0, 0))] + extra_specs,
        out_specs=pl.BlockSpec((tb, GDN_WIDTH), lambda i: ((0 if dest is None else row0 // tb) + i, 0)),
        out_shape=out_shape,
        input_output_aliases=aliases,
        compiler_params=_cparams(1),
        name="gdn_post",
    )(o, p_gdn, prm["norm_w"], prm["j128"], *extra)


def _split3(x):
    hi = x.astype(BF16)
    r = x - hi.astype(F32)
    mid = r.astype(BF16)
    return hi, mid, (r - mid.astype(F32)).astype(BF16)


def _dotb(a, b):
    return _dot(a.astype(BF16), b.astype(BF16))


def _dot3_packed(x, w, c):
    xh = x.astype(BF16).astype(F32)
    hi_lo = xh + pltpu.roll(x - xh, c, axis=1)
    wh, wl = _split(w)
    lhs = jnp.concatenate([hi_lo, xh], axis=1).astype(BF16)
    rhs = jnp.concatenate([wh, wh, wl, jnp.zeros_like(wh)], axis=0)
    return _dot(lhs, rhs)


def _pad_rows(x, rows):
    return jnp.concatenate([x, jnp.zeros((rows - x.shape[0], x.shape[1]), x.dtype)], axis=0)


def _gdn_chunk_kernel(q_ref, k_ref, v_ref, beta_ref, g_ref, s0_ref, o_ref, s_ref, *, hb, nc):
    c = GDN_CHUNK

    @pl.when(pl.program_id(2) == 0)
    def _():
        s_ref[...] = s0_ref[...]

    row = lax.broadcasted_iota(jnp.int32, (c, LANES), 0)
    col = lax.broadcasted_iota(jnp.int32, (c, LANES), 1)
    incl, strict = row >= col, row > col
    eye = (row == col).astype(F32)
    row3 = lax.broadcasted_iota(jnp.int32, (c, 2 * LANES), 0)
    col3 = lax.broadcasted_iota(jnp.int32, (c, 2 * LANES), 1)
    tri3 = ((row3 >= col3 % c) & (col3 < 3 * c)).astype(BF16)
    lane0_3 = (lax.broadcasted_iota(jnp.int32, (c, 3 * LANES), 1) % LANES == 0).astype(BF16)
    zero_c = jnp.zeros((c, LANES), BF16)
    contract_last = (((1,), (1,)), ((), ()))
    contract_first = (((0,), (0,)), ((), ()))

    units = [(h, ci) for ci in range(nc) for h in range(hb)]
    n_u = len(units)

    def load(ref):
        return [ref[ci * c:(ci + 1) * c, h * LANES:(h + 1) * LANES] for h, ci in units]

    q, k, v, beta, g = load(q_ref), load(k_ref), load(v_ref), load(beta_ref), load(g_ref)
    lanes = lambda x, u: x[:, u * LANES:(u + 1) * LANES]
    g_parts = jnp.concatenate([jnp.concatenate(list(_split3(x)) + [zero_c], axis=0) for x in g], axis=1)
    gc_all = _dot(tri3, g_parts)
    gc = [lanes(gc_all, u) for u in range(n_u)]
    gc_parts = jnp.concatenate([_pad_rows(jnp.concatenate(_split3(x), axis=1), LANES) for x in gc], axis=0)
    gc_row_all = lax.dot_general(lane0_3, gc_parts, contract_last, preferred_element_type=F32)
    dec = [jnp.where(incl, jnp.exp(jnp.where(incl, gc[u] - lanes(gc_row_all, u), 0.0)), 0.0) for u in range(n_u)]
    kb = [k[u] * beta[u] for u in range(n_u)]
    gram = [lax.dot_general(jnp.concatenate([kb[u], q[u]], axis=0).astype(BF16), _pad_rows(k[u].astype(BF16), LANES),
                            contract_last, preferred_element_type=F32) for u in range(n_u)]
    a = [jnp.where(strict, gram[u][:c] * dec[u], 0.0) for u in range(n_u)]
    attn = [jnp.where(incl, gram[u][c:] * dec[u], 0.0)[:, :c].astype(BF16) for u in range(n_u)]
    tinv = [eye - x for x in a]
    p = [_dot3_packed(x, x, c) for x in a]
    n_levels = c.bit_length() - 2
    for lvl in range(n_levels):
        last = lvl + 1 == n_levels
        lhs = tinv if last else [jnp.concatenate([tinv[u], p[u]], axis=0) for u in range(n_u)]
        prod = [_dot3_packed(lhs[u], p[u], c) for u in range(n_u)]
        tinv = [tinv[u] + prod[u][:c] for u in range(n_u)]
        if not last:
            p = [prod[u][c:] for u in range(n_u)]
    eg = [jnp.exp(x) for x in gc]
    uw = [_dotb(tinv[u][:, :c], jnp.concatenate([v[u] * beta[u], kb[u] * eg[u]], axis=1)) for u in range(n_u)]
    w_qe = [jnp.concatenate([uw[u][:, LANES:], q[u] * eg[u]], axis=0).astype(BF16) for u in range(n_u)]
    g_last = [x[c - 1:c, :] for x in gc]
    kd = [(k[u] * jnp.exp(g_last[u] - gc[u])).astype(BF16) for u in range(n_u)]
    eg_last = [jnp.exp(x) for x in g_last]

    s = [s_ref[0, h] for h in range(hb)]
    o_rows = []
    for ci in range(nc):
        us = [ci * hb + h for h in range(hb)]
        wq = [_dot(w_qe[u], s[h].astype(BF16)) for h, u in enumerate(us)]
        v_new = [(uw[u][:, :LANES] - wq[h][:c]).astype(BF16) for h, u in enumerate(us)]
        o_rows.append(jnp.concatenate([wq[h][c:] + _dot(attn[u], v_new[h]) for h, u in enumerate(us)], axis=1))
        s = [s[h] * eg_last[u] + lax.dot_general(kd[u], v_new[h], contract_first, preferred_element_type=F32)
             for h, u in enumerate(us)]
    o_ref[...] = jnp.concatenate(o_rows, axis=0)
    s_ref[0] = jnp.stack(s)


def gdn_chunked(q, k, v, beta, g, s0, n_seq, t_len, hb, nc):
    n_ch = t_len // (GDN_CHUNK * nc)
    spec = pl.BlockSpec((GDN_CHUNK * nc, hb * LANES), lambda b, h, c: (b * n_ch + c, h))
    sspec = pl.BlockSpec((1, hb, GDN_HEAD_DIM, GDN_HEAD_DIM), lambda b, h, c: (b, h, 0, 0))
    return pl.pallas_call(
        functools.partial(_gdn_chunk_kernel, hb=hb, nc=nc),
        grid=(n_seq, GDN_HEADS // hb, n_ch),
        in_specs=[spec] * 5 + [sspec],
        out_specs=[spec, sspec],
        out_shape=[jax.ShapeDtypeStruct(q.shape, F32), jax.ShapeDtypeStruct(s0.shape, F32)],
        compiler_params=_cparams(3),
        name="gdn_chunk",
    )(q, k, v, beta, g, s0)


def rwkv_mix(p_rwkv, shift_state, wkv_state, prm, n_seq, t_len, row0, tb_prep, tb_scan, dest=None):
    st8 = _pad_state_rows(shift_state[:, None, :])
    r, kk, d, b, k, v, g = rwkv_prep(p_rwkv, st8, prm, n_seq, t_len, tb_prep, row0)
    dup = max(1, LANES // (n_seq * RWKV_HEADS))
    s0 = _state_to_layout(wkv_state, dup, value_major=True)
    if dup == 2:
        tl = lambda x: x.reshape(n_seq, t_len * RWKV_HEADS, RWKV_HEAD_DIM)
        y, s1 = dplr_scan(tl(kk), tl(d), tl(b), tl(k), tl(r), tl(v), s0, tb_scan)
        y = y.reshape(n_seq * t_len, RWKV_WIDTH)
    else:
        kl = functools.partial(_to_key_layout, n_seq=n_seq, t_len=t_len, heads=RWKV_HEADS, dj=RWKV_HEAD_DIM, dup=dup)
        vt = _to_value_layout(v, n_seq, t_len, RWKV_HEADS, RWKV_HEAD_DIM, dup)
        y, s1 = dplr_scan(kl(kk), kl(d), kl(b), kl(k), kl(r), vt, s0, tb_scan)
        y = _from_value_layout(y, n_seq, t_len, RWKV_HEADS, RWKV_HEAD_DIM, dup)
    out = rwkv_post(y, r, k, v, g, prm, tb_prep, row0, dest)
    return out, _state_from_layout(s1, n_seq, RWKV_HEADS, dup, value_major=True)


def gdn_mix(p_gdn, p_swa, conv_state, ssm_state, prm, n_seq, t_len, row0, tb_prep, tb_scan, dest=None):
    st8 = _pad_state_rows(conv_state)
    if t_len % GDN_CHUNK == 0:
        kn, g, beta, v, q = gdn_prep(p_gdn, p_swa, st8, prm, n_seq, t_len, tb_prep, row0, chunked=True)
        nc = 4 if t_len % (4 * GDN_CHUNK) == 0 else 1
        o, s1 = gdn_chunked(q, kn, v, beta, g, ssm_state, n_seq, t_len, hb=8, nc=nc)
        return gdn_post(o, p_gdn, prm, tb_prep, row0, dest), s1
    kn, d, b, v, q = gdn_prep(p_gdn, p_swa, st8, prm, n_seq, t_len, tb_prep, row0, chunked=False)
    dup = 2 * max(1, LANES // (2 * n_seq * GDN_HEADS))
    kl = functools.partial(_to_key_layout, n_seq=n_seq, t_len=t_len, heads=GDN_HEADS, dj=GDN_HEAD_DIM, dup=dup)
    vt = _to_value_layout(v, n_seq, t_len, GDN_HEADS, GDN_HEAD_DIM, dup)
    s0 = _state_to_layout(ssm_state, dup, value_major=False)
    knl = kl(kn)
    o, s1 = dplr_scan(knl, kl(d), kl(b), knl, kl(q), vt, s0, tb_scan)
    o = _from_value_layout(o, n_seq, t_len, GDN_HEADS, GDN_HEAD_DIM, dup)
    out = gdn_post(o, p_gdn, prm, tb_prep, row0, dest)
    return out, _state_from_layout(s1, n_seq, GDN_HEADS, dup, value_major=False)


def _block_ones(seg):
    i = jnp.arange(LANES)
    return (i[:, None] // seg == i[None, :] // seg).astype(BF16)


def _head_expander(first_row):
    rows = jnp.arange(LANES)[:, None]
    cols = jnp.arange(GDN_WIDTH)[None, :] // GDN_HEAD_DIM
    return (rows == cols + first_row).astype(BF16)


def _layer_params(l, W):
    row = lambda x: x.reshape(1, -1)
    zeros = jnp.zeros((DECAY_LORA, RWKV_WIDTH), F32)
    wwa = jnp.concatenate([jnp.concatenate([W["rwkv_w2"][l], zeros], axis=1),
                           jnp.concatenate([zeros, W["rwkv_a2"][l]], axis=1)], axis=0)
    lane_row = lambda x, off: jnp.pad(x, (off, LANES - off - x.shape[0])).reshape(1, LANES)
    rwkv = dict(mix=row(W["rwkv_shift_mix"][l]), w0=row(W["rwkv_w0"][l]), a0=row(W["rwkv_a0"][l]), wwa=wwa,
                g2=W["rwkv_g2"][l], k_k=row(W["rwkv_k_k"][l]), k_a=row(W["rwkv_k_a"][l]),
                r_k=row(W["rwkv_r_k"][l]), ln_w=row(W["rwkv_ln_w"][l]), ln_b=row(W["rwkv_ln_b"][l]),
                j64=_block_ones(RWKV_HEAD_DIM))
    gdn = dict(conv_w=jnp.pad(W["gdn_conv_w"][l], ((0, SUBLANES - CONV_WIDTH), (0, 0))),
               a_log=lane_row(W["gdn_A_log"][l], GDN_HEADS), dt=lane_row(W["gdn_dt_bias"][l], GDN_HEADS),
               e_beta=_head_expander(0), e_gate=_head_expander(GDN_HEADS),
               norm_w=jnp.tile(W["gdn_norm_w"][l], GDN_HEADS).reshape(1, GDN_WIDTH), j128=_block_ones(LANES))
    return rwkv, gdn


def kernel(x_prompt, x_sample, state_rwkv_shift, state_rwkv_wkv, cache_swa_k, cache_swa_v, state_gdn_conv,
           state_gdn_ssm, norm_mix, w_in, rwkv_shift_mix, rwkv_w0, rwkv_w2, rwkv_a0, rwkv_a2, rwkv_g2, rwkv_k_k,
           rwkv_k_a, rwkv_r_k, rwkv_ln_w, rwkv_ln_b, swa_sinks, gdn_conv_w, gdn_A_log, gdn_dt_bias, gdn_norm_w,
           w_out, norm_ffn, w_gate, w_up, w_down, final_norm):
    W = dict(rwkv_shift_mix=rwkv_shift_mix, rwkv_w0=rwkv_w0, rwkv_w2=rwkv_w2, rwkv_a0=rwkv_a0, rwkv_a2=rwkv_a2,
             rwkv_g2=rwkv_g2, rwkv_k_k=rwkv_k_k, rwkv_k_a=rwkv_k_a, rwkv_r_k=rwkv_r_k, rwkv_ln_w=rwkv_ln_w,
             rwkv_ln_b=rwkv_ln_b, gdn_conv_w=gdn_conv_w, gdn_A_log=gdn_A_log, gdn_dt_bias=gdn_dt_bias,
             gdn_norm_w=gdn_norm_w)
    w_in_b, w_out_b, w_gate_b, w_up_b, w_down_b = (w.astype(BF16) for w in (w_in, w_out, w_gate, w_up, w_down))
    swa_col, gdn_col, ba_col = RWKV_PROJ, RWKV_PROJ + SWA_PROJ, RWKV_PROJ + SWA_PROJ + GDN_MAIN

    u, h = rmsnorm_join(x_prompt.reshape(ROWS_P, D_MODEL), x_sample.reshape(ROWS_D, D_MODEL), norm_mix[0], BF16)
    zero_shift = jnp.zeros((BATCH, RWKV_PROJ), F32)
    zero_wkv = jnp.zeros((BATCH, RWKV_HEADS, RWKV_HEAD_DIM, RWKV_HEAD_DIM), F32)
    zero_conv = jnp.zeros((BATCH, CONV_WIDTH - 1, GDN_CONV_DIM), F32)
    zero_ssm = jnp.zeros((BATCH, GDN_HEADS, GDN_HEAD_DIM, GDN_HEAD_DIM), F32)
    wb = cache_swa_k.shape[2]
    assert wb == WINDOW
    outs = [[] for _ in range(12)]
    bm = 1056
    for l in range(DEPTH):
        rw, gd = _layer_params(l, W)
        if l > 0:
            u = rmsnorm(h, norm_mix[l], BF16)
        p_rwkv = matmul(u, w_in_b, l, 2 * bm, 256, 0, RWKV_PROJ)
        p_swa = matmul(u, w_in_b, l, 2 * bm, 256, swa_col, SWA_PROJ)
        p_gdn = matmul(u, w_in_b, l, 2 * bm, 512, gdn_col, GDN_MAIN)
        p_ba = matmul(u, w_in_b, l, bm, LANES, ba_col, LANES)

        o_r, wkv_p = rwkv_mix(p_rwkv, zero_shift, zero_wkv, rw, BATCH, SEQ, 0, 128, 32, dest=ROWS)
        o_r, wkv_d = rwkv_mix(p_rwkv, state_rwkv_shift[l], state_rwkv_wkv[l], rw, DEC_BATCH, DEC_SEQ, ROWS_P,
                              DEC_SEQ, DEC_SEQ, dest=o_r)
        o_s = swa_attention(swa_sinks[l], p_swa, BATCH, SEQ, WINDOW, 0, dest=ROWS)
        o_s = swa_attention(swa_sinks[l], p_swa, DEC_BATCH, DEC_SEQ, DEC_SEQ, ROWS_P,
                            cache_swa_k[l].reshape(DEC_BATCH * wb, SWA_KV_WIDTH),
                            cache_swa_v[l].reshape(DEC_BATCH * wb, SWA_KV_WIDTH), dest=o_s)
        o_g, ssm_p = gdn_mix(p_gdn, p_ba, zero_conv, zero_ssm, gd, BATCH, SEQ, 0, 64, 32, dest=ROWS)
        o_g, ssm_d = gdn_mix(p_gdn, p_ba, state_gdn_conv[l], state_gdn_ssm[l], gd, DEC_BATCH, DEC_SEQ, ROWS_P,
                             DEC_SEQ, DEC_SEQ, dest=o_g)
        h = matmul_residual([o_r, o_s, o_g], w_out_b, l, h, bm, 512, [RWKV_WIDTH, SWA_WIDTH, GDN_WIDTH])
        u = rmsnorm(h, norm_ffn[l], BF16)
        h1 = swiglu_matmul(u, w_gate_b, w_up_b, l, bm, 512)
        h = matmul_residual([h1], w_down_b, l, h, bm, 512, [D_FF // 2])

        def tail_p(p, n_rows, c0, c1):
            return jnp.stack([lax.slice(p, ((b + 1) * SEQ - n_rows, c0), ((b + 1) * SEQ, c1)) for b in range(BATCH)])

        def tail_d(p, n_rows, c0, c1):
            x = lax.slice(p, (ROWS_P, c0), (ROWS, c1)).reshape(DEC_BATCH, DEC_SEQ, c1 - c0)
            return x[:, DEC_SEQ - n_rows:]

        kv = lambda x: x.reshape(x.shape[0], x.shape[1], SWA_KV_HEADS, SWA_HEAD_DIM)
        k0, k1, v1 = SWA_WIDTH, SWA_WIDTH + SWA_KV_WIDTH, SWA_PROJ
        layer_out = (
            tail_p(p_rwkv, 1, 0, RWKV_PROJ)[:, 0], wkv_p, kv(tail_p(p_swa, wb, k0, k1)), kv(tail_p(p_swa, wb, k1, v1)),
            tail_p(p_gdn, CONV_WIDTH - 1, 0, GDN_CONV_DIM), ssm_p,
            tail_d(p_rwkv, 1, 0, RWKV_PROJ)[:, 0], wkv_d,
            jnp.concatenate([cache_swa_k[l], kv(tail_d(p_swa, DEC_SEQ, k0, k1))], axis=1)[:, -wb:],
            jnp.concatenate([cache_swa_v[l], kv(tail_d(p_swa, DEC_SEQ, k1, v1))], axis=1)[:, -wb:],
            tail_d(p_gdn, CONV_WIDTH - 1, 0, GDN_CONV_DIM), ssm_d)
        for lst, t in zip(outs, layer_out):
            lst.append(t)
    y = rmsnorm(h, final_norm, F32)
    y_prompt = y[:ROWS_P].reshape(BATCH, SEQ, D_MODEL)
    y_sample = y[ROWS_P:].reshape(DEC_BATCH, DEC_SEQ, D_MODEL)
    return (y_prompt, y_sample) + tuple(jnp.stack(lst) for lst in outs)
```
